```python
import math
import jax, jax.numpy as jnp
from jax import lax
import numpy as np

D_MODEL = 2048
BATCH = 4
SEQ = 4096
DEPTH = 2

SSM_WIDTH = D_MODEL // 4
SSM_CH = 16
SSM_GROUPS = SSM_WIDTH // SSM_CH
SSM_STATE = 64
DIFF_WIDTH = 3 * D_MODEL // 8
DIFF_V_DIM = 128
DIFF_HEADS = DIFF_WIDTH // DIFF_V_DIM
DIFF_QK_DIM = DIFF_V_DIM // 2
GDN_WIDTH = D_MODEL - SSM_WIDTH - DIFF_WIDTH
GDN_HEAD_DIM = 128
GDN_HEADS = GDN_WIDTH // GDN_HEAD_DIM
CONV_WIDTH = 4
GDN_CHUNK = 64
ROPE_THETA = 500000.0
ROPE_DIM = DIFF_QK_DIM // 4
Q_BLOCK = 128
D_FF = 5632
N_EXPERTS = 8
TOP_K = 2
D_FF_EXPERT = 7168
N_DENSE = (DEPTH + 1) // 2
N_MOE = DEPTH // 2
IN_PROJ_WIDTH = SSM_WIDTH + 3 * DIFF_WIDTH + 4 * GDN_WIDTH + 2 * GDN_HEADS
EPS = 1e-6
F32 = jnp.float32

kernel_name = 'hybrid_s5_diffattn_gdn_moe_block'


def rmsnorm(x, w):
    xf = x.astype(F32)
    return xf * lax.rsqrt(jnp.mean(xf * xf, axis=-1, keepdims=True) + EPS) * w.astype(F32)


def l2norm(t):
    return t * lax.rsqrt(jnp.sum(t * t, axis=-1, keepdims=True) + EPS)


def s5_mixer(u, a_re, a_im, log_dt, b_re, b_im, c_re, c_im, d_skip, w_glu):
    bsz, seq, _ = u.shape
    ug = u.astype(F32).reshape(bsz, seq, SSM_GROUPS, SSM_CH)
    lam = lax.complex(a_re.astype(F32), a_im.astype(F32))
    dt = jnp.exp(log_dt.astype(F32))[:, None]
    lam_bar = jnp.exp(lam * dt)
    b_mat = lax.complex(b_re.astype(F32), b_im.astype(F32))
    b_bar = ((lam_bar - 1.0) / lam)[:, :, None] * b_mat
    bu = jnp.einsum('gph,bsgh->bsgp', b_bar, ug.astype(jnp.complex64))
    lam_seq = jnp.broadcast_to(lam_bar, bu.shape)

    def combine(e1, e2):
        a1, s1 = e1
        a2, s2 = e2
        return a1 * a2, a2 * s1 + s2

    _, states = lax.associative_scan(combine, (lam_seq, bu), axis=1)
    c_mat = lax.complex(c_re.astype(F32), c_im.astype(F32))
    y = jnp.real(jnp.einsum('ghp,bsgp->bsgh', c_mat, states))
    y = y + d_skip.astype(F32).reshape(SSM_GROUPS, SSM_CH) * ug
    y = jax.nn.gelu(y.reshape(bsz, seq, SSM_WIDTH))
    return y * jax.nn.sigmoid(y @ w_glu.astype(F32))


def partial_rope(t, cos, sin):
    half = ROPE_DIM // 2
    t1, t2, rest = t[..., :half], t[..., half:ROPE_DIM], t[..., ROPE_DIM:]
    return jnp.concatenate([t1 * cos - t2 * sin, t2 * cos + t1 * sin, rest], axis=-1)


def diff_attention(q, k, v, positions, lam_q1, lam_k1, lam_q2, lam_k2, subln_w, lambda_init):
    bsz, seq = q.shape[:2]
    inv_freq = ROPE_THETA ** (-jnp.arange(0, ROPE_DIM, 2, dtype=F32) / ROPE_DIM)
    ang = positions.astype(F32)[:, :, None] * inv_freq
    cos = jnp.cos(ang)[:, :, None, None, :]
    sin = jnp.sin(ang)[:, :, None, None, :]
    q = partial_rope(q, cos, sin) * (DIFF_QK_DIM ** -0.5)
    k = partial_rope(k, cos, sin)
    lam = (jnp.exp(jnp.sum(lam_q1.astype(F32) * lam_k1.astype(F32)))
           - jnp.exp(jnp.sum(lam_q2.astype(F32) * lam_k2.astype(F32))) + lambda_init)
    n_blocks = seq // Q_BLOCK
    q_blocks = jnp.moveaxis(q.reshape(bsz, n_blocks, Q_BLOCK, DIFF_HEADS, 2, DIFF_QK_DIM), 1, 0)
    key_idx = jnp.arange(seq)

    def attend(args):
        q_blk, blk = args
        s = jnp.einsum('bqhmd,bkhmd->bhmqk', q_blk, k)
        q_idx = blk * Q_BLOCK + jnp.arange(Q_BLOCK)
        s = jnp.where(key_idx[None, :] <= q_idx[:, None], s, -jnp.inf)
        p = jax.nn.softmax(s, axis=-1)
        w = p[:, :, 0] - lam * p[:, :, 1]
        return jnp.einsum('bhqk,bkhd->bqhd', w, v)

    o = lax.map(attend, (q_blocks, jnp.arange(n_blocks)))
    o = jnp.moveaxis(o, 0, 1).reshape(bsz, seq, DIFF_HEADS, DIFF_V_DIM)
    o = rmsnorm(o, subln_w) * (1.0 - lambda_init)
    return o.reshape(bsz, seq, DIFF_WIDTH)


def causal_short_conv(t, w):
    ch = t.shape[-1]
    y = lax.conv_general_dilated(t, w.astype(t.dtype)[:, None, :], window_strides=(1,),
                                 padding=[(CONV_WIDTH - 1, 0)],
                                 dimension_numbers=('NWC', 'WIO', 'NWC'),
                                 feature_group_count=ch)
    return jax.nn.silu(y)


def gated_delta_net(q, k, v, a, b, z, a_log, dt_bias, gnorm_w):
    bsz, seq = q.shape[:2]
    n_chunks = seq // GDN_CHUNK
    q = l2norm(q) * (GDN_HEAD_DIM ** -0.5)
    k = l2norm(k)
    beta = jax.nn.sigmoid(b)
    g = -jnp.exp(a_log.astype(F32)) * jax.nn.softplus(a + dt_bias.astype(F32))

    def chunks(t):
        t = t.reshape((bsz, n_chunks, GDN_CHUNK) + t.shape[2:])
        return jnp.moveaxis(t, 3, 1)

    qc, kc, vc = chunks(q), chunks(k), chunks(v)
    bc, gc = chunks(beta), chunks(g)
    g_cum = jnp.cumsum(gc, axis=-1)
    idx = jnp.arange(GDN_CHUNK)
    incl = idx[:, None] >= idx[None, :]
    strict = idx[:, None] > idx[None, :]
    decay = jnp.exp(jnp.where(incl, g_cum[..., :, None] - g_cum[..., None, :], -jnp.inf))
    k_beta = kc * bc[..., None]
    v_beta = vc * bc[..., None]
    lower = jnp.where(strict, jnp.einsum('bhncd,bhned->bhnce', k_beta, kc) * decay, 0.0)
    eye = jnp.eye(GDN_CHUNK, dtype=F32)
    t_mat = lax.linalg.triangular_solve(eye + lower, jnp.broadcast_to(eye, lower.shape),
                                        left_side=True, lower=True, unit_diagonal=True)
    u = t_mat @ v_beta
    w = t_mat @ (k_beta * jnp.exp(g_cum)[..., None])
    qk = jnp.where(incl, jnp.einsum('bhncd,bhned->bhnce', qc, kc) * decay, 0.0)

    def step(state, inp):
        q_i, k_i, u_i, w_i, g_i, qk_i = inp
        v_new = u_i - w_i @ state
        o_i = (q_i * jnp.exp(g_i)[..., None]) @ state + qk_i @ v_new
        g_last = g_i[..., -1:]
        state = (state * jnp.exp(g_last)[..., None]
                 + jnp.einsum('bhcd,bhce->bhde', k_i * jnp.exp(g_last - g_i)[..., None], v_new))
        return state, o_i

    xs = tuple(jnp.moveaxis(t, 2, 0) for t in (qc, kc, u, w, g_cum, qk))
    state0 = jnp.zeros((bsz, GDN_HEADS, GDN_HEAD_DIM, GDN_HEAD_DIM), F32)
    _, o = lax.scan(step, state0, xs)
    o = jnp.moveaxis(o, 0, 2).reshape(bsz, GDN_HEADS, seq, GDN_HEAD_DIM)
    o = jnp.moveaxis(o, 1, 2)
    o = rmsnorm(o, gnorm_w) * jax.nn.silu(z)
    return o.reshape(bsz, seq, GDN_WIDTH)


def hybrid_mixer(h, positions, layer_idx, w_in, w_out,
                 ssm_a_re, ssm_a_im, ssm_log_dt, ssm_b_re, ssm_b_im, ssm_c_re, ssm_c_im, ssm_d, ssm_w_glu,
                 diff_lam_q1, diff_lam_k1, diff_lam_q2, diff_lam_k2, diff_subln,
                 gdn_conv, gdn_a_log, gdn_dt_bias, gdn_norm):
    bsz, seq, _ = h.shape
    proj = (h @ w_in.astype(F32)).astype(F32)
    sizes = [SSM_WIDTH, DIFF_WIDTH, DIFF_WIDTH, DIFF_WIDTH, 3 * GDN_WIDTH, GDN_WIDTH, GDN_HEADS, GDN_HEADS]
    cuts = [int(s) for s in np.cumsum(sizes)[:-1]]
    u, dq, dk, dv, gqkv, gz, ga, gb = jnp.split(proj, cuts, axis=-1)
    y_ssm = s5_mixer(u, ssm_a_re, ssm_a_im, ssm_log_dt, ssm_b_re, ssm_b_im,
                     ssm_c_re, ssm_c_im, ssm_d, ssm_w_glu)
    lambda_init = 0.8 - 0.6 * math.exp(-0.3 * layer_idx)
    y_diff = diff_attention(dq.reshape(bsz, seq, DIFF_HEADS, 2, DIFF_QK_DIM),
                            dk.reshape(bsz, seq, DIFF_HEADS, 2, DIFF_QK_DIM),
                            dv.reshape(bsz, seq, DIFF_HEADS, DIFF_V_DIM), positions,
                            diff_lam_q1, diff_lam_k1, diff_lam_q2, diff_lam_k2, diff_subln, lambda_init)
    gq, gk, gv = jnp.split(causal_short_conv(gqkv, gdn_conv), 3, axis=-1)
    hs = (bsz, seq, GDN_HEADS, GDN_HEAD_DIM)
    y_gdn = gated_delta_net(gq.reshape(hs), gk.reshape(hs), gv.reshape(hs), ga, gb, gz.reshape(hs),
                            gdn_a_log, gdn_dt_bias, gdn_norm)
    return jnp.concatenate([y_ssm, y_diff, y_gdn], axis=-1) @ w_out.astype(F32)


def swiglu(h, w1, w3, w2):
    return (jax.nn.silu(h @ w1.astype(F32)) * (h @ w3.astype(F32))) @ w2.astype(F32)


def moe_swiglu(h, w_router, w1, w3, w2):
    bsz, seq, d = h.shape
    t = h.reshape(bsz * seq, d)
    logits = t @ w_router.astype(F32)
    top_val, top_idx = lax.top_k(logits, TOP_K)
    gates = jax.nn.softmax(top_val, axis=-1)
    combine = jnp.einsum('tk,tke->te', gates, jax.nn.one_hot(top_idx, N_EXPERTS, dtype=F32))
    out = jnp.zeros_like(t)
    for e in range(N_EXPERTS):
        out = out + combine[:, e:e + 1] * swiglu(t, w1[e], w3[e], w2[e])
    return out.reshape(bsz, seq, d)


def setup_inputs(seed: int = 0) -> dict:
    key = jax.random.key(seed)
    ks = jax.random.split(key, 40)

    def nrm(k, shape, scale):
        return scale * jax.random.normal(k, shape, F32)

    x = nrm(ks[0], (BATCH, SEQ, D_MODEL), 1.0)
    c = nrm(ks[1], (BATCH, D_MODEL), 1.0)
    positions = (jax.random.randint(ks[2], (BATCH, 1), 0, 2048, dtype=jnp.int32)
                 + jnp.arange(SEQ, dtype=jnp.int32)[None, :])
    w_ada = nrm(ks[3], (DEPTH, D_MODEL, 6 * D_MODEL), 0.5 * D_MODEL ** -0.5)
    b_ada = nrm(ks[4], (DEPTH, 6 * D_MODEL), 0.01)
    norm_mix = 1.0 + nrm(ks[5], (DEPTH, D_MODEL), 0.01)
    norm_ffn = 1.0 + nrm(ks[6], (DEPTH, D_MODEL), 0.01)
    norm_final = 1.0 + nrm(ks[7], (D_MODEL,), 0.01)
    w_in = nrm(ks[8], (DEPTH, D_MODEL, IN_PROJ_WIDTH), D_MODEL ** -0.5)
    w_out = nrm(ks[9], (DEPTH, D_MODEL, D_MODEL), D_MODEL ** -0.5)
    ssm_a_re = -0.5 + nrm(ks[10], (DEPTH, SSM_GROUPS, SSM_STATE), 0.01)
    ssm_a_im = math.pi * jnp.arange(SSM_STATE, dtype=F32) + nrm(ks[11], (DEPTH, SSM_GROUPS, SSM_STATE), 0.01)
    ssm_log_dt = jax.random.uniform(ks[12], (DEPTH, SSM_GROUPS), F32, math.log(1e-3), math.log(1e-1))
    ssm_b_re = nrm(ks[13], (DEPTH, SSM_GROUPS, SSM_STATE, SSM_CH), (2 * SSM_CH) ** -0.5)
    ssm_b_im = nrm(ks[14], (DEPTH, SSM_GROUPS, SSM_STATE, SSM_CH), (2 * SSM_CH) ** -0.5)
    ssm_c_re = nrm(ks[15], (DEPTH, SSM_GROUPS, SSM_CH, SSM_STATE), 0.5)
    ssm_c_im = nrm(ks[16], (DEPTH, SSM_GROUPS, SSM_CH, SSM_STATE), 0.5)
    ssm_d = nrm(ks[17], (DEPTH, SSM_WIDTH), 1.0)
    ssm_w_glu = nrm(ks[18], (DEPTH, SSM_WIDTH, SSM_WIDTH), SSM_WIDTH ** -0.5)
    diff_lam_q1 = nrm(ks[19], (DEPTH, DIFF_QK_DIM), 0.1)
    diff_lam_k1 = nrm(ks[20], (DEPTH, DIFF_QK_DIM), 0.1)
    diff_lam_q2 = nrm(ks[21], (DEPTH, DIFF_QK_DIM), 0.1)
    diff_lam_k2 = nrm(ks[22], (DEPTH, DIFF_QK_DIM), 0.1)
    diff_subln = 1.0 + nrm(ks[23], (DEPTH, DIFF_V_DIM), 0.01)
    gdn_conv = nrm(ks[24], (DEPTH, CONV_WIDTH, 3 * GDN_WIDTH), CONV_WIDTH ** -0.5)
    gdn_a_log = jnp.log(jax.random.uniform(ks[25], (DEPTH, GDN_HEADS), F32, 1.0, 16.0))
    dt0 = jnp.exp(jax.random.uniform(ks[26], (DEPTH, GDN_HEADS), F32, math.log(1e-3), math.log(1e-1)))
    gdn_dt_bias = dt0 + jnp.log(-jnp.expm1(-dt0))
    gdn_norm = 1.0 + nrm(ks[27], (DEPTH, GDN_HEAD_DIM), 0.01)
    ffn_w1 = nrm(ks[28], (N_DENSE, D_MODEL, D_FF), D_MODEL ** -0.5)
    ffn_w3 = nrm(ks[29], (N_DENSE, D_MODEL, D_FF), D_MODEL ** -0.5)
    ffn_w2 = nrm(ks[30], (N_DENSE, D_FF, D_MODEL), D_FF ** -0.5)
    moe_router = nrm(ks[31], (N_MOE, D_MODEL, N_EXPERTS), D_MODEL ** -0.5)
    moe_w1 = nrm(ks[32], (N_MOE, N_EXPERTS, D_MODEL, D_FF_EXPERT), D_MODEL ** -0.5)
    moe_w3 = nrm(ks[33], (N_MOE, N_EXPERTS, D_MODEL, D_FF_EXPERT), D_MODEL ** -0.5)
    moe_w2 = nrm(ks[34], (N_MOE, N_EXPERTS, D_FF_EXPERT, D_MODEL), D_FF_EXPERT ** -0.5)
    return {'x': x, 'c': c, 'positions': positions, 'w_ada': w_ada, 'b_ada': b_ada,
            'norm_mix': norm_mix, 'norm_ffn': norm_ffn, 'norm_final': norm_final,
            'w_in': w_in, 'w_out': w_out,
            'ssm_a_re': ssm_a_re, 'ssm_a_im': ssm_a_im, 'ssm_log_dt': ssm_log_dt,
            'ssm_b_re': ssm_b_re, 'ssm_b_im': ssm_b_im, 'ssm_c_re': ssm_c_re, 'ssm_c_im': ssm_c_im,
            'ssm_d': ssm_d, 'ssm_w_glu': ssm_w_glu,
            'diff_lam_q1': diff_lam_q1, 'diff_lam_k1': diff_lam_k1,
            'diff_lam_q2': diff_lam_q2, 'diff_lam_k2': diff_lam_k2, 'diff_subln': diff_subln,
            'gdn_conv': gdn_conv, 'gdn_a_log': gdn_a_log, 'gdn_dt_bias': gdn_dt_bias, 'gdn_norm': gdn_norm,
            'ffn_w1': ffn_w1, 'ffn_w3': ffn_w3, 'ffn_w2': ffn_w2,
            'moe_router': moe_router, 'moe_w1': moe_w1, 'moe_w3': moe_w3, 'moe_w2': moe_w2}


def reference(x, c, positions, w_ada, b_ada, norm_mix, norm_ffn, norm_final, w_in, w_out,
              ssm_a_re, ssm_a_im, ssm_log_dt, ssm_b_re, ssm_b_im, ssm_c_re, ssm_c_im, ssm_d, ssm_w_glu,
              diff_lam_q1, diff_lam_k1, diff_lam_q2, diff_lam_k2, diff_subln,
              gdn_conv, gdn_a_log, gdn_dt_bias, gdn_norm,
              ffn_w1, ffn_w3, ffn_w2, moe_router, moe_w1, moe_w3, moe_w2):
    h_res = x.astype(F32)
    cond = jax.nn.silu(c.astype(F32))
    for l in range(DEPTH):
        mod = cond @ w_ada[l].astype(F32) + b_ada[l].astype(F32)
        shift1, scale1, gate1, shift2, scale2, gate2 = jnp.split(mod[:, None, :], 6, axis=-1)
        hn = rmsnorm(h_res, norm_mix[l]) * (1.0 + scale1) + shift1
        mix = hybrid_mixer(hn, positions, l, w_in[l], w_out[l],
                           ssm_a_re[l], ssm_a_im[l], ssm_log_dt[l], ssm_b_re[l], ssm_b_im[l],
                           ssm_c_re[l], ssm_c_im[l], ssm_d[l], ssm_w_glu[l],
                           diff_lam_q1[l], diff_lam_k1[l], diff_lam_q2[l], diff_lam_k2[l], diff_subln[l],
                           gdn_conv[l], gdn_a_log[l], gdn_dt_bias[l], gdn_norm[l])
        h_res = h_res + gate1 * mix
        hn = rmsnorm(h_res, norm_ffn[l]) * (1.0 + scale2) + shift2
        if l % 2 == 0:
            f = swiglu(hn, ffn_w1[l // 2], ffn_w3[l // 2], ffn_w2[l // 2])
        else:
            f = moe_swiglu(hn, moe_router[l // 2], moe_w1[l // 2], moe_w3[l // 2], moe_w2[l // 2])
        h_res = h_res + gate2 * f
    return rmsnorm(h_res, norm_final).astype(x.dtype)
```

```python
import functools
import math

import numpy as np
import jax
import jax.numpy as jnp
from jax import lax
from jax.experimental import pallas as pl
from jax.experimental.pallas import tpu as pltpu

F32 = jnp.float32
BF16 = jnp.bfloat16
HIGHEST = lax.Precision.HIGHEST

D_MODEL = 2048
SSM_WIDTH = 512
SSM_CH = 16
SSM_GROUPS = SSM_WIDTH // SSM_CH
SSM_STATE = 64
SSM_CHUNK = 64
DIFF_WIDTH = 768
HEAD_DIM = 128
DIFF_HEADS = DIFF_WIDTH // HEAD_DIM
DIFF_QK_DIM = HEAD_DIM // 2
GDN_WIDTH = 768
GDN_HEADS = GDN_WIDTH // HEAD_DIM
CONV_WIDTH = 4
GDN_CHUNK = 64
ROPE_THETA = 500000.0
ROPE_DIM = DIFF_QK_DIM // 4
ROPE_HALF = ROPE_DIM // 2
N_EXPERTS = 8
EPS = 1e-6
LANES = 128
SUBLANES = 8

SEG = 768
SEG_GQ, SEG_GK, SEG_GV, SEG_GZ, SEG_DQ, SEG_DK, SEG_DV, SEG_U = range(8)
PROJ_WIDTH = 8 * SEG
AB_OFF = SSM_WIDTH

VMEM_LIMIT = 56 * 1024 * 1024


def _cparams(*sem):
    return pltpu.CompilerParams(dimension_semantics=sem, vmem_limit_bytes=VMEM_LIMIT)


def _dot(a, b):
    return jnp.dot(a, b, preferred_element_type=F32)


def _dot_nt(a, b):
    return lax.dot_general(a, b, (((1,), (1,)), ((), ())), preferred_element_type=F32)


def _dot_tn(a, b):
    return lax.dot_general(a, b, (((0,), (0,)), ((), ())), preferred_element_type=F32)


def _sigmoid(x):
    return 1.0 / (1.0 + jnp.exp(-x))


def _silu(x):
    return x * _sigmoid(x)


def _ada_kernel(c_ref, w_ref, b_ref, o_ref):
    cond = _silu(c_ref[...])
    o_ref[...] = _dot(cond.astype(BF16), w_ref[...].astype(BF16)) + b_ref[...]


def _ada(c_pad, w_ada, b_ada):
    depth, d, n = w_ada.shape
    tn = 1024
    return pl.pallas_call(
        _ada_kernel,
        grid=(depth, n // tn),
        in_specs=[pl.BlockSpec((SUBLANES, d), lambda l, j: (0, 0)),
                  pl.BlockSpec((None, d, tn), lambda l, j: (l, 0, j)),
                  pl.BlockSpec((None, 1, tn), lambda l, j: (l, 0, j))],
        out_specs=pl.BlockSpec((None, SUBLANES, tn), lambda l, j: (l, 0, j)),
        out_shape=jax.ShapeDtypeStruct((depth, SUBLANES, n), F32),
        compiler_params=_cparams("arbitrary", "arbitrary"),
        name="ada",
    )(c_pad, w_ada, b_ada.reshape(depth, 1, n))


def _rms_mod(h, w, scale, shift):
    y = h * lax.rsqrt(jnp.mean(h * h, axis=-1, keepdims=True) + EPS) * w
    if scale is not None:
        y = y * (1.0 + scale) + shift
    return y


def _resnorm_kernel(n_delta, row_gated, has_mod, emit_res, *refs):
    refs = list(refs)
    h = refs.pop(0)[...]
    if n_delta:
        delta = None
        for _ in range(n_delta):
            y = refs.pop(0)[...].astype(F32)
            if row_gated:
                y = y * refs.pop(0)[...]
            delta = y if delta is None else delta + y
        h = h + refs.pop(0)[...] * delta
    w = refs.pop(0)[...]
    scale = shift = None
    if has_mod:
        scale = refs.pop(0)[...]
        shift = refs.pop(0)[...]
    if emit_res:
        refs.pop(0)[...] = h
    o_ref = refs.pop(0)
    o_ref[...] = _rms_mod(h, w, scale, shift).astype(o_ref.dtype)


def _resnorm(h_res, seq, deltas, gate, norm_w, scale, shift, emit_res, out_dtype):
    t, d = h_res.shape
    tm = min(256, t)
    row = pl.BlockSpec((tm, d), lambda i: (i, 0))
    per_batch = pl.BlockSpec((None, 1, d), lambda i: ((i * tm) // seq, 0, 0))
    row_gated = bool(deltas) and deltas[0][1] is not None
    args, specs = [h_res], [row]
    for y, rg in deltas:
        args.append(y)
        specs.append(row)
        if row_gated:
            args.append(rg)
            specs.append(pl.BlockSpec((tm, 1), lambda i: (i, 0)))
    if deltas:
        args.append(gate)
        specs.append(per_batch)
    args.append(norm_w.reshape(1, d))
    specs.append(pl.BlockSpec((1, d), lambda i: (0, 0)))
    if scale is not None:
        args += [scale, shift]
        specs += [per_batch, per_batch]
    out_shape, out_specs = [], []
    if emit_res:
        out_shape.append(jax.ShapeDtypeStruct((t, d), F32))
        out_specs.append(row)
    out_shape.append(jax.ShapeDtypeStruct((t, d), out_dtype))
    out_specs.append(row)
    res = pl.pallas_call(
        functools.partial(_resnorm_kernel, len(deltas), row_gated, scale is not None, emit_res),
        grid=(t // tm,), in_specs=specs, out_specs=out_specs, out_shape=out_shape,
        compiler_params=_cparams("parallel"), name="resnorm",
    )(*args)
    return res if emit_res else res[0]


def _matmul_kernel(x_ref, w_ref, o_ref):
    o_ref[...] = _dot(x_ref[...], w_ref[...]).astype(o_ref.dtype)


def _matmul(x, w, out_dtype, tm, tn):
    m, k = x.shape
    n = w.shape[1]
    tm, tn = min(tm, m), min(tn, n)
    return pl.pallas_call(
        _matmul_kernel,
        grid=(n // tn, m // tm),
        in_specs=[pl.BlockSpec((tm, k), lambda j, i: (i, 0)),
                  pl.BlockSpec((k, tn), lambda j, i: (0, j))],
        out_specs=pl.BlockSpec((tm, tn), lambda j, i: (i, j)),
        out_shape=jax.ShapeDtypeStruct((m, n), out_dtype),
        compiler_params=_cparams("parallel", "parallel"), name="in_proj",
    )(x, w)


def _s5_tables(a_re, a_im, log_dt, b_re, b_im, c_re, c_im):
    L, G, P, H = SSM_CHUNK, SSM_GROUPS, SSM_STATE, SSM_CH
    lam = lax.complex(a_re.astype(F32), a_im.astype(F32))
    dt = jnp.exp(log_dt.astype(F32))[:, None]
    lam_bar = jnp.exp(lam * dt)
    b_bar = ((lam_bar - 1.0) / lam)[:, :, None] * lax.complex(b_re.astype(F32), b_im.astype(F32))
    c_mat = lax.complex(c_re.astype(F32), c_im.astype(F32))
    steps = jnp.arange(L + 1, dtype=F32)
    pw = jnp.exp((lam * dt)[:, None, :] * steps[None, :, None])
    kern = jnp.real(jnp.einsum('ghp,gjp,gpi->gjih', c_mat, pw[:, :L], b_bar))
    s_idx = jnp.arange(L)[:, None]
    t_idx = jnp.arange(L)[None, :]
    lag = t_idx - s_idx
    tm = jnp.where((lag >= 0)[None, :, :, None, None], kern[:, jnp.clip(lag, 0, L - 1)], 0.0)
    tm = tm.transpose(0, 1, 3, 2, 4).reshape(G, L * H, L * H)
    zc = pw[:, L - 1 - jnp.arange(L)][:, :, :, None] * b_bar[:, None]
    zc = zc.transpose(0, 1, 3, 2).reshape(G, L * H, P)
    cl = c_mat[:, None] * pw[:, 1:L + 1][:, :, None, :]
    cl = cl.transpose(0, 3, 1, 2).reshape(G, P, L * H)
    a_l = pw[:, L]
    return (tm.astype(BF16), jnp.real(zc).astype(BF16), jnp.imag(zc).astype(BF16),
            jnp.real(cl).astype(BF16), (-jnp.imag(cl)).astype(BF16),
            jnp.real(a_l)[:, None, :], jnp.imag(a_l)[:, None, :])


def _s5_core_kernel(nb, u_ref, tm_ref, zre_ref, zim_ref, yre_ref, yim_ref, are_ref, aim_ref,
                    o_ref, xre_ref, xim_ref):
    u = u_ref[...]
    z_re = _dot(u, zre_ref[...])
    z_im = _dot(u, zim_ref[...])
    a_re = are_ref[...]
    a_im = aim_ref[...]
    n_chunks = u.shape[0] // nb
    s_re = jnp.zeros((nb, SSM_STATE), F32)
    s_im = jnp.zeros((nb, SSM_STATE), F32)
    for c in range(n_chunks):
        xre_ref[c * nb:(c + 1) * nb, :] = s_re
        xim_ref[c * nb:(c + 1) * nb, :] = s_im
        zr = z_re[c * nb:(c + 1) * nb, :]
        zi = z_im[c * nb:(c + 1) * nb, :]
        s_re, s_im = a_re * s_re - a_im * s_im + zr, a_re * s_im + a_im * s_re + zi
    y = _dot(u, tm_ref[...])
    y = y + _dot(xre_ref[...].astype(BF16), yre_ref[...])
    y = y + _dot(xim_ref[...].astype(BF16), yim_ref[...])
    o_ref[...] = y.astype(o_ref.dtype)


def _s5_core(ug, tables, nb):
    g, r, w = ug.shape
    tm, zre, zim, yre, yim, are, aim = tables
    p = SSM_STATE

    def grp(*shape):
        return pl.BlockSpec((None,) + shape, lambda i: (i,) + (0,) * len(shape))

    return pl.pallas_call(
        functools.partial(_s5_core_kernel, nb),
        grid=(g,),
        in_specs=[grp(r, w), grp(w, w), grp(w, p), grp(w, p), grp(p, w), grp(p, w), grp(1, p), grp(1, p)],
        out_specs=grp(r, w),
        out_shape=jax.ShapeDtypeStruct((g, r, w), BF16),
        scratch_shapes=[pltpu.VMEM((r, p), F32), pltpu.VMEM((r, p), F32)],
        compiler_params=_cparams("parallel"), name="s5_core",
    )(ug, tm, zre, zim, yre, yim, are, aim)


def _gelu_tanh(x):
    return 0.5 * x * (1.0 + jnp.tanh(math.sqrt(2.0 / math.pi) * (x + 0.044715 * (x * x * x))))


def _s5_post_kernel(y_ref, u_ref, d_ref, w_ref, o_ref):
    u = u_ref[:, :SSM_WIDTH].astype(F32)
    y = _gelu_tanh(y_ref[...].astype(F32) + d_ref[...] * u)
    o_ref[...] = (y * _sigmoid(_dot(y.astype(BF16), w_ref[...]))).astype(o_ref.dtype)


def _s5_post(y_core, proj, d_skip, w_glu):
    t = y_core.shape[0]
    tm = min(512, t)
    return pl.pallas_call(
        _s5_post_kernel,
        grid=(t // tm,),
        in_specs=[pl.BlockSpec((tm, SSM_WIDTH), lambda i: (i, 0)),
                  pl.BlockSpec((tm, SEG), lambda i: (i, SEG_U)),
                  pl.BlockSpec((1, SSM_WIDTH), lambda i: (0, 0)),
                  pl.BlockSpec((SSM_WIDTH, SSM_WIDTH), lambda i: (0, 0))],
        out_specs=pl.BlockSpec((tm, SSM_WIDTH), lambda i: (i, 0)),
        out_shape=jax.ShapeDtypeStruct((t, SSM_WIDTH), BF16),
        compiler_params=_cparams("parallel"), name="s5_post",
    )(y_core, proj, d_skip.reshape(1, SSM_WIDTH).astype(F32), w_glu)


def _s5_mixer(proj, bsz, seq, tables, d_skip, w_glu):
    L, G, H = SSM_CHUNK, SSM_GROUPS, SSM_CH
    t = bsz * seq
    nc = seq // L
    u = proj[:, SEG_U * SEG:SEG_U * SEG + SSM_WIDTH]
    ug = u.reshape(bsz, nc, L, G, H).transpose(3, 1, 0, 2, 4).reshape(G, nc * bsz, L * H)
    yg = _s5_core(ug, tables, bsz)
    y_core = yg.reshape(G, nc, bsz, L, H).transpose(2, 1, 3, 0, 4).reshape(t, SSM_WIDTH)
    return _s5_post(y_core, proj, d_skip, w_glu)


def _rope_tables(positions):
    inv_freq = ROPE_THETA ** (-jnp.arange(0, ROPE_DIM, 2, dtype=F32) / ROPE_DIM)
    ang = positions.astype(F32).reshape(-1)[:, None] * inv_freq
    cos, sin = jnp.cos(ang), jnp.sin(ang)
    r = np.arange(LANES) % DIFF_QK_DIM
    first = jnp.asarray(r < ROPE_HALF)[None, :]
    second = jnp.asarray((r >= ROPE_HALF) & (r < ROPE_DIM))[None, :]
    idx = jnp.asarray(r % ROPE_HALF)
    cos_l, sin_l = cos[:, idx], sin[:, idx]
    cosf = jnp.where(first | second, cos_l, 1.0)
    sin_a = jnp.where(first, -sin_l, 0.0)
    sin_b = jnp.where(second, sin_l, 0.0)
    return cosf, sin_a, sin_b


def _rope_kernel(x_ref, c_ref, sa_ref, sb_ref, o_ref):
    cosf, sin_a, sin_b = c_ref[...], sa_ref[...], sb_ref[...]
    n_slabs = x_ref.shape[1] // LANES
    for s in range(n_slabs):
        x = x_ref[:, s * LANES:(s + 1) * LANES].astype(F32)
        y = (x * cosf + pltpu.roll(x, LANES - ROPE_HALF, axis=1) * sin_a
             + pltpu.roll(x, ROPE_HALF, axis=1) * sin_b)
        if s < n_slabs // 2:
            y = y * (DIFF_QK_DIM ** -0.5)
        o_ref[:, s * LANES:(s + 1) * LANES] = y.astype(o_ref.dtype)


def _rope(proj, tables):
    t = proj.shape[0]
    tm = min(512, t)
    w = 2 * DIFF_WIDTH
    tab = pl.BlockSpec((tm, LANES), lambda i: (i, 0))
    return pl.pallas_call(
        _rope_kernel,
        grid=(t // tm,),
        in_specs=[pl.BlockSpec((tm, w), lambda i: (i, SEG_DQ // 2)), tab, tab, tab],
        out_specs=pl.BlockSpec((tm, w), lambda i: (i, 0)),
        out_shape=jax.ShapeDtypeStruct((t, w), BF16),
        compiler_params=_cparams("parallel"), name="rope",
    )(proj, *tables)


def _attn_kernel(tq, tk, lambda_init, q_ref, k_ref, v_ref, lq1_ref, lk1_ref, lq2_ref, lk2_ref, sw_ref, o_ref):
    qi = pl.program_id(2)
    q = q_ref[...]
    lane = lax.broadcasted_iota(jnp.int32, q.shape, 1)
    zero = jnp.zeros_like(q)
    qm = (jnp.where(lane < DIFF_QK_DIM, q, zero), jnp.where(lane >= DIFF_QK_DIM, q, zero))

    def block(kv, masked, carry):
        start = pl.multiple_of(kv * tk, tk)
        k = k_ref[pl.ds(start, tk), :]
        v = v_ref[pl.ds(start, tk), :]
        new = []
        for m in range(2):
            m_old, l_old, acc = carry[m]
            s = _dot_nt(qm[m], k)
            if masked:
                row = lax.broadcasted_iota(jnp.int32, s.shape, 0)
                col = lax.broadcasted_iota(jnp.int32, s.shape, 1)
                s = jnp.where(col <= row, s, -jnp.inf)
            m_new = jnp.maximum(m_old, jnp.max(s, axis=-1, keepdims=True))
            alpha = jnp.exp(m_old - m_new)
            p = jnp.exp(s - m_new)
            l_new = alpha * l_old + jnp.sum(p, axis=-1, keepdims=True)
            acc = alpha * acc + _dot(p.astype(BF16), v)
            new.append((m_new, l_new, acc))
        return tuple(new)

    init = tuple((jnp.full((tq, 1), -jnp.inf, F32), jnp.zeros((tq, 1), F32), jnp.zeros((tq, HEAD_DIM), F32))
                 for _ in range(2))
    n_full = qi * (tq // tk)
    carry = lax.fori_loop(0, n_full, lambda kv, c: block(kv, False, c), init)
    for d in range(tq // tk):
        carry = block(n_full + d, True, carry)
    (_, l1, acc1), (_, l2, acc2) = carry
    lam = (jnp.exp(jnp.sum(lq1_ref[...] * lk1_ref[...], axis=-1, keepdims=True))
           - jnp.exp(jnp.sum(lq2_ref[...] * lk2_ref[...], axis=-1, keepdims=True)) + lambda_init)
    o = acc1 / l1 - lam * (acc2 / l2)
    o = o * lax.rsqrt(jnp.mean(o * o, axis=-1, keepdims=True) + EPS) * sw_ref[...] * (1.0 - lambda_init)
    o_ref[...] = o.astype(o_ref.dtype)


def _diff_attention(qk, proj, bsz, seq, lam_params, subln_w, lambda_init):
    tq = tk = min(512, seq)
    nh = DIFF_HEADS
    qk3 = qk.reshape(bsz, seq, 2 * DIFF_WIDTH)
    proj3 = proj.reshape(bsz, seq, PROJ_WIDTH)
    vec = pl.BlockSpec((1, DIFF_QK_DIM), lambda b, h, i: (0, 0))
    out = pl.pallas_call(
        functools.partial(_attn_kernel, tq, tk, lambda_init),
        grid=(bsz, nh, seq // tq),
        in_specs=[pl.BlockSpec((None, tq, HEAD_DIM), lambda b, h, i: (b, i, h)),
                  pl.BlockSpec((None, seq, HEAD_DIM), lambda b, h, i: (b, 0, nh + h)),
                  pl.BlockSpec((None, seq, HEAD_DIM), lambda b, h, i: (b, 0, SEG_DV * (SEG // HEAD_DIM) + h)),
                  vec, vec, vec, vec,
                  pl.BlockSpec((1, HEAD_DIM), lambda b, h, i: (0, 0))],
        out_specs=pl.BlockSpec((None, tq, HEAD_DIM), lambda b, h, i: (b, i, h)),
        out_shape=jax.ShapeDtypeStruct((bsz, seq, DIFF_WIDTH), BF16),
        compiler_params=_cparams("parallel", "parallel", "arbitrary"), name="diff_attn",
    )(qk3, qk3, proj3, *[p.reshape(1, DIFF_QK_DIM).astype(F32) for p in lam_params],
      subln_w.reshape(1, HEAD_DIM).astype(F32))
    return out.reshape(bsz * seq, DIFF_WIDTH)


HALO_ROWS = 16


def _conv_kernel(tiles_per_seq, x_ref, prev_ref, w_ref, o_ref):
    i = pl.program_id(0)
    j = pl.program_id(1)
    x = x_ref[...].astype(F32)
    prev = prev_ref[...].astype(F32)[HALO_ROWS - SUBLANES:]
    prev = jnp.where(i % tiles_per_seq == 0, jnp.zeros_like(prev), prev)
    w = w_ref[...]
    head_rows = lax.broadcasted_iota(jnp.int32, (SUBLANES, x.shape[1]), 0)
    y = x * w[CONV_WIDTH - 1:CONV_WIDTH, :]
    y_head = y[:SUBLANES]
    for back in range(1, CONV_WIDTH):
        wk = w[CONV_WIDTH - 1 - back:CONV_WIDTH - back, :]
        y = y + pltpu.roll(x, back, axis=0) * wk
        mixed = jnp.where(head_rows < back, pltpu.roll(prev, back, axis=0), pltpu.roll(x[:SUBLANES], back, axis=0))
        y_head = y_head + mixed * wk
    is_qk = j < 2 * GDN_WIDTH // x.shape[1]
    q_scale = jnp.where(j < GDN_WIDTH // x.shape[1], HEAD_DIM ** -0.5, 1.0)

    def finish(v):
        v = _silu(v)
        outs = []
        for h in range(v.shape[1] // HEAD_DIM):
            vh = v[:, h * HEAD_DIM:(h + 1) * HEAD_DIM]
            nrm = lax.rsqrt(jnp.sum(vh * vh, axis=-1, keepdims=True) + EPS) * q_scale
            outs.append(vh * jnp.where(is_qk, nrm, 1.0))
        return jnp.concatenate(outs, axis=1)

    o_ref[...] = finish(jnp.concatenate([y_head, y[SUBLANES:]], axis=0)).astype(o_ref.dtype)


def _gdn_conv(proj, conv_w, seq):
    t = proj.shape[0]
    tm = min(512, seq)
    cw = 2 * HEAD_DIM
    nj = 3 * GDN_WIDTH // cw
    halo = tm // HALO_ROWS
    return pl.pallas_call(
        functools.partial(_conv_kernel, seq // tm),
        grid=(t // tm, nj),
        in_specs=[pl.BlockSpec((tm, cw), lambda i, j: (i, j)),
                  pl.BlockSpec((HALO_ROWS, cw), lambda i, j: (jnp.maximum(i * halo - 1, 0), j)),
                  pl.BlockSpec((CONV_WIDTH, cw), lambda i, j: (0, j))],
        out_specs=pl.BlockSpec((tm, cw), lambda i, j: (i, j)),
        out_shape=jax.ShapeDtypeStruct((t, 3 * GDN_WIDTH), BF16),
        compiler_params=_cparams("parallel", "parallel"), name="gdn_conv",
    )(proj, proj, conv_w.astype(F32))


def _softplus(x):
    return jnp.maximum(x, 0.0) + jnp.log(1.0 + jnp.exp(-jnp.abs(x)))


def _gdn_kernel(n_chunks, q_ref, k_ref, v_ref, z_ref, ab_ref, abt_ref, alog_ref, dtb_ref, alog_t_ref, dtb_t_ref,
                gw_ref, o_ref, state_ref, gc_ref, gr_ref):
    C = GDN_CHUNK
    nh = GDN_HEADS
    rows = n_chunks * C

    @pl.when(pl.program_id(1) == 0)
    def _():
        state_ref[...] = jnp.zeros_like(state_ref)

    neg_a = -jnp.exp(alog_ref[...])
    g_col = neg_a * _softplus(ab_ref[:, :LANES].astype(F32) + dtb_ref[...])
    g_row = -jnp.exp(alog_t_ref[...]) * _softplus(abt_ref[...] + dtb_t_ref[...])
    r_i = lax.broadcasted_iota(jnp.int32, (rows, rows), 0)
    c_i = lax.broadcasted_iota(jnp.int32, (rows, rows), 1)
    same = (r_i // C) == (c_i // C)
    tri_l = jnp.where(same & (c_i <= r_i), 1.0, 0.0).astype(F32)
    tri_u = jnp.where(same & (r_i <= c_i), 1.0, 0.0).astype(F32)
    gc_ref[...] = jnp.dot(tri_l, g_col, precision=HIGHEST, preferred_element_type=F32)
    gr_full = jnp.dot(g_row, tri_u, precision=HIGHEST, preferred_element_type=F32)
    for c in range(n_chunks):
        gr_ref[c] = gr_full[:, c * C:(c + 1) * C]

    ci = lax.broadcasted_iota(jnp.int32, (C, C), 0)
    cj = lax.broadcasted_iota(jnp.int32, (C, C), 1)
    incl = ci >= cj
    strict = ci > cj
    eye = jnp.where(ci == cj, 1.0, 0.0).astype(F32)
    gw = gw_ref[...]

    def chunk(c, _):
        r0 = pl.multiple_of(c * C, C)
        gc_all = gc_ref[pl.ds(r0, C), :]
        gr_all = gr_ref[c]
        b_all = _sigmoid(ab_ref[pl.ds(r0, C), :LANES].astype(F32))
        for h in range(nh):
            sl = slice(h * HEAD_DIM, (h + 1) * HEAD_DIM)
            q = q_ref[pl.ds(r0, C), sl]
            k = k_ref[pl.ds(r0, C), sl]
            v = v_ref[pl.ds(r0, C), sl].astype(F32)
            z = z_ref[pl.ds(r0, C), sl].astype(F32)
            gcol = gc_all[:, h:h + 1]
            grow = gr_all[h:h + 1, :]
            beta = b_all[:, nh + h:nh + h + 1]
            decay = jnp.exp(jnp.where(incl, gcol - grow, -jnp.inf))
            kk = _dot_nt(k, k)
            neg_low = jnp.where(strict, -(beta * kk * decay), 0.0)
            t_mat = eye + neg_low
            pw = neg_low
            for _ in range(int(math.log2(C)) - 1):
                pw = jnp.dot(pw, pw, precision=HIGHEST, preferred_element_type=F32)
                t_mat = t_mat + jnp.dot(t_mat, pw, precision=HIGHEST, preferred_element_type=F32)
            kf = k.astype(F32)
            eg = jnp.exp(gcol)
            rhs = jnp.concatenate([v * beta, kf * (beta * eg)], axis=1).astype(BF16)
            uw = _dot(t_mat.astype(BF16), rhs)
            u_i, w_i = uw[:, :HEAD_DIM], uw[:, HEAD_DIM:]
            qk = jnp.where(incl, _dot_nt(q, k) * decay, 0.0)
            state = state_ref[h]
            sb = state.astype(BF16)
            v_new = u_i - _dot(w_i.astype(BF16), sb)
            vb = v_new.astype(BF16)
            o = _dot((q.astype(F32) * eg).astype(BF16), sb) + _dot(qk.astype(BF16), vb)
            g_last = gcol[C - 1:C, :]
            state_ref[h] = state * jnp.exp(g_last) + _dot_tn((kf * jnp.exp(g_last - gcol)).astype(BF16), vb)
            o = o * lax.rsqrt(jnp.mean(o * o, axis=-1, keepdims=True) + EPS) * gw * _silu(z)
            o_ref[pl.ds(r0, C), sl] = o.astype(o_ref.dtype)
        return 0

    lax.fori_loop(0, n_chunks, chunk, 0)


def _gdn(qkv, proj, bsz, seq, a_log, dt_bias, gnorm_w):
    t = bsz * seq
    n_chunks = min(4, seq // GDN_CHUNK)
    rows = n_chunks * GDN_CHUNK
    steps = seq // rows
    ab_off = SEG_U * SEG + AB_OFF
    abt = proj[:, ab_off:ab_off + 2 * SUBLANES].astype(F32).T

    def lane_vec(p):
        return jnp.zeros((1, LANES), F32).at[0, :GDN_HEADS].set(p.astype(F32))

    def sublane_vec(p):
        return jnp.zeros((2 * SUBLANES, 1), F32).at[:GDN_HEADS, 0].set(p.astype(F32))

    def rowblk(seg):
        return pl.BlockSpec((rows, SEG), lambda b, s: (b * steps + s, seg))

    return pl.pallas_call(
        functools.partial(_gdn_kernel, n_chunks),
        grid=(bsz, steps),
        in_specs=[rowblk(0), rowblk(1), rowblk(2),
                  pl.BlockSpec((rows, SEG), lambda b, s: (b * steps + s, SEG_GZ)),
                  pl.BlockSpec((rows, 2 * LANES), lambda b, s: (b * steps + s, ab_off // (2 * LANES))),
                  pl.BlockSpec((2 * SUBLANES, rows), lambda b, s: (0, b * steps + s)),
                  pl.BlockSpec((1, LANES), lambda b, s: (0, 0)),
                  pl.BlockSpec((1, LANES), lambda b, s: (0, 0)),
                  pl.BlockSpec((2 * SUBLANES, 1), lambda b, s: (0, 0)),
                  pl.BlockSpec((2 * SUBLANES, 1), lambda b, s: (0, 0)),
                  pl.BlockSpec((1, HEAD_DIM), lambda b, s: (0, 0))],
        out_specs=pl.BlockSpec((rows, GDN_WIDTH), lambda b, s: (b * steps + s, 0)),
        out_shape=jax.ShapeDtypeStruct((t, GDN_WIDTH), BF16),
        scratch_shapes=[pltpu.VMEM((GDN_HEADS, HEAD_DIM, HEAD_DIM), F32),
                        pltpu.VMEM((rows, LANES), F32),
                        pltpu.VMEM((n_chunks, 2 * SUBLANES, GDN_CHUNK), F32)],
        compiler_params=_cparams("parallel", "arbitrary"), name="gdn",
    )(qkv, qkv, qkv, proj, proj, abt, lane_vec(a_log), lane_vec(dt_bias),
      sublane_vec(a_log), sublane_vec(dt_bias), gnorm_w.reshape(1, HEAD_DIM).astype(F32))


def _outproj_kernel(with_router, a1_ref, a2_ref, a3_ref, w1_ref, w2_ref, w3_ref, h_ref, gate_ref,
                    nw_ref, scale_ref, shift_ref, *rest):
    if with_router:
        wr_ref, hres_ref, hn_ref, route_ref = rest
    else:
        hres_ref, hn_ref = rest
    mix = _dot(a1_ref[...], w1_ref[...]) + _dot(a2_ref[...], w2_ref[...]) + _dot(a3_ref[...], w3_ref[...])
    h = h_ref[...] + gate_ref[...] * mix
    hres_ref[...] = h
    hn = _rms_mod(h, nw_ref[...], scale_ref[...], shift_ref[...])
    hn_ref[...] = hn.astype(hn_ref.dtype)
    if with_router:
        logits = jnp.dot(hn, wr_ref[...], precision=HIGHEST, preferred_element_type=F32)
        lane = lax.broadcasted_iota(jnp.int32, logits.shape, 1)
        lg = jnp.where(lane < N_EXPERTS, logits, -jnp.inf)
        m1 = jnp.max(lg, axis=-1, keepdims=True)
        i1 = jnp.min(jnp.where(lg == m1, lane, LANES), axis=-1, keepdims=True)
        lg2 = jnp.where(lane == i1, -jnp.inf, lg)
        m2 = jnp.max(lg2, axis=-1, keepdims=True)
        i2 = jnp.min(jnp.where(lg2 == m2, lane, LANES), axis=-1, keepdims=True)
        e = jnp.exp(m2 - m1)
        g1 = 1.0 / (1.0 + e)
        g2 = e / (1.0 + e)
        route_ref[...] = jnp.where(lane == 0, i1.astype(F32),
                                   jnp.where(lane == 1, i2.astype(F32),
                                             jnp.where(lane == 2, g1, jnp.where(lane == 3, g2, 0.0))))


def _outproj(a1, a2, a3, w_out, h_res, seq, gate, norm_w, scale, shift, w_router):
    t, d = h_res.shape
    tm = min(256, t)
    with_router = w_router is not None
    k1, k2 = a1.shape[1], a2.shape[1]
    w1, w2, w3 = w_out[:k1], w_out[k1:k1 + k2], w_out[k1 + k2:]

    def rows(w):
        return pl.BlockSpec((tm, w), lambda i: (i, 0))

    def whole(a):
        return pl.BlockSpec(a.shape, lambda i: (0, 0))

    per_batch = pl.BlockSpec((None, 1, d), lambda i: ((i * tm) // seq, 0, 0))
    nw = norm_w.reshape(1, d).astype(F32)
    args = [a1, a2, a3, w1, w2, w3, h_res, gate, nw, scale, shift]
    specs = [rows(k1), rows(k2), rows(a3.shape[1]), whole(w1), whole(w2), whole(w3), rows(d), per_batch,
             whole(nw), per_batch, per_batch]
    out_shape = [jax.ShapeDtypeStruct((t, d), F32), jax.ShapeDtypeStruct((t, d), BF16)]
    out_specs = [rows(d), rows(d)]
    if with_router:
        wr = jnp.zeros((d, LANES), F32).at[:, :N_EXPERTS].set(w_router.astype(F32))
        args.append(wr)
        specs.append(whole(wr))
        out_shape.append(jax.ShapeDtypeStruct((t, LANES), F32))
        out_specs.append(rows(LANES))
    return pl.pallas_call(
        functools.partial(_outproj_kernel, with_router),
        grid=(t // tm,), in_specs=specs, out_specs=out_specs, out_shape=out_shape,
        compiler_params=_cparams("parallel"), name="out_proj",
    )(*args)


def _ffn_kernel(te_ref, nv_ref, x_ref, w1_ref, w3_ref, w2_ref, o_ref, acc_ref):
    i = pl.program_id(0)
    f = pl.program_id(1)

    @pl.when(f == 0)
    def _():
        acc_ref[...] = jnp.zeros_like(acc_ref)

    @pl.when(i < nv_ref[0])
    def _():
        x = x_ref[...]
        h1 = _dot(x, w1_ref[...])
        h3 = _dot(x, w3_ref[...])
        acc_ref[...] += _dot((_silu(h1) * h3).astype(BF16), w2_ref[...])

    @pl.when(f == pl.num_programs(1) - 1)
    def _():
        o_ref[...] = acc_ref[...].astype(o_ref.dtype)


def _ffn(x, w1, w3, w2, tile_expert, n_valid, tm, tf):
    r, d = x.shape
    ff = w1.shape[2]
    tm, tf = min(tm, r), min(tf, ff)
    grid_spec = pltpu.PrefetchScalarGridSpec(
        num_scalar_prefetch=2,
        grid=(r // tm, ff // tf),
        in_specs=[pl.BlockSpec((tm, d), lambda i, f, te, nv: (i, 0)),
                  pl.BlockSpec((None, d, tf), lambda i, f, te, nv: (te[i], 0, f)),
                  pl.BlockSpec((None, d, tf), lambda i, f, te, nv: (te[i], 0, f)),
                  pl.BlockSpec((None, tf, d), lambda i, f, te, nv: (te[i], f, 0))],
        out_specs=pl.BlockSpec((tm, d), lambda i, f, te, nv: (i, 0)),
        scratch_shapes=[pltpu.VMEM((tm, d), F32)])
    return pl.pallas_call(
        _ffn_kernel, grid_spec=grid_spec,
        out_shape=jax.ShapeDtypeStruct((r, d), BF16),
        compiler_params=_cparams("parallel", "arbitrary"), name="ffn",
    )(tile_expert, n_valid, x, w1, w3, w2)


GATHER_ROWS = 256


def _gather_kernel(idx_ref, src_ref, out_ref, sem):
    base = pl.program_id(0) * GATHER_ROWS

    def row_copy(r):
        return pltpu.make_async_copy(src_ref.at[pl.ds(idx_ref[base + r], 1)],
                                     out_ref.at[pl.ds(base + r, 1)], sem)

    def issue(r, _):
        row_copy(r).start()
        return 0

    def drain(r, _):
        row_copy(r).wait()
        return 0

    lax.fori_loop(0, GATHER_ROWS, issue, 0)
    lax.fori_loop(0, GATHER_ROWS, drain, 0)


def _gather_rows(src, idx):
    r = idx.shape[0]
    n, d = src.shape
    tile = (2 * SUBLANES, LANES)
    assert d == tile[0] * tile[1] and src.dtype == BF16
    grid_spec = pltpu.PrefetchScalarGridSpec(
        num_scalar_prefetch=1, grid=(r // GATHER_ROWS,),
        in_specs=[pl.BlockSpec(memory_space=pl.ANY)],
        out_specs=pl.BlockSpec(memory_space=pl.ANY),
        scratch_shapes=[pltpu.SemaphoreType.DMA(())])
    out = pl.pallas_call(
        _gather_kernel, grid_spec=grid_spec,
        out_shape=jax.ShapeDtypeStruct((r,) + tile, src.dtype),
        compiler_params=_cparams("arbitrary"), name="gather_rows",
    )(idx, src.reshape((n,) + tile))
    return out.reshape(r, d)


MOE_TILE = 512


def _moe_plan(route, tm):
    t = route.shape[0]
    e_flat = jnp.concatenate([route[:, 0], route[:, 1]]).astype(jnp.int32)
    onehot = (e_flat[:, None] == jnp.arange(N_EXPERTS, dtype=jnp.int32)[None, :]).astype(jnp.int32)
    counts = jnp.sum(onehot, axis=0)
    rank = jnp.sum((jnp.cumsum(onehot, axis=0) - onehot) * onehot, axis=1)
    padded = ((counts + tm - 1) // tm) * tm
    ends = jnp.cumsum(padded)
    pos = (ends - padded)[e_flat] + rank
    n_rows = 2 * t + N_EXPERTS * tm
    tok = jnp.tile(jnp.arange(t, dtype=jnp.int32), 2)
    src_tok = jnp.zeros((n_rows,), jnp.int32).at[pos].set(tok)
    n_tiles = n_rows // tm
    n_valid = (ends[-1] // tm).astype(jnp.int32)
    tile_start = jnp.arange(n_tiles, dtype=jnp.int32) * tm
    tile_expert = jnp.sum((tile_start[:, None] >= ends[None, :]).astype(jnp.int32), axis=1)
    last_expert = jnp.sum((tile_start[n_valid - 1] >= ends).astype(jnp.int32))
    tile_expert = jnp.where(tile_start < ends[-1], tile_expert, last_expert).astype(jnp.int32)
    return src_tok, pos[:t], pos[t:], tile_expert, n_valid.reshape(1)


def _in_proj_weight(w_in_l):
    o = 0
    u = w_in_l[:, o:o + SSM_WIDTH]; o += SSM_WIDTH
    dq = w_in_l[:, o:o + DIFF_WIDTH]; o += DIFF_WIDTH
    dk = w_in_l[:, o:o + DIFF_WIDTH]; o += DIFF_WIDTH
    dv = w_in_l[:, o:o + DIFF_WIDTH]; o += DIFF_WIDTH
    gqkv = w_in_l[:, o:o + 3 * GDN_WIDTH]; o += 3 * GDN_WIDTH
    gz = w_in_l[:, o:o + GDN_WIDTH]; o += GDN_WIDTH
    gab = w_in_l[:, o:]
    pad = jnp.zeros((w_in_l.shape[0], SEG - SSM_WIDTH - gab.shape[1]), w_in_l.dtype)
    return jnp.concatenate([gqkv, gz, dq, dk, dv, u, gab, pad], axis=1).astype(BF16)


def kernel(x, c, positions, w_ada, b_ada, norm_mix, norm_ffn, norm_final, w_in, w_out, ssm_a_re, ssm_a_im, ssm_log_dt, ssm_b_re, ssm_b_im, ssm_c_re, ssm_c_im, ssm_d, ssm_w_glu, diff_lam_q1, diff_lam_k1, diff_lam_q2, diff_lam_k2, diff_subln, gdn_conv, gdn_a_log, gdn_dt_bias, gdn_norm, ffn_w1, ffn_w3, ffn_w2, moe_router, moe_w1, moe_w3, moe_w2):
    bsz, seq, d = x.shape
    t = bsz * seq
    depth = w_in.shape[0]
    h_res = x.astype(F32).reshape(t, d)

    c_pad = jnp.zeros((SUBLANES, d), F32).at[:bsz].set(c.astype(F32))
    mod = _ada(c_pad, w_ada, b_ada)[:, :bsz]
    mods = [[m.reshape(bsz, 1, d) for m in jnp.split(mod[l], 6, axis=-1)] for l in range(depth)]
    rope_tables = _rope_tables(positions)

    hn = _resnorm(h_res, seq, [], None, norm_mix[0].astype(F32), mods[0][1], mods[0][0], False, BF16)
    out = None
    for l in range(depth):
        shift1, scale1, gate1, shift2, scale2, gate2 = mods[l]
        is_moe = l % 2 == 1
        proj = _matmul(hn, _in_proj_weight(w_in[l]), BF16, 512, PROJ_WIDTH // 2)

        tables = _s5_tables(ssm_a_re[l], ssm_a_im[l], ssm_log_dt[l], ssm_b_re[l], ssm_b_im[l],
                            ssm_c_re[l], ssm_c_im[l])
        y_ssm = _s5_mixer(proj, bsz, seq, tables, ssm_d[l], ssm_w_glu[l].astype(BF16))

        lambda_init = 0.8 - 0.6 * math.exp(-0.3 * l)
        qk = _rope(proj, rope_tables)
        y_diff = _diff_attention(qk, proj, bsz, seq,
                                 (diff_lam_q1[l], diff_lam_k1[l], diff_lam_q2[l], diff_lam_k2[l]),
                                 diff_subln[l], lambda_init)

        qkv = _gdn_conv(proj, gdn_conv[l], seq)
        y_gdn = _gdn(qkv, proj, bsz, seq, gdn_a_log[l], gdn_dt_bias[l], gdn_norm[l])

        res = _outproj(y_ssm, y_diff, y_gdn, w_out[l].astype(BF16), h_res, seq, gate1,
                       norm_ffn[l], scale2, shift2, moe_router[l // 2] if is_moe else None)
        h_res, hn2 = res[0], res[1]

        if is_moe:
            src_tok, pos1, pos2, tile_expert, n_valid = _moe_plan(res[2], MOE_TILE)
            xs = _gather_rows(hn2, src_tok)
            ys = _ffn(xs, moe_w1[l // 2].astype(BF16), moe_w3[l // 2].astype(BF16),
                      moe_w2[l // 2].astype(BF16), tile_expert, n_valid, MOE_TILE, 512)
            deltas = [(_gather_rows(ys, pos1), res[2][:, 2:3]), (_gather_rows(ys, pos2), res[2][:, 3:4])]
        else:
            n_tiles = t // min(512, t)
            ys = _ffn(hn2, ffn_w1[l // 2:l // 2 + 1].astype(BF16), ffn_w3[l // 2:l // 2 + 1].astype(BF16),
                      ffn_w2[l // 2:l // 2 + 1].astype(BF16), jnp.zeros((n_tiles,), jnp.int32),
                      jnp.full((1,), n_tiles, jnp.int32), 512, 512)
            deltas = [(ys, None)]

        if l + 1 < depth:
            h_res, hn = _resnorm(h_res, seq, deltas, gate2, norm_mix[l + 1].astype(F32),
                                 mods[l + 1][1], mods[l + 1][0], True, BF16)
        else:
            out = _resnorm(h_res, seq, deltas, gate2, norm_final.astype(F32), None, None, False, x.dtype)
    return out.reshape(bsz, seq, d)
```

```python
import functools
import math

import numpy as np
import jax
import jax.numpy as jnp
from jax import lax
from jax.experimental import pallas as pl
from jax.experimental.pallas import tpu as pltpu

F32 = jnp.float32
BF16 = jnp.bfloat16
HIGHEST = lax.Precision.HIGHEST

D_MODEL = 2048
SSM_WIDTH = 512
SSM_CH = 16
SSM_GROUPS = SSM_WIDTH // SSM_CH
SSM_STATE = 64
SSM_CHUNK = 64
DIFF_WIDTH = 768
HEAD_DIM = 128
DIFF_HEADS = DIFF_WIDTH // HEAD_DIM
DIFF_QK_DIM = HEAD_DIM // 2
GDN_WIDTH = 768
GDN_HEADS = GDN_WIDTH // HEAD_DIM
CONV_WIDTH = 4
GDN_CHUNK = 64
ROPE_THETA = 500000.0
ROPE_DIM = DIFF_QK_DIM // 4
ROPE_HALF = ROPE_DIM // 2
N_EXPERTS = 8
EPS = 1e-6
LANES = 128
SUBLANES = 8

SEG = 768
SEG_GQ, SEG_GK, SEG_GV, SEG_GZ, SEG_DQ, SEG_DK, SEG_DV, SEG_U = range(8)
PROJ_WIDTH = 8 * SEG
AB_OFF = SSM_WIDTH

VMEM_LIMIT = 56 * 1024 * 1024


def _cparams(*sem):
    return pltpu.CompilerParams(dimension_semantics=sem, vmem_limit_bytes=VMEM_LIMIT)


def _dot(a, b):
    return jnp.dot(a, b, preferred_element_type=F32)


def _dot_nt(a, b):
    return lax.dot_general(a, b, (((1,), (1,)), ((), ())), preferred_element_type=F32)


def _dot_tn(a, b):
    return lax.dot_general(a, b, (((0,), (0,)), ((), ())), preferred_element_type=F32)


def _sigmoid(x):
    return 1.0 / (1.0 + jnp.exp(-x))


def _silu(x):
    return x * _sigmoid(x)


def _ada_kernel(c_ref, w_ref, b_ref, o_ref):
    cond = _silu(c_ref[...])
    o_ref[...] = _dot(cond.astype(BF16), w_ref[...].astype(BF16)) + b_ref[...]


def _ada(c_pad, w_ada, b_ada):
    depth, d, n = w_ada.shape
    tn = 1024
    return pl.pallas_call(
        _ada_kernel,
        grid=(depth, n // tn),
        in_specs=[pl.BlockSpec((SUBLANES, d), lambda l, j: (0, 0)),
                  pl.BlockSpec((None, d, tn), lambda l, j: (l, 0, j)),
                  pl.BlockSpec((None, 1, tn), lambda l, j: (l, 0, j))],
        out_specs=pl.BlockSpec((None, SUBLANES, tn), lambda l, j: (l, 0, j)),
        out_shape=jax.ShapeDtypeStruct((depth, SUBLANES, n), F32),
        compiler_params=_cparams("arbitrary", "arbitrary"),
        name="ada",
    )(c_pad, w_ada, b_ada.reshape(depth, 1, n))


def _rms_mod(h, w, scale, shift):
    y = h * lax.rsqrt(jnp.mean(h * h, axis=-1, keepdims=True) + EPS) * w
    if scale is not None:
        y = y * (1.0 + scale) + shift
    return y


def _resnorm_kernel(n_delta, row_gated, has_mod, emit_res, *refs):
    refs = list(refs)
    h = refs.pop(0)[...]
    if n_delta:
        delta = None
        for _ in range(n_delta):
            y = refs.pop(0)[...].astype(F32)
            if row_gated:
                y = y * refs.pop(0)[...]
            delta = y if delta is None else delta + y
        h = h + refs.pop(0)[...] * delta
    w = refs.pop(0)[...]
    scale = shift = None
    if has_mod:
        scale = refs.pop(0)[...]
        shift = refs.pop(0)[...]
    if emit_res:
        refs.pop(0)[...] = h
    o_ref = refs.pop(0)
    o_ref[...] = _rms_mod(h, w, scale, shift).astype(o_ref.dtype)


def _resnorm(h_res, seq, deltas, gate, norm_w, scale, shift, emit_res, out_dtype):
    t, d = h_res.shape
    tm = min(256, t)
    row = pl.BlockSpec((tm, d), lambda i: (i, 0))
    per_batch = pl.BlockSpec((None, 1, d), lambda i: ((i * tm) // seq, 0, 0))
    row_gated = bool(deltas) and deltas[0][1] is not None
    args, specs = [h_res], [row]
    for y, rg in deltas:
        args.append(y)
        specs.append(row)
        if row_gated:
            args.append(rg)
            specs.append(pl.BlockSpec((tm, 1), lambda i: (i, 0)))
    if deltas:
        args.append(gate)
        specs.append(per_batch)
    args.append(norm_w.reshape(1, d))
    specs.append(pl.BlockSpec((1, d), lambda i: (0, 0)))
    if scale is not None:
        args += [scale, shift]
        specs += [per_batch, per_batch]
    out_shape, out_specs = [], []
    if emit_res:
        out_shape.append(jax.ShapeDtypeStruct((t, d), F32))
        out_specs.append(row)
    out_shape.append(jax.ShapeDtypeStruct((t, d), out_dtype))
    out_specs.append(row)
    res = pl.pallas_call(
        functools.partial(_resnorm_kernel, len(deltas), row_gated, scale is not None, emit_res),
        grid=(t // tm,), in_specs=specs, out_specs=out_specs, out_shape=out_shape,
        compiler_params=_cparams("parallel"), name="resnorm",
    )(*args)
    return res if emit_res else res[0]


def _matmul_kernel(x_ref, w_ref, o_ref):
    o_ref[...] = _dot(x_ref[...], w_ref[...]).astype(o_ref.dtype)


def _matmul(x, w, out_dtype, tm, tn):
    m, k = x.shape
    n = w.shape[1]
    tm, tn = min(tm, m), min(tn, n)
    return pl.pallas_call(
        _matmul_kernel,
        grid=(n // tn, m // tm),
        in_specs=[pl.BlockSpec((tm, k), lambda j, i: (i, 0)),
                  pl.BlockSpec((k, tn), lambda j, i: (0, j))],
        out_specs=pl.BlockSpec((tm, tn), lambda j, i: (i, j)),
        out_shape=jax.ShapeDtypeStruct((m, n), out_dtype),
        compiler_params=_cparams("parallel", "parallel"), name="in_proj",
    )(x, w)


def _s5_tables(a_re, a_im, log_dt, b_re, b_im, c_re, c_im):
    L, G, P, H = SSM_CHUNK, SSM_GROUPS, SSM_STATE, SSM_CH
    lam = lax.complex(a_re.astype(F32), a_im.astype(F32))
    dt = jnp.exp(log_dt.astype(F32))[:, None]
    lam_bar = jnp.exp(lam * dt)
    b_bar = ((lam_bar - 1.0) / lam)[:, :, None] * lax.complex(b_re.astype(F32), b_im.astype(F32))
    c_mat = lax.complex(c_re.astype(F32), c_im.astype(F32))
    steps = jnp.arange(L + 1, dtype=F32)
    pw = jnp.exp((lam * dt)[:, None, :] * steps[None, :, None])
    kern = jnp.real(jnp.einsum('ghp,gjp,gpi->gjih', c_mat, pw[:, :L], b_bar))
    s_idx = jnp.arange(L)[:, None]
    t_idx = jnp.arange(L)[None, :]
    lag = t_idx - s_idx
    tm = jnp.where((lag >= 0)[None, :, :, None, None], kern[:, jnp.clip(lag, 0, L - 1)], 0.0)
    tm = tm.transpose(0, 1, 3, 2, 4).reshape(G, L * H, L * H)
    zc = pw[:, L - 1 - jnp.arange(L)][:, :, :, None] * b_bar[:, None]
    zc = zc.transpose(0, 1, 3, 2).reshape(G, L * H, P)
    cl = c_mat[:, None] * pw[:, 1:L + 1][:, :, None, :]
    cl = cl.transpose(0, 3, 1, 2).reshape(G, P, L * H)
    a_l = pw[:, L]
    return (tm.astype(BF16), jnp.real(zc).astype(BF16), jnp.imag(zc).astype(BF16),
            jnp.real(cl).astype(BF16), (-jnp.imag(cl)).astype(BF16),
            jnp.real(a_l)[:, None, :], jnp.imag(a_l)[:, None, :])


def _s5_core_kernel(nb, u_ref, tm_ref, zre_ref, zim_ref, yre_ref, yim_ref, are_ref, aim_ref,
                    o_ref, xre_ref, xim_ref):
    u = u_ref[...]
    z_re = _dot(u, zre_ref[...])
    z_im = _dot(u, zim_ref[...])
    a_re = are_ref[...]
    a_im = aim_ref[...]
    n_chunks = u.shape[0] // nb
    s_re = jnp.zeros((nb, SSM_STATE), F32)
    s_im = jnp.zeros((nb, SSM_STATE), F32)
    for c in range(n_chunks):
        xre_ref[c * nb:(c + 1) * nb, :] = s_re
        xim_ref[c * nb:(c + 1) * nb, :] = s_im
        zr = z_re[c * nb:(c + 1) * nb, :]
        zi = z_im[c * nb:(c + 1) * nb, :]
        s_re, s_im = a_re * s_re - a_im * s_im + zr, a_re * s_im + a_im * s_re + zi
    y = _dot(u, tm_ref[...])
    y = y + _dot(xre_ref[...].astype(BF16), yre_ref[...])
    y = y + _dot(xim_ref[...].astype(BF16), yim_ref[...])
    o_ref[...] = y.astype(o_ref.dtype)


def _s5_core(ug, tables, nb):
    g, r, w = ug.shape
    tm, zre, zim, yre, yim, are, aim = tables
    p = SSM_STATE

    def grp(*shape):
        return pl.BlockSpec((None,) + shape, lambda i: (i,) + (0,) * len(shape))

    return pl.pallas_call(
        functools.partial(_s5_core_kernel, nb),
        grid=(g,),
        in_specs=[grp(r, w), grp(w, w), grp(w, p), grp(w, p), grp(p, w), grp(p, w), grp(1, p), grp(1, p)],
        out_specs=grp(r, w),
        out_shape=jax.ShapeDtypeStruct((g, r, w), BF16),
        scratch_shapes=[pltpu.VMEM((r, p), F32), pltpu.VMEM((r, p), F32)],
        compiler_params=_cparams("parallel"), name="s5_core",
    )(ug, tm, zre, zim, yre, yim, are, aim)


def _gelu_tanh(x):
    return 0.5 * x * (1.0 + jnp.tanh(math.sqrt(2.0 / math.pi) * (x + 0.044715 * (x * x * x))))


def _s5_post_kernel(y_ref, u_ref, d_ref, w_ref, o_ref):
    u = u_ref[:, :SSM_WIDTH].astype(F32)
    y = _gelu_tanh(y_ref[...].astype(F32) + d_ref[...] * u)
    o_ref[...] = (y * _sigmoid(_dot(y.astype(BF16), w_ref[...]))).astype(o_ref.dtype)


def _s5_post(y_core, proj, d_skip, w_glu):
    t = y_core.shape[0]
    tm = min(512, t)
    return pl.pallas_call(
        _s5_post_kernel,
        grid=(t // tm,),
        in_specs=[pl.BlockSpec((tm, SSM_WIDTH), lambda i: (i, 0)),
                  pl.BlockSpec((tm, SEG), lambda i: (i, SEG_U)),
                  pl.BlockSpec((1, SSM_WIDTH), lambda i: (0, 0)),
                  pl.BlockSpec((SSM_WIDTH, SSM_WIDTH), lambda i: (0, 0))],
        out_specs=pl.BlockSpec((tm, SSM_WIDTH), lambda i: (i, 0)),
        out_shape=jax.ShapeDtypeStruct((t, SSM_WIDTH), BF16),
        compiler_params=_cparams("parallel"), name="s5_post",
    )(y_core, proj, d_skip.reshape(1, SSM_WIDTH).astype(F32), w_glu)


def _s5_mixer(proj, bsz, seq, tables, d_skip, w_glu):
    L, G, H = SSM_CHUNK, SSM_GROUPS, SSM_CH
    t = bsz * seq
    nc = seq // L
    u = proj[:, SEG_U * SEG:SEG_U * SEG + SSM_WIDTH]
    ug = u.reshape(bsz, nc, L, G, H).transpose(3, 1, 0, 2, 4).reshape(G, nc * bsz, L * H)
    yg = _s5_core(ug, tables, bsz)
    y_core = yg.reshape(G, nc, bsz, L, H).transpose(2, 1, 3, 0, 4).reshape(t, SSM_WIDTH)
    return _s5_post(y_core, proj, d_skip, w_glu)


def _rope_tables(positions):
    inv_freq = ROPE_THETA ** (-jnp.arange(0, ROPE_DIM, 2, dtype=F32) / ROPE_DIM)
    ang = positions.astype(F32).reshape(-1)[:, None] * inv_freq
    cos, sin = jnp.cos(ang), jnp.sin(ang)
    r = np.arange(LANES) % DIFF_QK_DIM
    first = jnp.asarray(r < ROPE_HALF)[None, :]
    second = jnp.asarray((r >= ROPE_HALF) & (r < ROPE_DIM))[None, :]
    idx = jnp.asarray(r % ROPE_HALF)
    cos_l, sin_l = cos[:, idx], sin[:, idx]
    cosf = jnp.where(first | second, cos_l, 1.0)
    sin_a = jnp.where(first, -sin_l, 0.0)
    sin_b = jnp.where(second, sin_l, 0.0)
    return cosf, sin_a, sin_b


def _rope_kernel(x_ref, c_ref, sa_ref, sb_ref, o_ref):
    cosf, sin_a, sin_b = c_ref[...], sa_ref[...], sb_ref[...]
    n_slabs = x_ref.shape[1] // LANES
    for s in range(n_slabs):
        x = x_ref[:, s * LANES:(s + 1) * LANES].astype(F32)
        y = (x * cosf + pltpu.roll(x, LANES - ROPE_HALF, axis=1) * sin_a
             + pltpu.roll(x, ROPE_HALF, axis=1) * sin_b)
        if s < n_slabs // 2:
            y = y * (DIFF_QK_DIM ** -0.5)
        o_ref[:, s * LANES:(s + 1) * LANES] = y.astype(o_ref.dtype)


def _rope(proj, tables):
    t = proj.shape[0]
    tm = min(512, t)
    w = 2 * DIFF_WIDTH
    tab = pl.BlockSpec((tm, LANES), lambda i: (i, 0))
    return pl.pallas_call(
        _rope_kernel,
        grid=(t // tm,),
        in_specs=[pl.BlockSpec((tm, w), lambda i: (i, SEG_DQ // 2)), tab, tab, tab],
        out_specs=pl.BlockSpec((tm, w), lambda i: (i, 0)),
        out_shape=jax.ShapeDtypeStruct((t, w), BF16),
        compiler_params=_cparams("parallel"), name="rope",
    )(proj, *tables)


def _attn_kernel(tq, tk, lambda_init, q_ref, k_ref, v_ref, lq1_ref, lk1_ref, lq2_ref, lk2_ref, sw_ref, o_ref):
    qi = pl.program_id(2)
    q = q_ref[...]
    lane = lax.broadcasted_iota(jnp.int32, q.shape, 1)
    zero = jnp.zeros_like(q)
    qm = (jnp.where(lane < DIFF_QK_DIM, q, zero), jnp.where(lane >= DIFF_QK_DIM, q, zero))

    def block(kv, masked, carry):
        start = pl.multiple_of(kv * tk, tk)
        k = k_ref[pl.ds(start, tk), :]
        v = v_ref[pl.ds(start, tk), :]
        new = []
        for m in range(2):
            m_old, l_old, acc = carry[m]
            s = _dot_nt(qm[m], k)
            if masked:
                row = lax.broadcasted_iota(jnp.int32, s.shape, 0)
                col = lax.broadcasted_iota(jnp.int32, s.shape, 1)
                s = jnp.where(col <= row, s, -jnp.inf)
            m_new = jnp.maximum(m_old, jnp.max(s, axis=-1, keepdims=True))
            alpha = jnp.exp(m_old - m_new)
            p = jnp.exp(s - m_new)
            l_new = alpha * l_old + jnp.sum(p, axis=-1, keepdims=True)
            acc = alpha * acc + _dot(p.astype(BF16), v)
            new.append((m_new, l_new, acc))
        return tuple(new)

    init = tuple((jnp.full((tq, 1), -jnp.inf, F32), jnp.zeros((tq, 1), F32), jnp.zeros((tq, HEAD_DIM), F32))
                 for _ in range(2))
    n_full = qi * (tq // tk)
    carry = lax.fori_loop(0, n_full, lambda kv, c: block(kv, False, c), init)
    for d in range(tq // tk):
        carry = block(n_full + d, True, carry)
    (_, l1, acc1), (_, l2, acc2) = carry
    lam = (jnp.exp(jnp.sum(lq1_ref[...] * lk1_ref[...], axis=-1, keepdims=True))
           - jnp.exp(jnp.sum(lq2_ref[...] * lk2_ref[...], axis=-1, keepdims=True)) + lambda_init)
    o = acc1 / l1 - lam * (acc2 / l2)
    o = o * lax.rsqrt(jnp.mean(o * o, axis=-1, keepdims=True) + EPS) * sw_ref[...] * (1.0 - lambda_init)
    o_ref[...] = o.astype(o_ref.dtype)


def _diff_attention(qk, proj, bsz, seq, lam_params, subln_w, lambda_init):
    tq = tk = min(512, seq)
    nh = DIFF_HEADS
    qk3 = qk.reshape(bsz, seq, 2 * DIFF_WIDTH)
    proj3 = proj.reshape(bsz, seq, PROJ_WIDTH)
    vec = pl.BlockSpec((1, DIFF_QK_DIM), lambda b, h, i: (0, 0))
    out = pl.pallas_call(
        functools.partial(_attn_kernel, tq, tk, lambda_init),
        grid=(bsz, nh, seq // tq),
        in_specs=[pl.BlockSpec((None, tq, HEAD_DIM), lambda b, h, i: (b, i, h)),
                  pl.BlockSpec((None, seq, HEAD_DIM), lambda b, h, i: (b, 0, nh + h)),
                  pl.BlockSpec((None, seq, HEAD_DIM), lambda b, h, i: (b, 0, SEG_DV * (SEG // HEAD_DIM) + h)),
                  vec, vec, vec, vec,
                  pl.BlockSpec((1, HEAD_DIM), lambda b, h, i: (0, 0))],
        out_specs=pl.BlockSpec((None, tq, HEAD_DIM), lambda b, h, i: (b, i, h)),
        out_shape=jax.ShapeDtypeStruct((bsz, seq, DIFF_WIDTH), BF16),
        compiler_params=_cparams("parallel", "parallel", "arbitrary"), name="diff_attn",
    )(qk3, qk3, proj3, *[p.reshape(1, DIFF_QK_DIM).astype(F32) for p in lam_params],
      subln_w.reshape(1, HEAD_DIM).astype(F32))
    return out.reshape(bsz * seq, DIFF_WIDTH)


HALO_ROWS = 16


def _conv_kernel(tiles_per_seq, x_ref, prev_ref, w_ref, o_ref):
    i = pl.program_id(0)
    j = pl.program_id(1)
    x = x_ref[...].astype(F32)
    prev = prev_ref[...].astype(F32)[HALO_ROWS - SUBLANES:]
    prev = jnp.where(i % tiles_per_seq == 0, jnp.zeros_like(prev), prev)
    w = w_ref[...]
    head_rows = lax.broadcasted_iota(jnp.int32, (SUBLANES, x.shape[1]), 0)
    y = x * w[CONV_WIDTH - 1:CONV_WIDTH, :]
    y_head = y[:SUBLANES]
    for back in range(1, CONV_WIDTH):
        wk = w[CONV_WIDTH - 1 - back:CONV_WIDTH - back, :]
        y = y + pltpu.roll(x, back, axis=0) * wk
        mixed = jnp.where(head_rows < back, pltpu.roll(prev, back, axis=0), pltpu.roll(x[:SUBLANES], back, axis=0))
        y_head = y_head + mixed * wk
    is_qk = j < 2 * GDN_WIDTH // x.shape[1]
    q_scale = jnp.where(j < GDN_WIDTH // x.shape[1], HEAD_DIM ** -0.5, 1.0)

    def finish(v):
        v = _silu(v)
        outs = []
        for h in range(v.shape[1] // HEAD_DIM):
            vh = v[:, h * HEAD_DIM:(h + 1) * HEAD_DIM]
            nrm = lax.rsqrt(jnp.sum(vh * vh, axis=-1, keepdims=True) + EPS) * q_scale
            outs.append(vh * jnp.where(is_qk, nrm, 1.0))
        return jnp.concatenate(outs, axis=1)

    o_ref[...] = finish(jnp.concatenate([y_head, y[SUBLANES:]], axis=0)).astype(o_ref.dtype)


def _gdn_conv(proj, conv_w, seq):
    t = proj.shape[0]
    tm = min(512, seq)
    cw = 2 * HEAD_DIM
    nj = 3 * GDN_WIDTH // cw
    halo = tm // HALO_ROWS
    return pl.pallas_call(
        functools.partial(_conv_kernel, seq // tm),
        grid=(t // tm, nj),
        in_specs=[pl.BlockSpec((tm, cw), lambda i, j: (i, j)),
                  pl.BlockSpec((HALO_ROWS, cw), lambda i, j: (jnp.maximum(i * halo - 1, 0), j)),
                  pl.BlockSpec((CONV_WIDTH, cw), lambda i, j: (0, j))],
        out_specs=pl.BlockSpec((tm, cw), lambda i, j: (i, j)),
        out_shape=jax.ShapeDtypeStruct((t, 3 * GDN_WIDTH), BF16),
        compiler_params=_cparams("parallel", "parallel"), name="gdn_conv",
    )(proj, proj, conv_w.astype(F32))


def _softplus(x):
    return jnp.maximum(x, 0.0) + jnp.log(1.0 + jnp.exp(-jnp.abs(x)))


def _gdn_kernel(n_chunks, q_ref, k_ref, v_ref, z_ref, ab_ref, abt_ref, alog_ref, dtb_ref, alog_t_ref, dtb_t_ref,
                gw_ref, o_ref, state_ref):
    C = GDN_CHUNK
    nh = GDN_HEADS
    rows = n_chunks * C

    @pl.when(pl.program_id(1) == 0)
    def _():
        state_ref[...] = jnp.zeros_like(state_ref)

    neg_a = -jnp.exp(alog_ref[...])
    g_col = neg_a * _softplus(ab_ref[:, :LANES].astype(F32) + dtb_ref[...])
    g_row = -jnp.exp(alog_t_ref[...]) * _softplus(abt_ref[...] + dtb_t_ref[...])
    r_i = lax.broadcasted_iota(jnp.int32, (rows, rows), 0)
    c_i = lax.broadcasted_iota(jnp.int32, (rows, rows), 1)
    same = (r_i // C) == (c_i // C)
    tri_l = jnp.where(same & (c_i <= r_i), 1.0, 0.0).astype(F32)
    tri_u = jnp.where(same & (r_i <= c_i), 1.0, 0.0).astype(F32)
    gc_all = jnp.dot(tri_l, g_col, precision=HIGHEST, preferred_element_type=F32)
    gr_all = jnp.dot(g_row, tri_u, precision=HIGHEST, preferred_element_type=F32)
    b_all = _sigmoid(ab_ref[:, :LANES].astype(F32))

    incl = same & (r_i >= c_i)
    strict = same & (r_i > c_i)
    eye = jnp.where(r_i == c_i, 1.0, 0.0).astype(F32)
    gw = gw_ref[...]

    heads = range(nh)
    sls = [slice(h * HEAD_DIM, (h + 1) * HEAD_DIM) for h in heads]
    q = [q_ref[:, sl] for sl in sls]
    k = [k_ref[:, sl] for sl in sls]
    kf = [x.astype(F32) for x in k]
    gcol = [gc_all[:, h:h + 1] for h in heads]
    beta = [b_all[:, nh + h:nh + h + 1] for h in heads]
    decay = [jnp.exp(jnp.where(incl, gcol[h] - gr_all[h:h + 1, :], -jnp.inf)) for h in heads]
    pw = [jnp.where(strict, -(beta[h] * _dot_nt(k[h], k[h]) * decay[h]), 0.0) for h in heads]
    t_mat = [eye + p for p in pw]
    pw = [p.astype(BF16) for p in pw]
    for _ in range(int(math.log2(C)) - 1):
        pw = [_dot(p, p).astype(BF16) for p in pw]
        t_mat = [t + _dot(t.astype(BF16), p) for t, p in zip(t_mat, pw)]
    eg = [jnp.exp(g) for g in gcol]
    uw = [_dot(t_mat[h].astype(BF16),
               jnp.concatenate([v_ref[:, sls[h]].astype(F32) * beta[h], kf[h] * (beta[h] * eg[h])],
                               axis=1).astype(BF16)) for h in heads]
    u_all = [x[:, :HEAD_DIM] for x in uw]
    w_all = [x[:, HEAD_DIM:].astype(BF16) for x in uw]
    qk_all = [jnp.where(incl, _dot_nt(q[h], k[h]) * decay[h], 0.0).astype(BF16) for h in heads]
    qe_all = [(q[h].astype(F32) * eg[h]).astype(BF16) for h in heads]
    g_last = [[g[(c + 1) * C - 1:(c + 1) * C, :] for c in range(n_chunks)] for g in gcol]
    k_dec = [(kf[h] * jnp.exp(jnp.concatenate([jnp.broadcast_to(g, (C, 1)) for g in g_last[h]], axis=0)
                              - gcol[h])).astype(BF16) for h in heads]

    state = [state_ref[h] for h in heads]
    for c in range(n_chunks):
        rs = slice(c * C, (c + 1) * C)
        sb = [s.astype(BF16) for s in state]
        v_new = [(u_all[h][rs] - _dot(w_all[h][rs], sb[h])).astype(BF16) for h in heads]
        o = [_dot(qe_all[h][rs], sb[h]) + _dot(qk_all[h][rs, rs], v_new[h]) for h in heads]
        state = [state[h] * jnp.exp(g_last[h][c]) + _dot_tn(k_dec[h][rs], v_new[h]) for h in heads]
        for h in heads:
            gated = gw * _silu(z_ref[rs, sls[h]].astype(F32))
            o_h = o[h] * lax.rsqrt(jnp.mean(o[h] * o[h], axis=-1, keepdims=True) + EPS) * gated
            o_ref[rs, sls[h]] = o_h.astype(o_ref.dtype)
    for h in heads:
        state_ref[h] = state[h]


def _gdn(qkv, proj, bsz, seq, a_log, dt_bias, gnorm_w):
    t = bsz * seq
    n_chunks = min(4, seq // GDN_CHUNK)
    rows = n_chunks * GDN_CHUNK
    steps = seq // rows
    ab_off = SEG_U * SEG + AB_OFF
    abt = proj[:, ab_off:ab_off + 2 * SUBLANES].astype(F32).T

    def lane_vec(p):
        return jnp.zeros((1, LANES), F32).at[0, :GDN_HEADS].set(p.astype(F32))

    def sublane_vec(p):
        return jnp.zeros((2 * SUBLANES, 1), F32).at[:GDN_HEADS, 0].set(p.astype(F32))

    def rowblk(seg):
        return pl.BlockSpec((rows, SEG), lambda b, s: (b * steps + s, seg))

    return pl.pallas_call(
        functools.partial(_gdn_kernel, n_chunks),
        grid=(bsz, steps),
        in_specs=[rowblk(0), rowblk(1), rowblk(2),
                  pl.BlockSpec((rows, SEG), lambda b, s: (b * steps + s, SEG_GZ)),
                  pl.BlockSpec((rows, 2 * LANES), lambda b, s: (b * steps + s, ab_off // (2 * LANES))),
                  pl.BlockSpec((2 * SUBLANES, rows), lambda b, s: (0, b * steps + s)),
                  pl.BlockSpec((1, LANES), lambda b, s: (0, 0)),
                  pl.BlockSpec((1, LANES), lambda b, s: (0, 0)),
                  pl.BlockSpec((2 * SUBLANES, 1), lambda b, s: (0, 0)),
                  pl.BlockSpec((2 * SUBLANES, 1), lambda b, s: (0, 0)),
                  pl.BlockSpec((1, HEAD_DIM), lambda b, s: (0, 0))],
        out_specs=pl.BlockSpec((rows, GDN_WIDTH), lambda b, s: (b * steps + s, 0)),
        out_shape=jax.ShapeDtypeStruct((t, GDN_WIDTH), BF16),
        scratch_shapes=[pltpu.VMEM((GDN_HEADS, HEAD_DIM, HEAD_DIM), F32)],
        compiler_params=_cparams("parallel", "arbitrary"), name="gdn",
    )(qkv, qkv, qkv, proj, proj, abt, lane_vec(a_log), lane_vec(dt_bias),
      sublane_vec(a_log), sublane_vec(dt_bias), gnorm_w.reshape(1, HEAD_DIM).astype(F32))


def _outproj_kernel(with_router, a1_ref, a2_ref, a3_ref, w1_ref, w2_ref, w3_ref, h_ref, gate_ref,
                    nw_ref, scale_ref, shift_ref, *rest):
    if with_router:
        wr_ref, hres_ref, hn_ref, route_ref = rest
    else:
        hres_ref, hn_ref = rest
    mix = _dot(a1_ref[...], w1_ref[...]) + _dot(a2_ref[...], w2_ref[...]) + _dot(a3_ref[...], w3_ref[...])
    h = h_ref[...] + gate_ref[...] * mix
    hres_ref[...] = h
    hn = _rms_mod(h, nw_ref[...], scale_ref[...], shift_ref[...])
    hn_ref[...] = hn.astype(hn_ref.dtype)
    if with_router:
        logits = jnp.dot(hn, wr_ref[...], precision=HIGHEST, preferred_element_type=F32)
        lane = lax.broadcasted_iota(jnp.int32, logits.shape, 1)
        lg = jnp.where(lane < N_EXPERTS, logits, -jnp.inf)
        m1 = jnp.max(lg, axis=-1, keepdims=True)
        i1 = jnp.min(jnp.where(lg == m1, lane, LANES), axis=-1, keepdims=True)
        lg2 = jnp.where(lane == i1, -jnp.inf, lg)
        m2 = jnp.max(lg2, axis=-1, keepdims=True)
        i2 = jnp.min(jnp.where(lg2 == m2, lane, LANES), axis=-1, keepdims=True)
        e = jnp.exp(m2 - m1)
        g1 = 1.0 / (1.0 + e)
        g2 = e / (1.0 + e)
        route_ref[...] = jnp.where(lane == 0, i1.astype(F32),
                                   jnp.where(lane == 1, i2.astype(F32),
                                             jnp.where(lane == 2, g1, jnp.where(lane == 3, g2, 0.0))))


def _outproj(a1, a2, a3, w_out, h_res, seq, gate, norm_w, scale, shift, w_router):
    t, d = h_res.shape
    tm = min(256, t)
    with_router = w_router is not None
    k1, k2 = a1.shape[1], a2.shape[1]
    w1, w2, w3 = w_out[:k1], w_out[k1:k1 + k2], w_out[k1 + k2:]

    def rows(w):
        return pl.BlockSpec((tm, w), lambda i: (i, 0))

    def whole(a):
        return pl.BlockSpec(a.shape, lambda i: (0, 0))

    per_batch = pl.BlockSpec((None, 1, d), lambda i: ((i * tm) // seq, 0, 0))
    nw = norm_w.reshape(1, d).astype(F32)
    args = [a1, a2, a3, w1, w2, w3, h_res, gate, nw, scale, shift]
    specs = [rows(k1), rows(k2), rows(a3.shape[1]), whole(w1), whole(w2), whole(w3), rows(d), per_batch,
             whole(nw), per_batch, per_batch]
    out_shape = [jax.ShapeDtypeStruct((t, d), F32), jax.ShapeDtypeStruct((t, d), BF16)]
    out_specs = [rows(d), rows(d)]
    if with_router:
        wr = jnp.zeros((d, LANES), F32).at[:, :N_EXPERTS].set(w_router.astype(F32))
        args.append(wr)
        specs.append(whole(wr))
        out_shape.append(jax.ShapeDtypeStruct((t, LANES), F32))
        out_specs.append(rows(LANES))
    return pl.pallas_call(
        functools.partial(_outproj_kernel, with_router),
        grid=(t // tm,), in_specs=specs, out_specs=out_specs, out_shape=out_shape,
        compiler_params=_cparams("parallel"), name="out_proj",
    )(*args)


def _ffn_kernel(row_gated, te_ref, nv_ref, x_ref, w1_ref, w3_ref, w2_ref, *rest):
    if row_gated:
        rg_ref, o_ref, acc_ref = rest
    else:
        o_ref, acc_ref = rest
    i = pl.program_id(0)
    f = pl.program_id(1)

    @pl.when(f == 0)
    def _():
        acc_ref[...] = jnp.zeros_like(acc_ref)

    @pl.when(i < nv_ref[0])
    def _():
        x = x_ref[...]
        h1 = _dot(x, w1_ref[...])
        h3 = _dot(x, w3_ref[...])
        acc_ref[...] += _dot((_silu(h1) * h3).astype(BF16), w2_ref[...])

    @pl.when(f == pl.num_programs(1) - 1)
    def _():
        y = acc_ref[...]
        if row_gated:
            y = y * rg_ref[...]
        o_ref[...] = y.astype(o_ref.dtype)


def _ffn(x, w1, w3, w2, tile_expert, n_valid, row_gate, tm, tf):
    r, d = x.shape
    ff = w1.shape[2]
    tm, tf = min(tm, r), min(tf, ff)
    in_specs = [pl.BlockSpec((tm, d), lambda i, f, te, nv: (i, 0)),
                pl.BlockSpec((None, d, tf), lambda i, f, te, nv: (te[i], 0, f)),
                pl.BlockSpec((None, d, tf), lambda i, f, te, nv: (te[i], 0, f)),
                pl.BlockSpec((None, tf, d), lambda i, f, te, nv: (te[i], f, 0))]
    args = [tile_expert, n_valid, x, w1, w3, w2]
    if row_gate is not None:
        in_specs.append(pl.BlockSpec((tm, 1), lambda i, f, te, nv: (i, 0)))
        args.append(row_gate)
    grid_spec = pltpu.PrefetchScalarGridSpec(
        num_scalar_prefetch=2, grid=(r // tm, ff // tf), in_specs=in_specs,
        out_specs=pl.BlockSpec((tm, d), lambda i, f, te, nv: (i, 0)),
        scratch_shapes=[pltpu.VMEM((tm, d), F32)])
    return pl.pallas_call(
        functools.partial(_ffn_kernel, row_gate is not None), grid_spec=grid_spec,
        out_shape=jax.ShapeDtypeStruct((r, d), BF16),
        compiler_params=_cparams("parallel", "arbitrary"), name="ffn",
    )(*args)


MOE_TILE = 512


def _moe_plan(route, tm):
    t = route.shape[0]
    e_flat = route[:, :2].astype(jnp.int32).reshape(-1)
    onehot = (e_flat[:, None] == jnp.arange(N_EXPERTS, dtype=jnp.int32)[None, :]).astype(jnp.int32)
    counts = jnp.sum(onehot, axis=0)
    rank = jnp.sum((jnp.cumsum(onehot, axis=0) - onehot) * onehot, axis=1)
    padded = ((counts + tm - 1) // tm) * tm
    ends = jnp.cumsum(padded)
    pos = (ends - padded)[e_flat] + rank
    n_rows = 2 * t + N_EXPERTS * tm
    n_tiles, n_blocks = n_rows // tm, t // tm
    src_tok = jnp.full((n_rows,), -1, jnp.int32).at[pos].set(jnp.arange(2 * t, dtype=jnp.int32) // 2)
    row_gate = jnp.zeros((n_rows,), F32).at[pos].set(route[:, 2:4].reshape(-1))
    n_valid = (ends[-1] // tm).astype(jnp.int32)
    tile_start = jnp.arange(n_tiles, dtype=jnp.int32) * tm
    tile_expert = jnp.sum((tile_start[:, None] >= ends[None, :]).astype(jnp.int32), axis=1)
    last_expert = jnp.sum((tile_start[n_valid - 1] >= ends).astype(jnp.int32))
    tile_expert = jnp.where(tile_start < ends[-1], tile_expert, last_expert).astype(jnp.int32)

    blk = jnp.where(src_tok >= 0, src_tok // tm, -1).reshape(n_tiles, tm, 1)
    incidence = jnp.any(blk == jnp.arange(n_blocks, dtype=jnp.int32)[None, None, :], axis=1)
    w_max = n_tiles + N_EXPERTS * n_blocks

    def work_list(m):
        flat = m.reshape(-1)
        n = jnp.sum(flat.astype(jnp.int32))
        idx = jnp.nonzero(flat, size=w_max, fill_value=0)[0].astype(jnp.int32)
        w = jnp.arange(w_max, dtype=jnp.int32)
        valid = w < n
        idx = jnp.where(valid, idx, idx[n - 1])
        major, minor = idx // m.shape[1], idx % m.shape[1]
        first = valid & ((w == 0) | (major != jnp.roll(major, 1)))
        last = valid & ((w == n - 1) | (major != jnp.roll(major, -1)))
        return [a.astype(jnp.int32) for a in (major, minor, first, last, valid)]

    return (src_tok, row_gate.reshape(n_rows, 1), tile_expert, n_valid.reshape(1),
            work_list(incidence), work_list(incidence.T))


def _dispatch_kernel(wi_ref, wj_ref, first_ref, last_ref, valid_ref, x_ref, tok_ref, o_ref):
    w = pl.program_id(0)
    tm = x_ref.shape[0]

    @pl.when(valid_ref[w] == 1)
    def _():
        col_tok = lax.broadcasted_iota(jnp.int32, (tm, tm), 1) + wj_ref[w] * tm
        onehot = jnp.where(tok_ref[...] == col_tok, 1.0, 0.0).astype(BF16)
        picked = _dot(onehot, x_ref[...]).astype(o_ref.dtype)

        @pl.when(first_ref[w] == 1)
        def _():
            o_ref[...] = picked

        @pl.when(first_ref[w] == 0)
        def _():
            o_ref[...] += picked


def _dispatch(x, src_tok, work, tm):
    t, d = x.shape
    n_rows = src_tok.shape[0]
    grid_spec = pltpu.PrefetchScalarGridSpec(
        num_scalar_prefetch=5, grid=(work[0].shape[0],),
        in_specs=[pl.BlockSpec((tm, d), lambda w, wi, wj, *_: (wj[w], 0)),
                  pl.BlockSpec((tm, 1), lambda w, wi, wj, *_: (wi[w], 0))],
        out_specs=pl.BlockSpec((tm, d), lambda w, wi, wj, *_: (wi[w], 0)))
    return pl.pallas_call(
        _dispatch_kernel, grid_spec=grid_spec,
        out_shape=jax.ShapeDtypeStruct((n_rows, d), x.dtype),
        compiler_params=_cparams("arbitrary"), name="moe_dispatch",
    )(*work, x, src_tok.reshape(n_rows, 1))


def _combine_kernel(has_mod, emit_res, vj_ref, vi_ref, first_ref, last_ref, valid_ref, y_ref, tok_ref, h_ref,
                    gate_ref, nw_ref, *rest):
    rest = list(rest)
    scale_ref = shift_ref = None
    if has_mod:
        scale_ref, shift_ref = rest.pop(0), rest.pop(0)
    hres_ref = rest.pop(0) if emit_res else None
    o_ref, acc_ref = rest
    w = pl.program_id(0)
    tm = y_ref.shape[0]

    @pl.when(valid_ref[w] == 1)
    def _():
        row_tok = lax.broadcasted_iota(jnp.int32, (tm, tm), 0) + vj_ref[w] * tm
        onehot = jnp.where(tok_ref[...] == row_tok, 1.0, 0.0).astype(BF16)
        part = _dot(onehot, y_ref[...])

        @pl.when(first_ref[w] == 1)
        def _():
            acc_ref[...] = part

        @pl.when(first_ref[w] == 0)
        def _():
            acc_ref[...] += part

        @pl.when(last_ref[w] == 1)
        def _():
            h = h_ref[...] + gate_ref[...] * acc_ref[...]
            if emit_res:
                hres_ref[...] = h
            scale = scale_ref[...] if has_mod else None
            shift = shift_ref[...] if has_mod else None
            o_ref[...] = _rms_mod(h, nw_ref[...], scale, shift).astype(o_ref.dtype)


def _combine_resnorm(ys, src_tok, work, tm, h_res, seq, gate, norm_w, scale, shift, emit_res, out_dtype):
    t, d = h_res.shape
    n_tiles = ys.shape[0] // tm
    has_mod = scale is not None

    def tok_rows(w, vj, vi, *_):
        return (vj[w], 0)

    per_batch = pl.BlockSpec((None, 1, d), lambda w, vj, *_: ((vj[w] * tm) // seq, 0, 0))
    in_specs = [pl.BlockSpec((tm, d), lambda w, vj, vi, *_: (vi[w], 0)),
                pl.BlockSpec((None, 1, tm), lambda w, vj, vi, *_: (vi[w], 0, 0)),
                pl.BlockSpec((tm, d), tok_rows), per_batch,
                pl.BlockSpec((1, d), lambda w, *_: (0, 0))]
    args = list(work) + [ys, src_tok.reshape(n_tiles, 1, tm), h_res, gate, norm_w.reshape(1, d)]
    if has_mod:
        in_specs += [per_batch, per_batch]
        args += [scale, shift]
    out_shape, out_specs = [], []
    if emit_res:
        out_shape.append(jax.ShapeDtypeStruct((t, d), F32))
        out_specs.append(pl.BlockSpec((tm, d), tok_rows))
    out_shape.append(jax.ShapeDtypeStruct((t, d), out_dtype))
    out_specs.append(pl.BlockSpec((tm, d), tok_rows))
    grid_spec = pltpu.PrefetchScalarGridSpec(
        num_scalar_prefetch=5, grid=(work[0].shape[0],), in_specs=in_specs, out_specs=out_specs,
        scratch_shapes=[pltpu.VMEM((tm, d), F32)])
    res = pl.pallas_call(
        functools.partial(_combine_kernel, has_mod, emit_res), grid_spec=grid_spec, out_shape=out_shape,
        compiler_params=_cparams("arbitrary"), name="moe_combine",
    )(*args)
    return res if emit_res else res[0]


def _in_proj_weight(w_in_l):
    o = 0
    u = w_in_l[:, o:o + SSM_WIDTH]; o += SSM_WIDTH
    dq = w_in_l[:, o:o + DIFF_WIDTH]; o += DIFF_WIDTH
    dk = w_in_l[:, o:o + DIFF_WIDTH]; o += DIFF_WIDTH
    dv = w_in_l[:, o:o + DIFF_WIDTH]; o += DIFF_WIDTH
    gqkv = w_in_l[:, o:o + 3 * GDN_WIDTH]; o += 3 * GDN_WIDTH
    gz = w_in_l[:, o:o + GDN_WIDTH]; o += GDN_WIDTH
    gab = w_in_l[:, o:]
    pad = jnp.zeros((w_in_l.shape[0], SEG - SSM_WIDTH - gab.shape[1]), w_in_l.dtype)
    return jnp.concatenate([gqkv, gz, dq, dk, dv, u, gab, pad], axis=1).astype(BF16)


def kernel(x, c, positions, w_ada, b_ada, norm_mix, norm_ffn, norm_final, w_in, w_out, ssm_a_re, ssm_a_im, ssm_log_dt, ssm_b_re, ssm_b_im, ssm_c_re, ssm_c_im, ssm_d, ssm_w_glu, diff_lam_q1, diff_lam_k1, diff_lam_q2, diff_lam_k2, diff_subln, gdn_conv, gdn_a_log, gdn_dt_bias, gdn_norm, ffn_w1, ffn_w3, ffn_w2, moe_router, moe_w1, moe_w3, moe_w2):
    bsz, seq, d = x.shape
    t = bsz * seq
    depth = w_in.shape[0]
    h_res = x.astype(F32).reshape(t, d)

    c_pad = jnp.zeros((SUBLANES, d), F32).at[:bsz].set(c.astype(F32))
    mod = _ada(c_pad, w_ada, b_ada)[:, :bsz]
    mods = [[m.reshape(bsz, 1, d) for m in jnp.split(mod[l], 6, axis=-1)] for l in range(depth)]
    rope_tables = _rope_tables(positions)

    hn = _resnorm(h_res, seq, [], None, norm_mix[0].astype(F32), mods[0][1], mods[0][0], False, BF16)
    out = None
    for l in range(depth):
        shift1, scale1, gate1, shift2, scale2, gate2 = mods[l]
        is_moe = l % 2 == 1
        proj = _matmul(hn, _in_proj_weight(w_in[l]), BF16, 512, PROJ_WIDTH // 2)

        tables = _s5_tables(ssm_a_re[l], ssm_a_im[l], ssm_log_dt[l], ssm_b_re[l], ssm_b_im[l],
                            ssm_c_re[l], ssm_c_im[l])
        y_ssm = _s5_mixer(proj, bsz, seq, tables, ssm_d[l], ssm_w_glu[l].astype(BF16))

        lambda_init = 0.8 - 0.6 * math.exp(-0.3 * l)
        qk = _rope(proj, rope_tables)
        y_diff = _diff_attention(qk, proj, bsz, seq,
                                 (diff_lam_q1[l], diff_lam_k1[l], diff_lam_q2[l], diff_lam_k2[l]),
                                 diff_subln[l], lambda_init)

        qkv = _gdn_conv(proj, gdn_conv[l], seq)
        y_gdn = _gdn(qkv, proj, bsz, seq, gdn_a_log[l], gdn_dt_bias[l], gdn_norm[l])

        res = _outproj(y_ssm, y_diff, y_gdn, w_out[l].astype(BF16), h_res, seq, gate1,
                       norm_ffn[l], scale2, shift2, moe_router[l // 2] if is_moe else None)
        h_res, hn2 = res[0], res[1]

        last = l + 1 == depth
        if last:
            nxt = (norm_final.astype(F32), None, None, False, x.dtype)
        else:
            nxt = (norm_mix[l + 1].astype(F32), mods[l + 1][1], mods[l + 1][0], True, BF16)
        if is_moe:
            src_tok, row_gate, tile_expert, n_valid, work_sorted, work_token = _moe_plan(res[2], MOE_TILE)
            xs = _dispatch(hn2, src_tok, work_sorted, MOE_TILE)
            ys = _ffn(xs, moe_w1[l // 2].astype(BF16), moe_w3[l // 2].astype(BF16),
                      moe_w2[l // 2].astype(BF16), tile_expert, n_valid, row_gate, MOE_TILE, 512)
            res = _combine_resnorm(ys, src_tok, work_token, MOE_TILE, h_res, seq, gate2, *nxt)
        else:
            n_tiles = t // min(512, t)
            ys = _ffn(hn2, ffn_w1[l // 2:l // 2 + 1].astype(BF16), ffn_w3[l // 2:l // 2 + 1].astype(BF16),
                      ffn_w2[l // 2:l // 2 + 1].astype(BF16), jnp.zeros((n_tiles,), jnp.int32),
                      jnp.full((1,), n_tiles, jnp.int32), None, 512, 512)
            res = _resnorm(h_res, seq, [(ys, None)], gate2, *nxt)
        if last:
            out = res
        else:
            h_res, hn = res
    return out.reshape(bsz, seq, d)
```

```python
import functools
import math

import numpy as np
import jax
import jax.numpy as jnp
from jax import lax
from jax.experimental import pallas as pl
from jax.experimental.pallas import tpu as pltpu

F32 = jnp.float32
BF16 = jnp.bfloat16
HIGHEST = lax.Precision.HIGHEST

D_MODEL = 2048
SSM_WIDTH = 512
SSM_CH = 16
SSM_GROUPS = SSM_WIDTH // SSM_CH
SSM_STATE = 64
SSM_CHUNK = 16
DIFF_WIDTH = 768
HEAD_DIM = 128
DIFF_HEADS = DIFF_WIDTH // HEAD_DIM
DIFF_QK_DIM = HEAD_DIM // 2
GDN_WIDTH = 768
GDN_HEADS = GDN_WIDTH // HEAD_DIM
CONV_WIDTH = 4
GDN_CHUNK = 64
ROPE_THETA = 500000.0
ROPE_DIM = DIFF_QK_DIM // 4
ROPE_HALF = ROPE_DIM // 2
N_EXPERTS = 8
EPS = 1e-6
LANES = 128
SUBLANES = 8

SEG = 768
SEG_GQ, SEG_GK, SEG_GV, SEG_GZ, SEG_DQ, SEG_DK, SEG_DV, SEG_U = range(8)
PROJ_WIDTH = 8 * SEG
AB_OFF = SSM_WIDTH

VMEM_LIMIT = 56 * 1024 * 1024


def _cparams(*sem):
    return pltpu.CompilerParams(dimension_semantics=sem, vmem_limit_bytes=VMEM_LIMIT)


def _dot(a, b):
    return jnp.dot(a, b, preferred_element_type=F32)


def _dot_nt(a, b):
    return lax.dot_general(a, b, (((1,), (1,)), ((), ())), preferred_element_type=F32)


def _dot_tn(a, b):
    return lax.dot_general(a, b, (((0,), (0,)), ((), ())), preferred_element_type=F32)


def _sigmoid(x):
    return 1.0 / (1.0 + jnp.exp(-x))


def _silu(x):
    return x * _sigmoid(x)


def _ada_kernel(c_ref, w_ref, b_ref, o_ref):
    cond = _silu(c_ref[...])
    o_ref[...] = _dot(cond.astype(BF16), w_ref[...].astype(BF16)) + b_ref[...]


def _ada(c_pad, w_ada, b_ada):
    depth, d, n = w_ada.shape
    tn = 1024
    return pl.pallas_call(
        _ada_kernel,
        grid=(depth, n // tn),
        in_specs=[pl.BlockSpec((SUBLANES, d), lambda l, j: (0, 0)),
                  pl.BlockSpec((None, d, tn), lambda l, j: (l, 0, j)),
                  pl.BlockSpec((None, 1, tn), lambda l, j: (l, 0, j))],
        out_specs=pl.BlockSpec((None, SUBLANES, tn), lambda l, j: (l, 0, j)),
        out_shape=jax.ShapeDtypeStruct((depth, SUBLANES, n), F32),
        compiler_params=_cparams("arbitrary", "arbitrary"),
        name="ada",
    )(c_pad, w_ada, b_ada.reshape(depth, 1, n))


def _rms_mod(h, w, scale, shift):
    y = h * lax.rsqrt(jnp.mean(h * h, axis=-1, keepdims=True) + EPS) * w
    if scale is not None:
        y = y * (1.0 + scale) + shift
    return y


def _resnorm_kernel(n_delta, row_gated, has_mod, emit_res, *refs):
    refs = list(refs)
    h = refs.pop(0)[...]
    if n_delta:
        delta = None
        for _ in range(n_delta):
            y = refs.pop(0)[...].astype(F32)
            if row_gated:
                y = y * refs.pop(0)[...]
            delta = y if delta is None else delta + y
        h = h + refs.pop(0)[...] * delta
    w = refs.pop(0)[...]
    scale = shift = None
    if has_mod:
        scale = refs.pop(0)[...]
        shift = refs.pop(0)[...]
    if emit_res:
        refs.pop(0)[...] = h
    o_ref = refs.pop(0)
    o_ref[...] = _rms_mod(h, w, scale, shift).astype(o_ref.dtype)


def _resnorm(h_res, seq, deltas, gate, norm_w, scale, shift, emit_res, out_dtype):
    t, d = h_res.shape
    tm = min(256, t)
    row = pl.BlockSpec((tm, d), lambda i: (i, 0))
    per_batch = pl.BlockSpec((None, 1, d), lambda i: ((i * tm) // seq, 0, 0))
    row_gated = bool(deltas) and deltas[0][1] is not None
    args, specs = [h_res], [row]
    for y, rg in deltas:
        args.append(y)
        specs.append(row)
        if row_gated:
            args.append(rg)
            specs.append(pl.BlockSpec((tm, 1), lambda i: (i, 0)))
    if deltas:
        args.append(gate)
        specs.append(per_batch)
    args.append(norm_w.reshape(1, d))
    specs.append(pl.BlockSpec((1, d), lambda i: (0, 0)))
    if scale is not None:
        args += [scale, shift]
        specs += [per_batch, per_batch]
    out_shape, out_specs = [], []
    if emit_res:
        out_shape.append(jax.ShapeDtypeStruct((t, d), F32))
        out_specs.append(row)
    out_shape.append(jax.ShapeDtypeStruct((t, d), out_dtype))
    out_specs.append(row)
    res = pl.pallas_call(
        functools.partial(_resnorm_kernel, len(deltas), row_gated, scale is not None, emit_res),
        grid=(t // tm,), in_specs=specs, out_specs=out_specs, out_shape=out_shape,
        compiler_params=_cparams("parallel"), name="resnorm",
    )(*args)
    return res if emit_res else res[0]


IN_BLOCK = 2 * LANES


def _in_proj_blocks():
    src_segments = [(SEG_U, SSM_WIDTH), (SEG_DQ, DIFF_WIDTH), (SEG_DK, DIFF_WIDTH), (SEG_DV, DIFF_WIDTH),
                    (SEG_GQ, 3 * GDN_WIDTH), (SEG_GZ, GDN_WIDTH)]
    dest = []
    for seg, width in src_segments:
        dest += [seg * SEG // IN_BLOCK + b for b in range(width // IN_BLOCK)]
    dest.append((SEG_U * SEG + AB_OFF) // IN_BLOCK)
    return np.asarray(dest, np.int32)


def _in_proj_kernel(n_cols, dest_ref, x_ref, w_ref, o_ref, wb_ref):
    j = pl.program_id(0)

    @pl.when(pl.program_id(1) == 0)
    def _():
        col = lax.broadcasted_iota(jnp.int32, w_ref.shape, 1) + j * IN_BLOCK
        wb_ref[...] = jnp.where(col < n_cols, w_ref[...], 0.0).astype(BF16)

    o_ref[...] = _dot(x_ref[...], wb_ref[...]).astype(o_ref.dtype)


def _in_proj(x, w_in_l):
    m, k = x.shape
    n_cols = w_in_l.shape[1]
    dest = _in_proj_blocks()
    tm = min(2048, m)
    grid_spec = pltpu.PrefetchScalarGridSpec(
        num_scalar_prefetch=1, grid=(dest.shape[0], m // tm),
        in_specs=[pl.BlockSpec((tm, k), lambda j, i, dest: (i, 0)),
                  pl.BlockSpec((k, IN_BLOCK), lambda j, i, dest: (0, j))],
        out_specs=pl.BlockSpec((tm, IN_BLOCK), lambda j, i, dest: (i, dest[j])),
        scratch_shapes=[pltpu.VMEM((k, IN_BLOCK), BF16)])
    return pl.pallas_call(
        functools.partial(_in_proj_kernel, n_cols), grid_spec=grid_spec,
        out_shape=jax.ShapeDtypeStruct((m, PROJ_WIDTH), BF16),
        compiler_params=_cparams("arbitrary", "arbitrary"), name="in_proj",
    )(jnp.asarray(dest), x, w_in_l)


def _s5_tables(a_re, a_im, log_dt, b_re, b_im, c_re, c_im):
    L, G, P, H = SSM_CHUNK, SSM_GROUPS, SSM_STATE, SSM_CH
    lam = lax.complex(a_re.astype(F32), a_im.astype(F32))
    dt = jnp.exp(log_dt.astype(F32))[:, None]
    lam_bar = jnp.exp(lam * dt)
    b_bar = ((lam_bar - 1.0) / lam)[:, :, None] * lax.complex(b_re.astype(F32), b_im.astype(F32))
    c_mat = lax.complex(c_re.astype(F32), c_im.astype(F32))
    steps = jnp.arange(L + 1, dtype=F32)
    pw = jnp.exp((lam * dt)[:, None, :] * steps[None, :, None])
    kern = jnp.real(jnp.einsum('ghp,gjp,gpi->gjih', c_mat, pw[:, :L], b_bar))
    s_idx = jnp.arange(L)[:, None]
    t_idx = jnp.arange(L)[None, :]
    lag = t_idx - s_idx
    tm = jnp.where((lag >= 0)[None, :, :, None, None], kern[:, jnp.clip(lag, 0, L - 1)], 0.0)
    tm = tm.transpose(0, 1, 3, 2, 4).reshape(G, L * H, L * H)
    zc = pw[:, L - 1 - jnp.arange(L)][:, :, :, None] * b_bar[:, None]
    zc = zc.transpose(0, 1, 3, 2).reshape(G, L * H, P)
    cl = c_mat[:, None] * pw[:, 1:L + 1][:, :, None, :]
    cl = cl.transpose(0, 3, 1, 2).reshape(G, P, L * H)
    a_l = pw[:, L]
    return (tm.astype(BF16), jnp.real(zc).astype(BF16), jnp.imag(zc).astype(BF16),
            jnp.real(cl).astype(BF16), (-jnp.imag(cl)).astype(BF16),
            jnp.real(a_l)[:, None, :], jnp.imag(a_l)[:, None, :])


def _s5_core_kernel(nb, u_ref, tm_ref, zre_ref, zim_ref, yre_ref, yim_ref, are_ref, aim_ref,
                    o_ref, xre_ref, xim_ref):
    u = u_ref[...]
    z_re = _dot(u, zre_ref[...])
    z_im = _dot(u, zim_ref[...])
    a_re = are_ref[...]
    a_im = aim_ref[...]
    n_chunks = u.shape[0] // nb
    s_re = jnp.zeros((nb, SSM_STATE), F32)
    s_im = jnp.zeros((nb, SSM_STATE), F32)
    for c in range(n_chunks):
        xre_ref[c * nb:(c + 1) * nb, :] = s_re
        xim_ref[c * nb:(c + 1) * nb, :] = s_im
        zr = z_re[c * nb:(c + 1) * nb, :]
        zi = z_im[c * nb:(c + 1) * nb, :]
        s_re, s_im = a_re * s_re - a_im * s_im + zr, a_re * s_im + a_im * s_re + zi
    y = _dot(u, tm_ref[...])
    y = y + _dot(xre_ref[...].astype(BF16), yre_ref[...])
    y = y + _dot(xim_ref[...].astype(BF16), yim_ref[...])
    o_ref[...] = y.astype(o_ref.dtype)


def _s5_core(ug, tables, nb):
    g, r, w = ug.shape
    tm, zre, zim, yre, yim, are, aim = tables
    p = SSM_STATE

    def grp(*shape):
        return pl.BlockSpec((None,) + shape, lambda i: (i,) + (0,) * len(shape))

    return pl.pallas_call(
        functools.partial(_s5_core_kernel, nb),
        grid=(g,),
        in_specs=[grp(r, w), grp(w, w), grp(w, p), grp(w, p), grp(p, w), grp(p, w), grp(1, p), grp(1, p)],
        out_specs=grp(r, w),
        out_shape=jax.ShapeDtypeStruct((g, r, w), BF16),
        scratch_shapes=[pltpu.VMEM((r, p), F32), pltpu.VMEM((r, p), F32)],
        compiler_params=_cparams("parallel"), name="s5_core",
    )(ug, tm, zre, zim, yre, yim, are, aim)


def _gelu_tanh(x):
    return 0.5 * x * (1.0 + jnp.tanh(math.sqrt(2.0 / math.pi) * (x + 0.044715 * (x * x * x))))


def _s5_post_kernel(y_ref, u_ref, d_ref, w_ref, o_ref):
    u = u_ref[:, :SSM_WIDTH].astype(F32)
    y = _gelu_tanh(y_ref[...].astype(F32) + d_ref[...] * u)
    o_ref[...] = (y * _sigmoid(_dot(y.astype(BF16), w_ref[...]))).astype(o_ref.dtype)


def _s5_post(y_core, proj, d_skip, w_glu):
    t = y_core.shape[0]
    tm = min(512, t)
    return pl.pallas_call(
        _s5_post_kernel,
        grid=(t // tm,),
        in_specs=[pl.BlockSpec((tm, SSM_WIDTH), lambda i: (i, 0)),
                  pl.BlockSpec((tm, SEG), lambda i: (i, SEG_U)),
                  pl.BlockSpec((1, SSM_WIDTH), lambda i: (0, 0)),
                  pl.BlockSpec((SSM_WIDTH, SSM_WIDTH), lambda i: (0, 0))],
        out_specs=pl.BlockSpec((tm, SSM_WIDTH), lambda i: (i, 0)),
        out_shape=jax.ShapeDtypeStruct((t, SSM_WIDTH), BF16),
        compiler_params=_cparams("parallel"), name="s5_post",
    )(y_core, proj, d_skip.reshape(1, SSM_WIDTH).astype(F32), w_glu)


def _s5_mixer(proj, bsz, seq, tables, d_skip, w_glu):
    L, G, H = SSM_CHUNK, SSM_GROUPS, SSM_CH
    t = bsz * seq
    nc = seq // L
    u = proj[:, SEG_U * SEG:SEG_U * SEG + SSM_WIDTH]
    ug = u.reshape(bsz, nc, L, G, H).transpose(3, 1, 0, 2, 4).reshape(G, nc * bsz, L * H)
    yg = _s5_core(ug, tables, bsz)
    y_core = yg.reshape(G, nc, bsz, L, H).transpose(2, 1, 3, 0, 4).reshape(t, SSM_WIDTH)
    return _s5_post(y_core, proj, d_skip, w_glu)


def _rope_tables(positions):
    inv_freq = ROPE_THETA ** (-jnp.arange(0, ROPE_DIM, 2, dtype=F32) / ROPE_DIM)
    ang = positions.astype(F32).reshape(-1)[:, None] * inv_freq
    cos, sin = jnp.cos(ang), jnp.sin(ang)
    r = np.arange(LANES) % DIFF_QK_DIM
    first = jnp.asarray(r < ROPE_HALF)[None, :]
    second = jnp.asarray((r >= ROPE_HALF) & (r < ROPE_DIM))[None, :]
    idx = jnp.asarray(r % ROPE_HALF)
    cos_l, sin_l = cos[:, idx], sin[:, idx]
    cosf = jnp.where(first | second, cos_l, 1.0)
    sin_a = jnp.where(first, -sin_l, 0.0)
    sin_b = jnp.where(second, sin_l, 0.0)
    return cosf, sin_a, sin_b


def _rope_kernel(x_ref, c_ref, sa_ref, sb_ref, o_ref):
    cosf, sin_a, sin_b = c_ref[...], sa_ref[...], sb_ref[...]
    n_slabs = x_ref.shape[1] // LANES
    for s in range(n_slabs):
        x = x_ref[:, s * LANES:(s + 1) * LANES].astype(F32)
        y = (x * cosf + pltpu.roll(x, LANES - ROPE_HALF, axis=1) * sin_a
             + pltpu.roll(x, ROPE_HALF, axis=1) * sin_b)
        if s < n_slabs // 2:
            y = y * (DIFF_QK_DIM ** -0.5 * math.log2(math.e))
        o_ref[:, s * LANES:(s + 1) * LANES] = y.astype(o_ref.dtype)


def _rope(proj, tables):
    t = proj.shape[0]
    tm = min(512, t)
    w = 2 * DIFF_WIDTH
    tab = pl.BlockSpec((tm, LANES), lambda i: (i, 0))
    return pl.pallas_call(
        _rope_kernel,
        grid=(t // tm,),
        in_specs=[pl.BlockSpec((tm, w), lambda i: (i, SEG_DQ // 2)), tab, tab, tab],
        out_specs=pl.BlockSpec((tm, w), lambda i: (i, 0)),
        out_shape=jax.ShapeDtypeStruct((t, w), BF16),
        compiler_params=_cparams("parallel"), name="rope",
    )(proj, *tables)


def _attn_kernel(tq, lambda_init, q_ref, k_ref, v_ref, lq1_ref, lk1_ref, lq2_ref, lk2_ref, sw_ref, o_ref):
    qi = pl.program_id(2)
    q = q_ref[...]
    lane = lax.broadcasted_iota(jnp.int32, q.shape, 1)
    zero = jnp.zeros_like(q)
    qm = (jnp.where(lane < DIFF_QK_DIM, q, zero), jnp.where(lane >= DIFF_QK_DIM, q, zero))
    def block(kv, masked, carry):
        start = pl.multiple_of(kv * tq, tq)
        k = k_ref[pl.ds(start, tq), :]
        v = v_ref[pl.ds(start, tq), :]
        new = []
        for i in range(2):
            m_old, l_old, acc = carry[i]
            s = _dot_nt(qm[i], k)
            if masked:
                row = lax.broadcasted_iota(jnp.int32, s.shape, 0)
                col = lax.broadcasted_iota(jnp.int32, s.shape, 1)
                s = jnp.where(col <= row, s, -jnp.inf)
            m_new = jnp.maximum(m_old, jnp.max(s, axis=-1, keepdims=True))
            alpha = jnp.exp2(m_old - m_new)
            p = jnp.exp2(s - m_new)
            l_new = alpha * l_old + jnp.sum(p, axis=-1, keepdims=True)
            new.append((m_new, l_new, alpha * acc + _dot(p.astype(BF16), v)))
        return tuple(new)

    init = tuple((jnp.full((tq, 1), -jnp.inf, F32), jnp.zeros((tq, 1), F32), jnp.zeros((tq, HEAD_DIM), F32))
                 for _ in range(2))
    carry = lax.fori_loop(0, qi, lambda kv, c: block(kv, False, c), init)
    (_, l1, acc1), (_, l2, acc2) = block(qi, True, carry)
    lam = (jnp.exp(jnp.sum(lq1_ref[...] * lk1_ref[...], axis=-1, keepdims=True))
           - jnp.exp(jnp.sum(lq2_ref[...] * lk2_ref[...], axis=-1, keepdims=True)) + lambda_init)
    o = acc1 / l1 - lam * (acc2 / l2)
    o = o * lax.rsqrt(jnp.mean(o * o, axis=-1, keepdims=True) + EPS) * sw_ref[...] * (1.0 - lambda_init)
    o_ref[...] = o.astype(o_ref.dtype)


def _diff_attention(qk, proj, bsz, seq, lam_params, subln_w, lambda_init):
    tq = min(512, seq)
    nh = DIFF_HEADS
    qk3 = qk.reshape(bsz, seq, 2 * DIFF_WIDTH)
    proj3 = proj.reshape(bsz, seq, PROJ_WIDTH)
    vec = pl.BlockSpec((1, DIFF_QK_DIM), lambda b, h, i: (0, 0))
    out = pl.pallas_call(
        functools.partial(_attn_kernel, tq, lambda_init),
        grid=(bsz, nh, seq // tq),
        in_specs=[pl.BlockSpec((None, tq, HEAD_DIM), lambda b, h, i: (b, i, h)),
                  pl.BlockSpec((None, seq, HEAD_DIM), lambda b, h, i: (b, 0, nh + h)),
                  pl.BlockSpec((None, seq, HEAD_DIM), lambda b, h, i: (b, 0, SEG_DV * (SEG // HEAD_DIM) + h)),
                  vec, vec, vec, vec,
                  pl.BlockSpec((1, HEAD_DIM), lambda b, h, i: (0, 0))],
        out_specs=pl.BlockSpec((None, tq, HEAD_DIM), lambda b, h, i: (b, i, h)),
        out_shape=jax.ShapeDtypeStruct((bsz, seq, DIFF_WIDTH), BF16),
        compiler_params=_cparams("parallel", "parallel", "arbitrary"), name="diff_attn",
    )(qk3, qk3, proj3, *[p.reshape(1, DIFF_QK_DIM).astype(F32) for p in lam_params],
      subln_w.reshape(1, HEAD_DIM).astype(F32))
    return out.reshape(bsz * seq, DIFF_WIDTH)


HALO_ROWS = 16


def _conv_kernel(tiles_per_seq, x_ref, prev_ref, w_ref, o_ref):
    i = pl.program_id(0)
    j = pl.program_id(1)
    x = x_ref[...].astype(F32)
    prev = prev_ref[...].astype(F32)[HALO_ROWS - SUBLANES:]
    prev = jnp.where(i % tiles_per_seq == 0, jnp.zeros_like(prev), prev)
    w = w_ref[...]
    head_rows = lax.broadcasted_iota(jnp.int32, (SUBLANES, x.shape[1]), 0)
    y = x * w[CONV_WIDTH - 1:CONV_WIDTH, :]
    y_head = y[:SUBLANES]
    for back in range(1, CONV_WIDTH):
        wk = w[CONV_WIDTH - 1 - back:CONV_WIDTH - back, :]
        y = y + pltpu.roll(x, back, axis=0) * wk
        mixed = jnp.where(head_rows < back, pltpu.roll(prev, back, axis=0), pltpu.roll(x[:SUBLANES], back, axis=0))
        y_head = y_head + mixed * wk
    is_qk = j < 2 * GDN_WIDTH // x.shape[1]
    q_scale = jnp.where(j < GDN_WIDTH // x.shape[1], HEAD_DIM ** -0.5, 1.0)

    def finish(v):
        v = _silu(v)
        outs = []
        for h in range(v.shape[1] // HEAD_DIM):
            vh = v[:, h * HEAD_DIM:(h + 1) * HEAD_DIM]
            nrm = lax.rsqrt(jnp.sum(vh * vh, axis=-1, keepdims=True) + EPS) * q_scale
            outs.append(vh * jnp.where(is_qk, nrm, 1.0))
        return jnp.concatenate(outs, axis=1)

    o_ref[...] = finish(jnp.concatenate([y_head, y[SUBLANES:]], axis=0)).astype(o_ref.dtype)


def _gdn_conv(proj, conv_w, seq):
    t = proj.shape[0]
    tm = min(512, seq)
    cw = 2 * HEAD_DIM
    nj = 3 * GDN_WIDTH // cw
    halo = tm // HALO_ROWS
    return pl.pallas_call(
        functools.partial(_conv_kernel, seq // tm),
        grid=(t // tm, nj),
        in_specs=[pl.BlockSpec((tm, cw), lambda i, j: (i, j)),
                  pl.BlockSpec((HALO_ROWS, cw), lambda i, j: (jnp.maximum(i * halo - 1, 0), j)),
                  pl.BlockSpec((CONV_WIDTH, cw), lambda i, j: (0, j))],
        out_specs=pl.BlockSpec((tm, cw), lambda i, j: (i, j)),
        out_shape=jax.ShapeDtypeStruct((t, 3 * GDN_WIDTH), BF16),
        compiler_params=_cparams("parallel", "parallel"), name="gdn_conv",
    )(proj, proj, conv_w.astype(F32))


def _softplus(x):
    return jnp.maximum(x, 0.0) + jnp.log(1.0 + jnp.exp(-jnp.abs(x)))


def _gdn_kernel(n_chunks, q_ref, k_ref, v_ref, z_ref, ab_ref, abt_ref, alog_ref, dtb_ref, alog_t_ref, dtb_t_ref,
                gw_ref, o_ref, state_ref):
    C = GDN_CHUNK
    nh = GDN_HEADS
    rows = n_chunks * C

    @pl.when(pl.program_id(1) == 0)
    def _():
        state_ref[...] = jnp.zeros_like(state_ref)

    neg_a = -jnp.exp(alog_ref[...])
    g_col = neg_a * _softplus(ab_ref[:, :LANES].astype(F32) + dtb_ref[...])
    g_row = -jnp.exp(alog_t_ref[...]) * _softplus(abt_ref[...] + dtb_t_ref[...])
    r_i = lax.broadcasted_iota(jnp.int32, (rows, rows), 0)
    c_i = lax.broadcasted_iota(jnp.int32, (rows, rows), 1)
    same = (r_i // C) == (c_i // C)
    tri_l = jnp.where(same & (c_i <= r_i), 1.0, 0.0).astype(F32)
    tri_u = jnp.where(same & (r_i <= c_i), 1.0, 0.0).astype(F32)
    gc_all = jnp.dot(tri_l, g_col, precision=HIGHEST, preferred_element_type=F32)
    gr_all = jnp.dot(g_row, tri_u, precision=HIGHEST, preferred_element_type=F32)
    b_all = _sigmoid(ab_ref[:, :LANES].astype(F32))

    incl = same & (r_i >= c_i)
    strict = same & (r_i > c_i)
    eye = jnp.where(r_i == c_i, 1.0, 0.0).astype(F32)
    gw = gw_ref[...]

    heads = range(nh)
    sls = [slice(h * HEAD_DIM, (h + 1) * HEAD_DIM) for h in heads]
    q = [q_ref[:, sl] for sl in sls]
    k = [k_ref[:, sl] for sl in sls]
    kf = [x.astype(F32) for x in k]
    gcol = [gc_all[:, h:h + 1] for h in heads]
    beta = [b_all[:, nh + h:nh + h + 1] for h in heads]
    decay = [jnp.exp(jnp.where(incl, gcol[h] - gr_all[h:h + 1, :], -jnp.inf)) for h in heads]
    pw = [jnp.where(strict, -(beta[h] * _dot_nt(k[h], k[h]) * decay[h]), 0.0) for h in heads]
    t_mat = [eye + p for p in pw]
    pw = [p.astype(BF16) for p in pw]
    for _ in range(int(math.log2(C)) - 1):
        pw = [_dot(p, p).astype(BF16) for p in pw]
        t_mat = [t + _dot(t.astype(BF16), p) for t, p in zip(t_mat, pw)]
    eg = [jnp.exp(g) for g in gcol]
    uw = [_dot(t_mat[h].astype(BF16),
               jnp.concatenate([v_ref[:, sls[h]].astype(F32) * beta[h], kf[h] * (beta[h] * eg[h])],
                               axis=1).astype(BF16)) for h in heads]
    u_all = [x[:, :HEAD_DIM] for x in uw]
    w_all = [x[:, HEAD_DIM:].astype(BF16) for x in uw]
    qk_all = [jnp.where(incl, _dot_nt(q[h], k[h]) * decay[h], 0.0).astype(BF16) for h in heads]
    qe_all = [(q[h].astype(F32) * eg[h]).astype(BF16) for h in heads]
    g_last = [[g[(c + 1) * C - 1:(c + 1) * C, :] for c in range(n_chunks)] for g in gcol]
    k_dec = [(kf[h] * jnp.exp(jnp.concatenate([jnp.broadcast_to(g, (C, 1)) for g in g_last[h]], axis=0)
                              - gcol[h])).astype(BF16) for h in heads]

    state = [state_ref[h] for h in heads]
    for c in range(n_chunks):
        rs = slice(c * C, (c + 1) * C)
        sb = [s.astype(BF16) for s in state]
        v_new = [(u_all[h][rs] - _dot(w_all[h][rs], sb[h])).astype(BF16) for h in heads]
        o = [_dot(qe_all[h][rs], sb[h]) + _dot(qk_all[h][rs, rs], v_new[h]) for h in heads]
        state = [state[h] * jnp.exp(g_last[h][c]) + _dot_tn(k_dec[h][rs], v_new[h]) for h in heads]
        for h in heads:
            gated = gw * _silu(z_ref[rs, sls[h]].astype(F32))
            o_h = o[h] * lax.rsqrt(jnp.mean(o[h] * o[h], axis=-1, keepdims=True) + EPS) * gated
            o_ref[rs, sls[h]] = o_h.astype(o_ref.dtype)
    for h in heads:
        state_ref[h] = state[h]


def _gdn(qkv, proj, bsz, seq, a_log, dt_bias, gnorm_w):
    t = bsz * seq
    n_chunks = min(4, seq // GDN_CHUNK)
    rows = n_chunks * GDN_CHUNK
    steps = seq // rows
    ab_off = SEG_U * SEG + AB_OFF
    abt = proj[:, ab_off:ab_off + 2 * SUBLANES].astype(F32).T

    def lane_vec(p):
        return jnp.zeros((1, LANES), F32).at[0, :GDN_HEADS].set(p.astype(F32))

    def sublane_vec(p):
        return jnp.zeros((2 * SUBLANES, 1), F32).at[:GDN_HEADS, 0].set(p.astype(F32))

    def rowblk(seg):
        return pl.BlockSpec((rows, SEG), lambda b, s: (b * steps + s, seg))

    return pl.pallas_call(
        functools.partial(_gdn_kernel, n_chunks),
        grid=(bsz, steps),
        in_specs=[rowblk(0), rowblk(1), rowblk(2),
                  pl.BlockSpec((rows, SEG), lambda b, s: (b * steps + s, SEG_GZ)),
                  pl.BlockSpec((rows, 2 * LANES), lambda b, s: (b * steps + s, ab_off // (2 * LANES))),
                  pl.BlockSpec((2 * SUBLANES, rows), lambda b, s: (0, b * steps + s)),
                  pl.BlockSpec((1, LANES), lambda b, s: (0, 0)),
                  pl.BlockSpec((1, LANES), lambda b, s: (0, 0)),
                  pl.BlockSpec((2 * SUBLANES, 1), lambda b, s: (0, 0)),
                  pl.BlockSpec((2 * SUBLANES, 1), lambda b, s: (0, 0)),
                  pl.BlockSpec((1, HEAD_DIM), lambda b, s: (0, 0))],
        out_specs=pl.BlockSpec((rows, GDN_WIDTH), lambda b, s: (b * steps + s, 0)),
        out_shape=jax.ShapeDtypeStruct((t, GDN_WIDTH), BF16),
        scratch_shapes=[pltpu.VMEM((GDN_HEADS, HEAD_DIM, HEAD_DIM), F32)],
        compiler_params=_cparams("parallel", "arbitrary"), name="gdn",
    )(qkv, qkv, qkv, proj, proj, abt, lane_vec(a_log), lane_vec(dt_bias),
      sublane_vec(a_log), sublane_vec(dt_bias), gnorm_w.reshape(1, HEAD_DIM).astype(F32))


def _outproj_kernel(with_router, a1_ref, a2_ref, a3_ref, w1_ref, w2_ref, w3_ref, h_ref, gate_ref,
                    nw_ref, scale_ref, shift_ref, *rest):
    if with_router:
        wr_hi_ref, wr_lo_ref, hres_ref, hn_ref, route_ref = rest
    else:
        hres_ref, hn_ref = rest
    mix = _dot(a1_ref[...], w1_ref[...]) + _dot(a2_ref[...], w2_ref[...]) + _dot(a3_ref[...], w3_ref[...])
    h = h_ref[...] + gate_ref[...] * mix
    hres_ref[...] = h
    hn = _rms_mod(h, nw_ref[...], scale_ref[...], shift_ref[...])
    hn_ref[...] = hn.astype(hn_ref.dtype)
    if with_router:
        hn_hi = hn.astype(BF16)
        hn_lo = (hn - hn_hi.astype(F32)).astype(BF16)
        logits = (_dot(hn_hi, wr_hi_ref[...]) + _dot(hn_lo, wr_hi_ref[...])) + _dot(hn_hi, wr_lo_ref[...])
        lane = lax.broadcasted_iota(jnp.int32, logits.shape, 1)
        lg = jnp.where(lane < N_EXPERTS, logits, -jnp.inf)
        m1 = jnp.max(lg, axis=-1, keepdims=True)
        i1 = jnp.min(jnp.where(lg == m1, lane, LANES), axis=-1, keepdims=True)
        lg2 = jnp.where(lane == i1, -jnp.inf, lg)
        m2 = jnp.max(lg2, axis=-1, keepdims=True)
        i2 = jnp.min(jnp.where(lg2 == m2, lane, LANES), axis=-1, keepdims=True)
        e = jnp.exp(m2 - m1)
        g1 = 1.0 / (1.0 + e)
        g2 = e / (1.0 + e)
        route_ref[...] = jnp.where(lane == 0, i1.astype(F32),
                                   jnp.where(lane == 1, i2.astype(F32),
                                             jnp.where(lane == 2, g1, jnp.where(lane == 3, g2, 0.0))))


def _outproj(a1, a2, a3, w_out, h_res, seq, gate, norm_w, scale, shift, w_router):
    t, d = h_res.shape
    tm = min(256, t)
    with_router = w_router is not None
    k1, k2 = a1.shape[1], a2.shape[1]
    w1, w2, w3 = w_out[:k1], w_out[k1:k1 + k2], w_out[k1 + k2:]

    def rows(w):
        return pl.BlockSpec((tm, w), lambda i: (i, 0))

    def whole(a):
        return pl.BlockSpec(a.shape, lambda i: (0, 0))

    per_batch = pl.BlockSpec((None, 1, d), lambda i: ((i * tm) // seq, 0, 0))
    nw = norm_w.reshape(1, d).astype(F32)
    args = [a1, a2, a3, w1, w2, w3, h_res, gate, nw, scale, shift]
    specs = [rows(k1), rows(k2), rows(a3.shape[1]), whole(w1), whole(w2), whole(w3), rows(d), per_batch,
             whole(nw), per_batch, per_batch]
    out_shape = [jax.ShapeDtypeStruct((t, d), F32), jax.ShapeDtypeStruct((t, d), BF16)]
    out_specs = [rows(d), rows(d)]
    if with_router:
        wr = jnp.zeros((d, LANES), F32).at[:, :N_EXPERTS].set(w_router.astype(F32))
        wr_hi = wr.astype(BF16)
        wr_lo = (wr - wr_hi.astype(F32)).astype(BF16)
        args += [wr_hi, wr_lo]
        specs += [whole(wr_hi), whole(wr_lo)]
        out_shape.append(jax.ShapeDtypeStruct((t, LANES), F32))
        out_specs.append(rows(LANES))
    return pl.pallas_call(
        functools.partial(_outproj_kernel, with_router),
        grid=(t // tm,), in_specs=specs, out_specs=out_specs, out_shape=out_shape,
        compiler_params=_cparams("parallel"), name="out_proj",
    )(*args)


FFN_SUB = 256


def _ffn_kernel(row_gated, te_ref, nv_ref, x_ref, w1_ref, w3_ref, w2_ref, *rest):
    if row_gated:
        rg_ref, o_ref, acc_ref = rest
    else:
        o_ref, acc_ref = rest
    i = pl.program_id(0)
    f = pl.program_id(1)

    @pl.when(f == 0)
    def _():
        acc_ref[...] = jnp.zeros_like(acc_ref)

    @pl.when(i < nv_ref[0])
    def _():
        x = x_ref[...]
        n_sub = w1_ref.shape[1] // FFN_SUB

        def up(s):
            cols = slice(s * FFN_SUB, (s + 1) * FFN_SUB)
            return _dot(x, w1_ref[:, cols]), _dot(x, w3_ref[:, cols])

        def down(s, h):
            return _dot((_silu(h[0]) * h[1]).astype(BF16), w2_ref[s * FFN_SUB:(s + 1) * FFN_SUB, :])

        h = up(0)
        total = None
        for s in range(1, n_sub + 1):
            h_next = up(s) if s < n_sub else None
            part = down(s - 1, h)
            total = part if total is None else total + part
            h = h_next
        acc_ref[...] += total

    @pl.when(f == pl.num_programs(1) - 1)
    def _():
        y = acc_ref[...]
        if row_gated:
            y = y * rg_ref[...]
        o_ref[...] = y.astype(o_ref.dtype)


def _ffn(x, w1, w3, w2, tile_expert, n_valid, row_gate, tm, tf):
    r, d = x.shape
    ff = w1.shape[2]
    tm, tf = min(tm, r), min(tf, ff)
    in_specs = [pl.BlockSpec((tm, d), lambda i, f, te, nv: (i, 0)),
                pl.BlockSpec((None, d, tf), lambda i, f, te, nv: (te[i], 0, f)),
                pl.BlockSpec((None, d, tf), lambda i, f, te, nv: (te[i], 0, f)),
                pl.BlockSpec((None, tf, d), lambda i, f, te, nv: (te[i], f, 0))]
    args = [tile_expert, n_valid, x, w1, w3, w2]
    if row_gate is not None:
        in_specs.append(pl.BlockSpec((tm, 1), lambda i, f, te, nv: (i, 0)))
        args.append(row_gate)
    grid_spec = pltpu.PrefetchScalarGridSpec(
        num_scalar_prefetch=2, grid=(r // tm, ff // tf), in_specs=in_specs,
        out_specs=pl.BlockSpec((tm, d), lambda i, f, te, nv: (i, 0)),
        scratch_shapes=[pltpu.VMEM((tm, d), F32)])
    return pl.pallas_call(
        functools.partial(_ffn_kernel, row_gate is not None), grid_spec=grid_spec,
        out_shape=jax.ShapeDtypeStruct((r, d), BF16),
        compiler_params=_cparams("parallel", "arbitrary"), name="ffn",
    )(*args)


MOE_TILE = 512


def _moe_plan(route, tm):
    t = route.shape[0]
    e_flat = route[:, :2].astype(jnp.int32).reshape(-1)
    onehot = (e_flat[:, None] == jnp.arange(N_EXPERTS, dtype=jnp.int32)[None, :]).astype(jnp.int32)
    counts = jnp.sum(onehot, axis=0)
    rank = jnp.sum((jnp.cumsum(onehot, axis=0) - onehot) * onehot, axis=1)
    padded = ((counts + tm - 1) // tm) * tm
    ends = jnp.cumsum(padded)
    pos = (ends - padded)[e_flat] + rank
    n_rows = 2 * t + N_EXPERTS * tm
    n_tiles, n_blocks = n_rows // tm, t // tm
    src_tok = jnp.full((n_rows,), -1, jnp.int32).at[pos].set(jnp.arange(2 * t, dtype=jnp.int32) // 2)
    row_gate = jnp.zeros((n_rows,), F32).at[pos].set(route[:, 2:4].reshape(-1))
    n_valid = (ends[-1] // tm).astype(jnp.int32)
    tile_start = jnp.arange(n_tiles, dtype=jnp.int32) * tm
    tile_expert = jnp.sum((tile_start[:, None] >= ends[None, :]).astype(jnp.int32), axis=1)
    last_expert = jnp.sum((tile_start[n_valid - 1] >= ends).astype(jnp.int32))
    tile_expert = jnp.where(tile_start < ends[-1], tile_expert, last_expert).astype(jnp.int32)

    blk = jnp.where(src_tok >= 0, src_tok // tm, -1).reshape(n_tiles, tm, 1)
    incidence = jnp.any(blk == jnp.arange(n_blocks, dtype=jnp.int32)[None, None, :], axis=1)
    w_max = n_tiles + N_EXPERTS * n_blocks

    def work_list(m):
        flat = m.reshape(-1)
        n = jnp.sum(flat.astype(jnp.int32))
        idx = jnp.nonzero(flat, size=w_max, fill_value=0)[0].astype(jnp.int32)
        w = jnp.arange(w_max, dtype=jnp.int32)
        valid = w < n
        idx = jnp.where(valid, idx, idx[n - 1])
        major, minor = idx // m.shape[1], idx % m.shape[1]
        first = valid & ((w == 0) | (major != jnp.roll(major, 1)))
        last = valid & ((w == n - 1) | (major != jnp.roll(major, -1)))
        return [a.astype(jnp.int32) for a in (major, minor, first, last, valid)]

    return (src_tok, row_gate.reshape(n_rows, 1), tile_expert, n_valid.reshape(1),
            work_list(incidence), work_list(incidence.T))


def _dispatch_kernel(wi_ref, wj_ref, first_ref, last_ref, valid_ref, x_ref, tok_ref, o_ref):
    w = pl.program_id(0)
    tm = x_ref.shape[0]

    @pl.when(valid_ref[w] == 1)
    def _():
        col_tok = lax.broadcasted_iota(jnp.int32, (tm, tm), 1) + wj_ref[w] * tm
        onehot = jnp.where(tok_ref[...] == col_tok, 1.0, 0.0).astype(BF16)
        picked = _dot(onehot, x_ref[...]).astype(o_ref.dtype)

        @pl.when(first_ref[w] == 1)
        def _():
            o_ref[...] = picked

        @pl.when(first_ref[w] == 0)
        def _():
            o_ref[...] += picked


def _dispatch(x, src_tok, work, tm):
    t, d = x.shape
    n_rows = src_tok.shape[0]
    grid_spec = pltpu.PrefetchScalarGridSpec(
        num_scalar_prefetch=5, grid=(work[0].shape[0],),
        in_specs=[pl.BlockSpec((tm, d), lambda w, wi, wj, *_: (wj[w], 0)),
                  pl.BlockSpec((tm, 1), lambda w, wi, wj, *_: (wi[w], 0))],
        out_specs=pl.BlockSpec((tm, d), lambda w, wi, wj, *_: (wi[w], 0)))
    return pl.pallas_call(
        _dispatch_kernel, grid_spec=grid_spec,
        out_shape=jax.ShapeDtypeStruct((n_rows, d), x.dtype),
        compiler_params=_cparams("arbitrary"), name="moe_dispatch",
    )(*work, x, src_tok.reshape(n_rows, 1))


def _combine_kernel(has_mod, emit_res, vj_ref, vi_ref, first_ref, last_ref, valid_ref, y_ref, tok_ref, h_ref,
                    gate_ref, nw_ref, *rest):
    rest = list(rest)
    scale_ref = shift_ref = None
    if has_mod:
        scale_ref, shift_ref = rest.pop(0), rest.pop(0)
    hres_ref = rest.pop(0) if emit_res else None
    o_ref, acc_ref = rest
    w = pl.program_id(0)
    tm = y_ref.shape[0]

    @pl.when(valid_ref[w] == 1)
    def _():
        row_tok = lax.broadcasted_iota(jnp.int32, (tm, tm), 0) + vj_ref[w] * tm
        onehot = jnp.where(tok_ref[...] == row_tok, 1.0, 0.0).astype(BF16)
        part = _dot(onehot, y_ref[...])

        @pl.when(first_ref[w] == 1)
        def _():
            acc_ref[...] = part

        @pl.when(first_ref[w] == 0)
        def _():
            acc_ref[...] += part

        @pl.when(last_ref[w] == 1)
        def _():
            h = h_ref[...] + gate_ref[...] * acc_ref[...]
            if emit_res:
                hres_ref[...] = h
            scale = scale_ref[...] if has_mod else None
            shift = shift_ref[...] if has_mod else None
            o_ref[...] = _rms_mod(h, nw_ref[...], scale, shift).astype(o_ref.dtype)


def _combine_resnorm(ys, src_tok, work, tm, h_res, seq, gate, norm_w, scale, shift, emit_res, out_dtype):
    t, d = h_res.shape
    n_tiles = ys.shape[0] // tm
    has_mod = scale is not None

    def tok_rows(w, vj, vi, *_):
        return (vj[w], 0)

    per_batch = pl.BlockSpec((None, 1, d), lambda w, vj, *_: ((vj[w] * tm) // seq, 0, 0))
    in_specs = [pl.BlockSpec((tm, d), lambda w, vj, vi, *_: (vi[w], 0)),
                pl.BlockSpec((None, 1, tm), lambda w, vj, vi, *_: (vi[w], 0, 0)),
                pl.BlockSpec((tm, d), tok_rows), per_batch,
                pl.BlockSpec((1, d), lambda w, *_: (0, 0))]
    args = list(work) + [ys, src_tok.reshape(n_tiles, 1, tm), h_res, gate, norm_w.reshape(1, d)]
    if has_mod:
        in_specs += [per_batch, per_batch]
        args += [scale, shift]
    out_shape, out_specs = [], []
    if emit_res:
        out_shape.append(jax.ShapeDtypeStruct((t, d), F32))
        out_specs.append(pl.BlockSpec((tm, d), tok_rows))
    out_shape.append(jax.ShapeDtypeStruct((t, d), out_dtype))
    out_specs.append(pl.BlockSpec((tm, d), tok_rows))
    grid_spec = pltpu.PrefetchScalarGridSpec(
        num_scalar_prefetch=5, grid=(work[0].shape[0],), in_specs=in_specs, out_specs=out_specs,
        scratch_shapes=[pltpu.VMEM((tm, d), F32)])
    res = pl.pallas_call(
        functools.partial(_combine_kernel, has_mod, emit_res), grid_spec=grid_spec, out_shape=out_shape,
        compiler_params=_cparams("arbitrary"), name="moe_combine",
    )(*args)
    return res if emit_res else res[0]


def kernel(x, c, positions, w_ada, b_ada, norm_mix, norm_ffn, norm_final, w_in, w_out, ssm_a_re, ssm_a_im, ssm_log_dt, ssm_b_re, ssm_b_im, ssm_c_re, ssm_c_im, ssm_d, ssm_w_glu, diff_lam_q1, diff_lam_k1, diff_lam_q2, diff_lam_k2, diff_subln, gdn_conv, gdn_a_log, gdn_dt_bias, gdn_norm, ffn_w1, ffn_w3, ffn_w2, moe_router, moe_w1, moe_w3, moe_w2):
    bsz, seq, d = x.shape
    t = bsz * seq
    depth = w_in.shape[0]
    h_res = x.astype(F32).reshape(t, d)

    c_pad = jnp.zeros((SUBLANES, d), F32).at[:bsz].set(c.astype(F32))
    mod = _ada(c_pad, w_ada, b_ada)[:, :bsz]
    mods = [[m.reshape(bsz, 1, d) for m in jnp.split(mod[l], 6, axis=-1)] for l in range(depth)]
    rope_tables = _rope_tables(positions)

    hn = _resnorm(h_res, seq, [], None, norm_mix[0].astype(F32), mods[0][1], mods[0][0], False, BF16)
    out = None
    for l in range(depth):
        shift1, scale1, gate1, shift2, scale2, gate2 = mods[l]
        is_moe = l % 2 == 1
        proj = _in_proj(hn, w_in[l])

        tables = _s5_tables(ssm_a_re[l], ssm_a_im[l], ssm_log_dt[l], ssm_b_re[l], ssm_b_im[l],
                            ssm_c_re[l], ssm_c_im[l])
        y_ssm = _s5_mixer(proj, bsz, seq, tables, ssm_d[l], ssm_w_glu[l].astype(BF16))

        lambda_init = 0.8 - 0.6 * math.exp(-0.3 * l)
        qk = _rope(proj, rope_tables)
        y_diff = _diff_attention(qk, proj, bsz, seq,
                                 (diff_lam_q1[l], diff_lam_k1[l], diff_lam_q2[l], diff_lam_k2[l]),
                                 diff_subln[l], lambda_init)

        qkv = _gdn_conv(proj, gdn_conv[l], seq)
        y_gdn = _gdn(qkv, proj, bsz, seq, gdn_a_log[l], gdn_dt_bias[l], gdn_norm[l])

        res = _outproj(y_ssm, y_diff, y_gdn, w_out[l].astype(BF16), h_res, seq, gate1,
                       norm_ffn[l], scale2, shift2, moe_router[l // 2] if is_moe else None)
        h_res, hn2 = res[0], res[1]

        last = l + 1 == depth
        if last:
            nxt = (norm_final.astype(F32), None, None, False, x.dtype)
        else:
            nxt = (norm_mix[l + 1].astype(F32), mods[l + 1][1], mods[l + 1][0], True, BF16)
        if is_moe:
            src_tok, row_gate, tile_expert, n_valid, work_sorted, work_token = _moe_plan(res[2], MOE_TILE)
            xs = _dispatch(hn2, src_tok, work_sorted, MOE_TILE)
            ys = _ffn(xs, moe_w1[l // 2].astype(BF16), moe_w3[l // 2].astype(BF16),
                      moe_w2[l // 2].astype(BF16), tile_expert, n_valid, row_gate, MOE_TILE, 1024)
            res = _combine_resnorm(ys, src_tok, work_token, MOE_TILE, h_res, seq, gate2, *nxt)
        else:
            n_tiles = t // min(512, t)
            ys = _ffn(hn2, ffn_w1[l // 2:l // 2 + 1].astype(BF16), ffn_w3[l // 2:l // 2 + 1].astype(BF16),
                      ffn_w2[l // 2:l // 2 + 1].astype(BF16), jnp.zeros((n_tiles,), jnp.int32),
                      jnp.full((1,), n_tiles, jnp.int32), None, 512, 512)
            res = _resnorm(h_res, seq, [(ys, None)], gate2, *nxt)
        if last:
            out = res
        else:
            h_res, hn = res
    return out.reshape(bsz, seq, d)
```

```python
import functools
import math

import numpy as np
import jax
import jax.numpy as jnp
from jax import lax
from jax.experimental import pallas as pl
from jax.experimental.pallas import tpu as pltpu

F32 = jnp.float32
BF16 = jnp.bfloat16
HIGHEST = lax.Precision.HIGHEST

D_MODEL = 2048
SSM_WIDTH = 512
SSM_CH = 16
SSM_GROUPS = SSM_WIDTH // SSM_CH
SSM_STATE = 64
SSM_CHUNK = 16
DIFF_WIDTH = 768
HEAD_DIM = 128
DIFF_HEADS = DIFF_WIDTH // HEAD_DIM
DIFF_QK_DIM = HEAD_DIM // 2
GDN_WIDTH = 768
GDN_HEADS = GDN_WIDTH // HEAD_DIM
CONV_WIDTH = 4
GDN_CHUNK = 64
ROPE_THETA = 500000.0
ROPE_DIM = DIFF_QK_DIM // 4
ROPE_HALF = ROPE_DIM // 2
N_EXPERTS = 8
EPS = 1e-6
LANES = 128
SUBLANES = 8

SEG = 768
SEG_GQ, SEG_GK, SEG_GV, SEG_GZ, SEG_DQ, SEG_DK, SEG_DV, SEG_U = range(8)
PROJ_WIDTH = 8 * SEG
AB_OFF = SSM_WIDTH

VMEM_LIMIT = 56 * 1024 * 1024


def _cparams(*sem):
    return pltpu.CompilerParams(dimension_semantics=sem, vmem_limit_bytes=VMEM_LIMIT)


def _dot(a, b):
    return jnp.dot(a, b, preferred_element_type=F32)


def _dot_nt(a, b):
    return lax.dot_general(a, b, (((1,), (1,)), ((), ())), preferred_element_type=F32)


def _dot_tn(a, b):
    return lax.dot_general(a, b, (((0,), (0,)), ((), ())), preferred_element_type=F32)


def _sigmoid(x):
    return 1.0 / (1.0 + jnp.exp(-x))


def _silu(x):
    return x * _sigmoid(x)


def _ada_kernel(c_ref, w_ref, b_ref, o_ref):
    cond = _silu(c_ref[...])
    o_ref[...] = _dot(cond.astype(BF16), w_ref[...].astype(BF16)) + b_ref[...]


def _ada(c_pad, w_ada, b_ada):
    depth, d, n = w_ada.shape
    tn = 1024
    return pl.pallas_call(
        _ada_kernel,
        grid=(depth, n // tn),
        in_specs=[pl.BlockSpec((SUBLANES, d), lambda l, j: (0, 0)),
                  pl.BlockSpec((None, d, tn), lambda l, j: (l, 0, j)),
                  pl.BlockSpec((None, 1, tn), lambda l, j: (l, 0, j))],
        out_specs=pl.BlockSpec((None, SUBLANES, tn), lambda l, j: (l, 0, j)),
        out_shape=jax.ShapeDtypeStruct((depth, SUBLANES, n), F32),
        compiler_params=_cparams("arbitrary", "arbitrary"),
        name="ada",
    )(c_pad, w_ada, b_ada.reshape(depth, 1, n))


def _rms_mod(h, w, scale, shift):
    y = h * lax.rsqrt(jnp.mean(h * h, axis=-1, keepdims=True) + EPS) * w
    if scale is not None:
        y = y * (1.0 + scale) + shift
    return y


def _resnorm_kernel(has_delta, has_mod, emit_res, *refs):
    refs = list(refs)
    h = refs.pop(0)[...]
    if has_delta:
        y = refs.pop(0)[...].astype(F32)
        h = h + refs.pop(0)[...] * y
    w = refs.pop(0)[...]
    scale = shift = None
    if has_mod:
        scale = refs.pop(0)[...]
        shift = refs.pop(0)[...]
    if emit_res:
        refs.pop(0)[...] = h
    o_ref = refs.pop(0)
    o_ref[...] = _rms_mod(h, w, scale, shift).astype(o_ref.dtype)


def _resnorm(h_res, seq, delta, gate, norm_w, scale, shift, emit_res, out_dtype):
    t, d = h_res.shape
    tm = min(256, t)
    row = pl.BlockSpec((tm, d), lambda i: (i, 0))
    per_batch = pl.BlockSpec((None, 1, d), lambda i: ((i * tm) // seq, 0, 0))
    args, specs = [h_res], [row]
    if delta is not None:
        args += [delta, gate]
        specs += [row, per_batch]
    args.append(norm_w.reshape(1, d))
    specs.append(pl.BlockSpec((1, d), lambda i: (0, 0)))
    if scale is not None:
        args += [scale, shift]
        specs += [per_batch, per_batch]
    out_shape, out_specs = [], []
    if emit_res:
        out_shape.append(jax.ShapeDtypeStruct((t, d), F32))
        out_specs.append(row)
    out_shape.append(jax.ShapeDtypeStruct((t, d), out_dtype))
    out_specs.append(row)
    res = pl.pallas_call(
        functools.partial(_resnorm_kernel, delta is not None, scale is not None, emit_res),
        grid=(t // tm,), in_specs=specs, out_specs=out_specs, out_shape=out_shape,
        compiler_params=_cparams("parallel"), name="resnorm",
    )(*args)
    return res if emit_res else res[0]


IN_BLOCK = 2 * LANES


def _in_proj_blocks():
    src_segments = [(SEG_U, SSM_WIDTH), (SEG_DQ, DIFF_WIDTH), (SEG_DK, DIFF_WIDTH), (SEG_DV, DIFF_WIDTH),
                    (SEG_GQ, 3 * GDN_WIDTH), (SEG_GZ, GDN_WIDTH)]
    dest = []
    for seg, width in src_segments:
        dest += [seg * SEG // IN_BLOCK + b for b in range(width // IN_BLOCK)]
    dest.append((SEG_U * SEG + AB_OFF) // IN_BLOCK)
    return np.asarray(dest, np.int32)


def _in_proj_kernel(n_cols, dest_ref, x_ref, w_ref, o_ref, wb_ref):
    j = pl.program_id(1)

    @pl.when(pl.program_id(0) == 0)
    def _():
        col = lax.broadcasted_iota(jnp.int32, w_ref.shape, 1) + j * IN_BLOCK
        wb_ref[j] = jnp.where(col < n_cols, w_ref[...], 0.0).astype(BF16)

    o_ref[...] = _dot(x_ref[...], wb_ref[j]).astype(o_ref.dtype)


def _in_proj(x, w_in, layer):
    m, k = x.shape
    n_cols = w_in.shape[2]
    dest = _in_proj_blocks()
    n_blocks = dest.shape[0]
    tm = min(1024, m)
    grid_spec = pltpu.PrefetchScalarGridSpec(
        num_scalar_prefetch=1, grid=(m // tm, n_blocks),
        in_specs=[pl.BlockSpec((tm, k), lambda i, j, dest: (i, 0)),
                  pl.BlockSpec((None, k, IN_BLOCK),
                               lambda i, j, dest: (layer, 0, jnp.where(i == 0, j, n_blocks - 1)))],
        out_specs=pl.BlockSpec((tm, IN_BLOCK), lambda i, j, dest: (i, dest[j])),
        scratch_shapes=[pltpu.VMEM((n_blocks, k, IN_BLOCK), BF16)])
    return pl.pallas_call(
        functools.partial(_in_proj_kernel, n_cols), grid_spec=grid_spec,
        out_shape=jax.ShapeDtypeStruct((m, PROJ_WIDTH), BF16),
        compiler_params=_cparams("arbitrary", "arbitrary"), name="in_proj",
    )(jnp.asarray(dest), x, w_in)


def _s5_tables(a_re, a_im, log_dt, b_re, b_im, c_re, c_im, n_steps):
    L, G, P, H = SSM_CHUNK, SSM_GROUPS, SSM_STATE, SSM_CH
    lam = lax.complex(a_re.astype(F32), a_im.astype(F32))
    log_lam_bar = lam * jnp.exp(log_dt.astype(F32))[:, None]
    lam_bar = jnp.exp(log_lam_bar)
    b_bar = ((lam_bar - 1.0) / lam)[:, :, None] * lax.complex(b_re.astype(F32), b_im.astype(F32))
    c_mat = lax.complex(c_re.astype(F32), c_im.astype(F32))
    steps = jnp.arange(L + 1, dtype=F32)
    pw = jnp.exp(log_lam_bar[:, None, :] * steps[None, :, None])
    kern = jnp.real(jnp.einsum('ghp,gjp,gpi->gjih', c_mat, pw[:, :L], b_bar))
    lag = jnp.arange(L)[None, :] - jnp.arange(L)[:, None]
    tm = jnp.where((lag >= 0)[None, :, :, None, None], kern[:, jnp.clip(lag, 0, L - 1)], 0.0)
    tm = tm.transpose(0, 2, 4, 1, 3).reshape(G, L * H, L * H)
    zc = pw[:, L - 1 - jnp.arange(L)][:, :, :, None] * b_bar[:, None]
    zc = zc.transpose(0, 2, 1, 3).reshape(G, P, L * H)
    zm = jnp.concatenate([jnp.real(zc), jnp.imag(zc)], axis=1)
    cl = (c_mat[:, None] * pw[:, 1:L + 1][:, :, None, :]).reshape(G, L * H, P)
    ym = jnp.concatenate([jnp.real(cl), -jnp.imag(cl)], axis=2)
    a_pows = jnp.exp(log_lam_bar[:, None, :] * (L * 2.0 ** jnp.arange(n_steps, dtype=F32))[None, :, None])
    return (tm.astype(BF16), zm.astype(BF16), ym.astype(BF16),
            jnp.real(a_pows)[..., None], jnp.imag(a_pows)[..., None])


def _s5_core_kernel(n_steps, chunks_per_seq, u_ref, tm_ref, zm_ref, ym_ref, are_ref, aim_ref, o_ref):
    p = SSM_STATE
    u = u_ref[...]
    z = _dot(zm_ref[...], u)
    x_re, x_im = z[:p], z[p:]
    c_idx = lax.broadcasted_iota(jnp.int32, x_re.shape, 1) % chunks_per_seq
    for k in range(n_steps):
        d = 1 << k
        a_re, a_im = are_ref[k], aim_ref[k]
        inside = c_idx >= d
        s_re = jnp.where(inside, pltpu.roll(x_re, d, axis=1), 0.0)
        s_im = jnp.where(inside, pltpu.roll(x_im, d, axis=1), 0.0)
        x_re, x_im = x_re + (a_re * s_re - a_im * s_im), x_im + (a_re * s_im + a_im * s_re)
    inside = c_idx >= 1
    x_prev = jnp.concatenate([jnp.where(inside, pltpu.roll(x_re, 1, axis=1), 0.0),
                              jnp.where(inside, pltpu.roll(x_im, 1, axis=1), 0.0)], axis=0)
    o_ref[...] = (_dot(tm_ref[...], u) + _dot(ym_ref[...], x_prev.astype(BF16))).astype(o_ref.dtype)


def _s5_core(ug, tables, chunks_per_seq):
    g, w, r = ug.shape
    tm, zm, ym, are, aim = tables
    p = SSM_STATE
    n_steps = are.shape[1]

    def grp(*shape):
        return pl.BlockSpec((None,) + shape, lambda i: (i,) + (0,) * len(shape))

    return pl.pallas_call(
        functools.partial(_s5_core_kernel, n_steps, chunks_per_seq),
        grid=(g,),
        in_specs=[grp(w, r), grp(w, w), grp(2 * p, w), grp(w, 2 * p), grp(n_steps, p, 1), grp(n_steps, p, 1)],
        out_specs=grp(w, r),
        out_shape=jax.ShapeDtypeStruct((g, w, r), BF16),
        compiler_params=_cparams("parallel"), name="s5_core",
    )(ug, tm, zm, ym, are, aim)


def _gelu_tanh(x):
    return 0.5 * x * (1.0 + jnp.tanh(math.sqrt(2.0 / math.pi) * (x + 0.044715 * (x * x * x))))


def _s5_post_kernel(y_ref, u_ref, d_ref, w_ref, o_ref):
    u = u_ref[:, :SSM_WIDTH].astype(F32)
    y = _gelu_tanh(y_ref[...].astype(F32) + d_ref[...] * u)
    o_ref[...] = (y * _sigmoid(_dot(y.astype(BF16), w_ref[...]))).astype(o_ref.dtype)


def _s5_post(y_core, proj, d_skip, w_glu):
    t = y_core.shape[0]
    tm = min(512, t)
    return pl.pallas_call(
        _s5_post_kernel,
        grid=(t // tm,),
        in_specs=[pl.BlockSpec((tm, SSM_WIDTH), lambda i: (i, 0)),
                  pl.BlockSpec((tm, SEG), lambda i: (i, SEG_U)),
                  pl.BlockSpec((1, SSM_WIDTH), lambda i: (0, 0)),
                  pl.BlockSpec((SSM_WIDTH, SSM_WIDTH), lambda i: (0, 0))],
        out_specs=pl.BlockSpec((tm, SSM_WIDTH), lambda i: (i, 0)),
        out_shape=jax.ShapeDtypeStruct((t, SSM_WIDTH), BF16),
        compiler_params=_cparams("parallel"), name="s5_post",
    )(y_core, proj, d_skip.reshape(1, SSM_WIDTH).astype(F32), w_glu)


def _s5_mixer(proj, bsz, seq, tables, d_skip, w_glu):
    L, G, H = SSM_CHUNK, SSM_GROUPS, SSM_CH
    t = bsz * seq
    nc = seq // L
    u = proj[:, SEG_U * SEG:SEG_U * SEG + SSM_WIDTH]
    ug = u.reshape(t // L, L, G, H).transpose(2, 1, 3, 0).reshape(G, L * H, t // L)
    yg = _s5_core(ug, tables, nc)
    y_core = yg.reshape(G, L, H, t // L).transpose(3, 1, 0, 2).reshape(t, SSM_WIDTH)
    return _s5_post(y_core, proj, d_skip, w_glu)


def _rope_tables(positions):
    inv_freq = ROPE_THETA ** (-jnp.arange(0, ROPE_DIM, 2, dtype=F32) / ROPE_DIM)
    ang = positions.astype(F32).reshape(-1)[:, None] * inv_freq
    cos, sin = jnp.cos(ang), jnp.sin(ang)
    r = np.arange(LANES) % DIFF_QK_DIM
    first = jnp.asarray(r < ROPE_HALF)[None, :]
    second = jnp.asarray((r >= ROPE_HALF) & (r < ROPE_DIM))[None, :]
    idx = jnp.asarray(r % ROPE_HALF)
    cos_l, sin_l = cos[:, idx], sin[:, idx]
    cosf = jnp.where(first | second, cos_l, 1.0)
    sin_a = jnp.where(first, -sin_l, 0.0)
    sin_b = jnp.where(second, sin_l, 0.0)
    return cosf, sin_a, sin_b


def _rope_kernel(x_ref, c_ref, sa_ref, sb_ref, o_ref):
    cosf, sin_a, sin_b = c_ref[...], sa_ref[...], sb_ref[...]
    n_slabs = x_ref.shape[1] // LANES
    for s in range(n_slabs):
        x = x_ref[:, s * LANES:(s + 1) * LANES].astype(F32)
        y = (x * cosf + pltpu.roll(x, LANES - ROPE_HALF, axis=1) * sin_a
             + pltpu.roll(x, ROPE_HALF, axis=1) * sin_b)
        if s < n_slabs // 2:
            y = y * (DIFF_QK_DIM ** -0.5 * math.log2(math.e))
        o_ref[:, s * LANES:(s + 1) * LANES] = y.astype(o_ref.dtype)


def _rope(proj, tables):
    t = proj.shape[0]
    tm = min(512, t)
    w = 2 * DIFF_WIDTH
    tab = pl.BlockSpec((tm, LANES), lambda i: (i, 0))
    return pl.pallas_call(
        _rope_kernel,
        grid=(t // tm,),
        in_specs=[pl.BlockSpec((tm, w), lambda i: (i, SEG_DQ // 2)), tab, tab, tab],
        out_specs=pl.BlockSpec((tm, w), lambda i: (i, 0)),
        out_shape=jax.ShapeDtypeStruct((t, w), BF16),
        compiler_params=_cparams("parallel"), name="rope",
    )(proj, *tables)


def _attn_kernel(tq, lambda_init, q_ref, k_ref, v_ref, lq1_ref, lk1_ref, lq2_ref, lk2_ref, sw_ref, o_ref):
    qi = pl.program_id(2)
    q = q_ref[...]
    lane = lax.broadcasted_iota(jnp.int32, q.shape, 1)
    zero = jnp.zeros_like(q)
    qm = (jnp.where(lane < DIFF_QK_DIM, q, zero), jnp.where(lane >= DIFF_QK_DIM, q, zero))
    def block(kv, masked, carry):
        start = pl.multiple_of(kv * tq, tq)
        k = k_ref[pl.ds(start, tq), :]
        v = v_ref[pl.ds(start, tq), :]
        new = []
        for i in range(2):
            m_old, l_old, acc = carry[i]
            s = _dot_nt(qm[i], k)
            if masked:
                row = lax.broadcasted_iota(jnp.int32, s.shape, 0)
                col = lax.broadcasted_iota(jnp.int32, s.shape, 1)
                s = jnp.where(col <= row, s, -jnp.inf)
            m_new = jnp.maximum(m_old, jnp.max(s, axis=-1, keepdims=True))
            alpha = jnp.exp2(m_old - m_new)
            p = jnp.exp2(s - m_new)
            l_new = alpha * l_old + jnp.sum(p, axis=-1, keepdims=True)
            new.append((m_new, l_new, alpha * acc + _dot(p.astype(BF16), v)))
        return tuple(new)

    init = tuple((jnp.full((tq, 1), -jnp.inf, F32), jnp.zeros((tq, 1), F32), jnp.zeros((tq, HEAD_DIM), F32))
                 for _ in range(2))
    carry = lax.fori_loop(0, qi, lambda kv, c: block(kv, False, c), init)
    (_, l1, acc1), (_, l2, acc2) = block(qi, True, carry)
    lam = (jnp.exp(jnp.sum(lq1_ref[...] * lk1_ref[...], axis=-1, keepdims=True))
           - jnp.exp(jnp.sum(lq2_ref[...] * lk2_ref[...], axis=-1, keepdims=True)) + lambda_init)
    o = acc1 / l1 - lam * (acc2 / l2)
    o = o * lax.rsqrt(jnp.mean(o * o, axis=-1, keepdims=True) + EPS) * sw_ref[...] * (1.0 - lambda_init)
    o_ref[...] = o.astype(o_ref.dtype)


def _diff_attention(qk, proj, bsz, seq, lam_params, subln_w, lambda_init):
    tq = min(512, seq)
    nh = DIFF_HEADS
    qk3 = qk.reshape(bsz, seq, 2 * DIFF_WIDTH)
    proj3 = proj.reshape(bsz, seq, PROJ_WIDTH)
    vec = pl.BlockSpec((1, DIFF_QK_DIM), lambda b, h, i: (0, 0))
    out = pl.pallas_call(
        functools.partial(_attn_kernel, tq, lambda_init),
        grid=(bsz, nh, seq // tq),
        in_specs=[pl.BlockSpec((None, tq, HEAD_DIM), lambda b, h, i: (b, i, h)),
                  pl.BlockSpec((None, seq, HEAD_DIM), lambda b, h, i: (b, 0, nh + h)),
                  pl.BlockSpec((None, seq, HEAD_DIM), lambda b, h, i: (b, 0, SEG_DV * (SEG // HEAD_DIM) + h)),
                  vec, vec, vec, vec,
                  pl.BlockSpec((1, HEAD_DIM), lambda b, h, i: (0, 0))],
        out_specs=pl.BlockSpec((None, tq, HEAD_DIM), lambda b, h, i: (b, i, h)),
        out_shape=jax.ShapeDtypeStruct((bsz, seq, DIFF_WIDTH), BF16),
        compiler_params=_cparams("parallel", "parallel", "arbitrary"), name="diff_attn",
    )(qk3, qk3, proj3, *[p.reshape(1, DIFF_QK_DIM).astype(F32) for p in lam_params],
      subln_w.reshape(1, HEAD_DIM).astype(F32))
    return out.reshape(bsz * seq, DIFF_WIDTH)


HALO_ROWS = 16


def _conv_kernel(tiles_per_seq, x_ref, prev_ref, w_ref, o_ref):
    i = pl.program_id(0)
    j = pl.program_id(1)
    x = x_ref[...].astype(F32)
    prev = prev_ref[...].astype(F32)[HALO_ROWS - SUBLANES:]
    prev = jnp.where(i % tiles_per_seq == 0, jnp.zeros_like(prev), prev)
    w = w_ref[...]
    head_rows = lax.broadcasted_iota(jnp.int32, (SUBLANES, x.shape[1]), 0)
    y = x * w[CONV_WIDTH - 1:CONV_WIDTH, :]
    y_head = y[:SUBLANES]
    for back in range(1, CONV_WIDTH):
        wk = w[CONV_WIDTH - 1 - back:CONV_WIDTH - back, :]
        y = y + pltpu.roll(x, back, axis=0) * wk
        mixed = jnp.where(head_rows < back, pltpu.roll(prev, back, axis=0), pltpu.roll(x[:SUBLANES], back, axis=0))
        y_head = y_head + mixed * wk
    is_qk = j < 2 * GDN_WIDTH // x.shape[1]
    q_scale = jnp.where(j < GDN_WIDTH // x.shape[1], HEAD_DIM ** -0.5, 1.0)

    def finish(v):
        v = _silu(v)
        outs = []
        for h in range(v.shape[1] // HEAD_DIM):
            vh = v[:, h * HEAD_DIM:(h + 1) * HEAD_DIM]
            nrm = lax.rsqrt(jnp.sum(vh * vh, axis=-1, keepdims=True) + EPS) * q_scale
            outs.append(vh * jnp.where(is_qk, nrm, 1.0))
        return jnp.concatenate(outs, axis=1)

    o_ref[...] = finish(jnp.concatenate([y_head, y[SUBLANES:]], axis=0)).astype(o_ref.dtype)


def _gdn_conv(proj, conv_w, seq):
    t = proj.shape[0]
    tm = min(512, seq)
    cw = GDN_WIDTH
    nj = 3 * GDN_WIDTH // cw
    halo = tm // HALO_ROWS
    return pl.pallas_call(
        functools.partial(_conv_kernel, seq // tm),
        grid=(t // tm, nj),
        in_specs=[pl.BlockSpec((tm, cw), lambda i, j: (i, j)),
                  pl.BlockSpec((HALO_ROWS, cw), lambda i, j: (jnp.maximum(i * halo - 1, 0), j)),
                  pl.BlockSpec((CONV_WIDTH, cw), lambda i, j: (0, j))],
        out_specs=pl.BlockSpec((tm, cw), lambda i, j: (i, j)),
        out_shape=jax.ShapeDtypeStruct((t, 3 * GDN_WIDTH), BF16),
        compiler_params=_cparams("parallel", "parallel"), name="gdn_conv",
    )(proj, proj, conv_w.astype(F32))


def _softplus(x):
    return jnp.maximum(x, 0.0) + jnp.log(1.0 + jnp.exp(-jnp.abs(x)))


def _gdn_kernel(n_chunks, q_ref, k_ref, v_ref, z_ref, ab_ref, abt_ref, alog_ref, dtb_ref, alog_t_ref, dtb_t_ref,
                gw_ref, o_ref, state_ref):
    C = GDN_CHUNK
    nh = GDN_HEADS
    rows = n_chunks * C

    @pl.when(pl.program_id(1) == 0)
    def _():
        state_ref[...] = jnp.zeros_like(state_ref)

    neg_a = -jnp.exp(alog_ref[...])
    g_col = neg_a * _softplus(ab_ref[:, :LANES].astype(F32) + dtb_ref[...])
    g_row = -jnp.exp(alog_t_ref[...]) * _softplus(abt_ref[...] + dtb_t_ref[...])
    r_i = lax.broadcasted_iota(jnp.int32, (rows, rows), 0)
    c_i = lax.broadcasted_iota(jnp.int32, (rows, rows), 1)
    same = (r_i // C) == (c_i // C)
    tri_l = jnp.where(same & (c_i <= r_i), 1.0, 0.0).astype(F32)
    tri_u = jnp.where(same & (r_i <= c_i), 1.0, 0.0).astype(F32)
    gc_all = jnp.dot(tri_l, g_col, precision=HIGHEST, preferred_element_type=F32)
    gr_all = jnp.dot(g_row, tri_u, precision=HIGHEST, preferred_element_type=F32)
    b_all = _sigmoid(ab_ref[:, :LANES].astype(F32))

    incl = same & (r_i >= c_i)
    strict = same & (r_i > c_i)
    eye = jnp.where(r_i == c_i, 1.0, 0.0).astype(F32)
    gw = gw_ref[...]

    heads = range(nh)
    sls = [slice(h * HEAD_DIM, (h + 1) * HEAD_DIM) for h in heads]
    q = [q_ref[:, sl] for sl in sls]
    k = [k_ref[:, sl] for sl in sls]
    kf = [x.astype(F32) for x in k]
    gcol = [gc_all[:, h:h + 1] for h in heads]
    beta = [b_all[:, nh + h:nh + h + 1] for h in heads]
    decay = [jnp.exp(jnp.where(incl, gcol[h] - gr_all[h:h + 1, :], -jnp.inf)) for h in heads]
    pw = [jnp.where(strict, -(beta[h] * _dot_nt(k[h], k[h]) * decay[h]), 0.0) for h in heads]
    t_mat = [eye + p for p in pw]
    pw = [p.astype(BF16) for p in pw]
    for _ in range(int(math.log2(C)) - 1):
        pw = [_dot(p, p).astype(BF16) for p in pw]
        t_mat = [t + _dot(t.astype(BF16), p) for t, p in zip(t_mat, pw)]
    eg = [jnp.exp(g) for g in gcol]
    uw = [_dot(t_mat[h].astype(BF16),
               jnp.concatenate([v_ref[:, sls[h]].astype(F32) * beta[h], kf[h] * (beta[h] * eg[h])],
                               axis=1).astype(BF16)) for h in heads]
    u_all = [x[:, :HEAD_DIM] for x in uw]
    w_all = [x[:, HEAD_DIM:].astype(BF16) for x in uw]
    qk_all = [jnp.where(incl, _dot_nt(q[h], k[h]) * decay[h], 0.0).astype(BF16) for h in heads]
    qe_all = [(q[h].astype(F32) * eg[h]).astype(BF16) for h in heads]
    g_last = [[g[(c + 1) * C - 1:(c + 1) * C, :] for c in range(n_chunks)] for g in gcol]
    k_dec = [(kf[h] * jnp.exp(jnp.concatenate([jnp.broadcast_to(g, (C, 1)) for g in g_last[h]], axis=0)
                              - gcol[h])).astype(BF16) for h in heads]

    state = [state_ref[h] for h in heads]
    for c in range(n_chunks):
        rs = slice(c * C, (c + 1) * C)
        sb = [s.astype(BF16) for s in state]
        v_new = [(u_all[h][rs] - _dot(w_all[h][rs], sb[h])).astype(BF16) for h in heads]
        o = [_dot(qe_all[h][rs], sb[h]) + _dot(qk_all[h][rs, rs], v_new[h]) for h in heads]
        state = [state[h] * jnp.exp(g_last[h][c]) + _dot_tn(k_dec[h][rs], v_new[h]) for h in heads]
        for h in heads:
            gated = gw * _silu(z_ref[rs, sls[h]].astype(F32))
            o_h = o[h] * lax.rsqrt(jnp.mean(o[h] * o[h], axis=-1, keepdims=True) + EPS) * gated
            o_ref[rs, sls[h]] = o_h.astype(o_ref.dtype)
    for h in heads:
        state_ref[h] = state[h]


def _gdn(qkv, proj, bsz, seq, a_log, dt_bias, gnorm_w):
    t = bsz * seq
    n_chunks = min(4, seq // GDN_CHUNK)
    rows = n_chunks * GDN_CHUNK
    steps = seq // rows
    ab_off = SEG_U * SEG + AB_OFF
    abt = proj[:, ab_off:ab_off + 2 * SUBLANES].astype(F32).T

    def lane_vec(p):
        return jnp.zeros((1, LANES), F32).at[0, :GDN_HEADS].set(p.astype(F32))

    def sublane_vec(p):
        return jnp.zeros((2 * SUBLANES, 1), F32).at[:GDN_HEADS, 0].set(p.astype(F32))

    def rowblk(seg):
        return pl.BlockSpec((rows, SEG), lambda b, s: (b * steps + s, seg))

    return pl.pallas_call(
        functools.partial(_gdn_kernel, n_chunks),
        grid=(bsz, steps),
        in_specs=[rowblk(0), rowblk(1), rowblk(2),
                  pl.BlockSpec((rows, SEG), lambda b, s: (b * steps + s, SEG_GZ)),
                  pl.BlockSpec((rows, 2 * LANES), lambda b, s: (b * steps + s, ab_off // (2 * LANES))),
                  pl.BlockSpec((2 * SUBLANES, rows), lambda b, s: (0, b * steps + s)),
                  pl.BlockSpec((1, LANES), lambda b, s: (0, 0)),
                  pl.BlockSpec((1, LANES), lambda b, s: (0, 0)),
                  pl.BlockSpec((2 * SUBLANES, 1), lambda b, s: (0, 0)),
                  pl.BlockSpec((2 * SUBLANES, 1), lambda b, s: (0, 0)),
                  pl.BlockSpec((1, HEAD_DIM), lambda b, s: (0, 0))],
        out_specs=pl.BlockSpec((rows, GDN_WIDTH), lambda b, s: (b * steps + s, 0)),
        out_shape=jax.ShapeDtypeStruct((t, GDN_WIDTH), BF16),
        scratch_shapes=[pltpu.VMEM((GDN_HEADS, HEAD_DIM, HEAD_DIM), F32)],
        compiler_params=_cparams("parallel", "arbitrary"), name="gdn",
    )(qkv, qkv, qkv, proj, proj, abt, lane_vec(a_log), lane_vec(dt_bias),
      sublane_vec(a_log), sublane_vec(dt_bias), gnorm_w.reshape(1, HEAD_DIM).astype(F32))


def _outproj_kernel(with_router, a1_ref, a2_ref, a3_ref, w1_ref, w2_ref, w3_ref, h_ref, gate_ref,
                    nw_ref, scale_ref, shift_ref, *rest):
    if with_router:
        wr_hi_ref, wr_lo_ref, hres_ref, hn_ref, route_ref = rest
    else:
        hres_ref, hn_ref = rest
    mix = _dot(a1_ref[...], w1_ref[...]) + _dot(a2_ref[...], w2_ref[...]) + _dot(a3_ref[...], w3_ref[...])
    h = h_ref[...] + gate_ref[...] * mix
    hres_ref[...] = h
    hn = _rms_mod(h, nw_ref[...], scale_ref[...], shift_ref[...])
    hn_ref[...] = hn.astype(hn_ref.dtype)
    if with_router:
        hn_hi = hn.astype(BF16)
        hn_lo = (hn - hn_hi.astype(F32)).astype(BF16)
        logits = (_dot(hn_hi, wr_hi_ref[...]) + _dot(hn_lo, wr_hi_ref[...])) + _dot(hn_hi, wr_lo_ref[...])
        lane = lax.broadcasted_iota(jnp.int32, logits.shape, 1)
        lg = jnp.where(lane < N_EXPERTS, logits, -jnp.inf)
        m1 = jnp.max(lg, axis=-1, keepdims=True)
        i1 = jnp.min(jnp.where(lg == m1, lane, LANES), axis=-1, keepdims=True)
        lg2 = jnp.where(lane == i1, -jnp.inf, lg)
        m2 = jnp.max(lg2, axis=-1, keepdims=True)
        i2 = jnp.min(jnp.where(lg2 == m2, lane, LANES), axis=-1, keepdims=True)
        e = jnp.exp(m2 - m1)
        g1 = 1.0 / (1.0 + e)
        g2 = e / (1.0 + e)
        route_ref[...] = jnp.where(lane == 0, i1.astype(F32),
                                   jnp.where(lane == 1, i2.astype(F32),
                                             jnp.where(lane == 2, g1, jnp.where(lane == 3, g2, 0.0))))


def _outproj(a1, a2, a3, w_out, h_res, seq, gate, norm_w, scale, shift, w_router):
    t, d = h_res.shape
    tm = min(256, t)
    with_router = w_router is not None
    k1, k2 = a1.shape[1], a2.shape[1]
    w1, w2, w3 = w_out[:k1], w_out[k1:k1 + k2], w_out[k1 + k2:]

    def rows(w):
        return pl.BlockSpec((tm, w), lambda i: (i, 0))

    def whole(a):
        return pl.BlockSpec(a.shape, lambda i: (0, 0))

    per_batch = pl.BlockSpec((None, 1, d), lambda i: ((i * tm) // seq, 0, 0))
    nw = norm_w.reshape(1, d).astype(F32)
    args = [a1, a2, a3, w1, w2, w3, h_res, gate, nw, scale, shift]
    specs = [rows(k1), rows(k2), rows(a3.shape[1]), whole(w1), whole(w2), whole(w3), rows(d), per_batch,
             whole(nw), per_batch, per_batch]
    out_shape = [jax.ShapeDtypeStruct((t, d), F32), jax.ShapeDtypeStruct((t, d), BF16)]
    out_specs = [rows(d), rows(d)]
    if with_router:
        wr = jnp.zeros((d, LANES), F32).at[:, :N_EXPERTS].set(w_router.astype(F32))
        wr_hi = wr.astype(BF16)
        wr_lo = (wr - wr_hi.astype(F32)).astype(BF16)
        args += [wr_hi, wr_lo]
        specs += [whole(wr_hi), whole(wr_lo)]
        out_shape.append(jax.ShapeDtypeStruct((t, LANES), F32))
        out_specs.append(rows(LANES))
    return pl.pallas_call(
        functools.partial(_outproj_kernel, with_router),
        grid=(t // tm,), in_specs=specs, out_specs=out_specs, out_shape=out_shape,
        compiler_params=_cparams("parallel"), name="out_proj",
    )(*args)


FFN_SUB = 256


def _ffn_kernel(row_gated, te_ref, nv_ref, x_ref, w1_ref, w3_ref, w2_ref, *rest):
    if row_gated:
        rg_ref, o_ref, acc_ref = rest
    else:
        o_ref, acc_ref = rest
    i = pl.program_id(0)
    f = pl.program_id(1)

    @pl.when(f == 0)
    def _():
        acc_ref[...] = jnp.zeros_like(acc_ref)

    @pl.when(i < nv_ref[0])
    def _():
        x = x_ref[...]
        n_sub = w1_ref.shape[1] // FFN_SUB

        def up(s):
            cols = slice(s * FFN_SUB, (s + 1) * FFN_SUB)
            return _dot(x, w1_ref[:, cols]), _dot(x, w3_ref[:, cols])

        def down(s, h):
            return _dot((_silu(h[0]) * h[1]).astype(BF16), w2_ref[s * FFN_SUB:(s + 1) * FFN_SUB, :])

        h = up(0)
        total = None
        for s in range(1, n_sub + 1):
            h_next = up(s) if s < n_sub else None
            part = down(s - 1, h)
            total = part if total is None else total + part
            h = h_next
        acc_ref[...] += total

    @pl.when(f == pl.num_programs(1) - 1)
    def _():
        y = acc_ref[...]
        if row_gated:
            y = y * rg_ref[...]
        o_ref[...] = y.astype(o_ref.dtype)


def _ffn(x, w1, w3, w2, tile_expert, n_valid, row_gate, tm, tf):
    r, d = x.shape
    ff = w1.shape[2]
    tm, tf = min(tm, r), min(tf, ff)
    in_specs = [pl.BlockSpec((tm, d), lambda i, f, te, nv: (i, 0)),
                pl.BlockSpec((None, d, tf), lambda i, f, te, nv: (te[i], 0, f)),
                pl.BlockSpec((None, d, tf), lambda i, f, te, nv: (te[i], 0, f)),
                pl.BlockSpec((None, tf, d), lambda i, f, te, nv: (te[i], f, 0))]
    args = [tile_expert, n_valid, x, w1, w3, w2]
    if row_gate is not None:
        in_specs.append(pl.BlockSpec((tm, 1), lambda i, f, te, nv: (i, 0)))
        args.append(row_gate)
    grid_spec = pltpu.PrefetchScalarGridSpec(
        num_scalar_prefetch=2, grid=(r // tm, ff // tf), in_specs=in_specs,
        out_specs=pl.BlockSpec((tm, d), lambda i, f, te, nv: (i, 0)),
        scratch_shapes=[pltpu.VMEM((tm, d), F32)])
    return pl.pallas_call(
        functools.partial(_ffn_kernel, row_gate is not None), grid_spec=grid_spec,
        out_shape=jax.ShapeDtypeStruct((r, d), BF16),
        compiler_params=_cparams("parallel", "arbitrary"), name="ffn",
    )(*args)


MOE_TILE = 512


def _moe_plan(route, tm):
    t = route.shape[0]
    e_flat = route[:, :2].astype(jnp.int32).reshape(-1)
    onehot = (e_flat[:, None] == jnp.arange(N_EXPERTS, dtype=jnp.int32)[None, :]).astype(jnp.int32)
    counts = jnp.sum(onehot, axis=0)
    rank = jnp.sum((jnp.cumsum(onehot, axis=0) - onehot) * onehot, axis=1)
    padded = ((counts + tm - 1) // tm) * tm
    ends = jnp.cumsum(padded)
    pos = (ends - padded)[e_flat] + rank
    n_rows = 2 * t + N_EXPERTS * tm
    n_tiles, n_blocks = n_rows // tm, t // tm
    tok_f = (jnp.arange(2 * t, dtype=jnp.int32) // 2).astype(F32)
    rows = jnp.stack([jnp.full((n_rows,), -1.0, F32), jnp.zeros((n_rows,), F32)], axis=1)
    rows = rows.at[pos].set(jnp.stack([tok_f, route[:, 2:4].reshape(-1)], axis=1))
    src_tok = rows[:, 0].astype(jnp.int32)
    row_gate = rows[:, 1]
    n_valid = (ends[-1] // tm).astype(jnp.int32)
    tile_start = jnp.arange(n_tiles, dtype=jnp.int32) * tm
    tile_expert = jnp.sum((tile_start[:, None] >= ends[None, :]).astype(jnp.int32), axis=1)
    last_expert = jnp.sum((tile_start[n_valid - 1] >= ends).astype(jnp.int32))
    tile_expert = jnp.where(tile_start < ends[-1], tile_expert, last_expert).astype(jnp.int32)

    blk = jnp.where(src_tok >= 0, src_tok // tm, -1).reshape(n_tiles, tm, 1)
    incidence = jnp.any(blk == jnp.arange(n_blocks, dtype=jnp.int32)[None, None, :], axis=1)
    unused = (tile_start >= ends[-1])[:, None] & (jnp.arange(n_blocks) == 0)[None, :]
    w_max = n_tiles + N_EXPERTS * n_blocks + N_EXPERTS

    def work_list(m):
        flat = m.reshape(-1)
        n = jnp.sum(flat.astype(jnp.int32))
        idx = jnp.nonzero(flat, size=w_max, fill_value=0)[0].astype(jnp.int32)
        w = jnp.arange(w_max, dtype=jnp.int32)
        valid = w < n
        idx = jnp.where(valid, idx, idx[n - 1])
        major, minor = idx // m.shape[1], idx % m.shape[1]
        first = valid & ((w == 0) | (major != jnp.roll(major, 1)))
        last = valid & ((w == n - 1) | (major != jnp.roll(major, -1)))
        return [a.astype(jnp.int32) for a in (major, minor, first, last, valid)]

    return (src_tok, row_gate.reshape(n_rows, 1), tile_expert, n_valid.reshape(1),
            work_list(incidence | unused), work_list(incidence.T))


def _dispatch_kernel(wi_ref, wj_ref, first_ref, last_ref, valid_ref, x_ref, tok_ref, o_ref):
    w = pl.program_id(0)
    tm = x_ref.shape[0]

    @pl.when(valid_ref[w] == 1)
    def _():
        col_tok = lax.broadcasted_iota(jnp.int32, (tm, tm), 1) + wj_ref[w] * tm
        onehot = jnp.where(tok_ref[...] == col_tok, 1.0, 0.0).astype(BF16)
        picked = _dot(onehot, x_ref[...]).astype(o_ref.dtype)

        @pl.when(first_ref[w] == 1)
        def _():
            o_ref[...] = picked

        @pl.when(first_ref[w] == 0)
        def _():
            o_ref[...] += picked


def _dispatch(x, src_tok, work, tm):
    t, d = x.shape
    n_rows = src_tok.shape[0]
    grid_spec = pltpu.PrefetchScalarGridSpec(
        num_scalar_prefetch=5, grid=(work[0].shape[0],),
        in_specs=[pl.BlockSpec((tm, d), lambda w, wi, wj, *_: (wj[w], 0)),
                  pl.BlockSpec((tm, 1), lambda w, wi, wj, *_: (wi[w], 0))],
        out_specs=pl.BlockSpec((tm, d), lambda w, wi, wj, *_: (wi[w], 0)))
    return pl.pallas_call(
        _dispatch_kernel, grid_spec=grid_spec,
        out_shape=jax.ShapeDtypeStruct((n_rows, d), x.dtype),
        compiler_params=_cparams("arbitrary"), name="moe_dispatch",
    )(*work, x, src_tok.reshape(n_rows, 1))


def _combine_kernel(has_mod, emit_res, vj_ref, vi_ref, first_ref, last_ref, valid_ref, y_ref, tok_ref, h_ref,
                    gate_ref, nw_ref, *rest):
    rest = list(rest)
    scale_ref = shift_ref = None
    if has_mod:
        scale_ref, shift_ref = rest.pop(0), rest.pop(0)
    hres_ref = rest.pop(0) if emit_res else None
    o_ref, acc_ref = rest
    w = pl.program_id(0)
    tm = y_ref.shape[0]

    @pl.when(valid_ref[w] == 1)
    def _():
        row_tok = lax.broadcasted_iota(jnp.int32, (tm, tm), 0) + vj_ref[w] * tm
        onehot = jnp.where(tok_ref[...] == row_tok, 1.0, 0.0).astype(BF16)
        part = _dot(onehot, y_ref[...])

        @pl.when(first_ref[w] == 1)
        def _():
            acc_ref[...] = part

        @pl.when(first_ref[w] == 0)
        def _():
            acc_ref[...] += part

        @pl.when(last_ref[w] == 1)
        def _():
            h = h_ref[...] + gate_ref[...] * acc_ref[...]
            if emit_res:
                hres_ref[...] = h
            scale = scale_ref[...] if has_mod else None
            shift = shift_ref[...] if has_mod else None
            o_ref[...] = _rms_mod(h, nw_ref[...], scale, shift).astype(o_ref.dtype)


def _combine_resnorm(ys, src_tok, work, tm, h_res, seq, gate, norm_w, scale, shift, emit_res, out_dtype):
    t, d = h_res.shape
    n_tiles = ys.shape[0] // tm
    has_mod = scale is not None

    def tok_rows(w, vj, vi, *_):
        return (vj[w], 0)

    per_batch = pl.BlockSpec((None, 1, d), lambda w, vj, *_: ((vj[w] * tm) // seq, 0, 0))
    in_specs = [pl.BlockSpec((tm, d), lambda w, vj, vi, *_: (vi[w], 0)),
                pl.BlockSpec((None, 1, tm), lambda w, vj, vi, *_: (vi[w], 0, 0)),
                pl.BlockSpec((tm, d), tok_rows), per_batch,
                pl.BlockSpec((1, d), lambda w, *_: (0, 0))]
    args = list(work) + [ys, src_tok.reshape(n_tiles, 1, tm), h_res, gate, norm_w.reshape(1, d)]
    if has_mod:
        in_specs += [per_batch, per_batch]
        args += [scale, shift]
    out_shape, out_specs = [], []
    if emit_res:
        out_shape.append(jax.ShapeDtypeStruct((t, d), F32))
        out_specs.append(pl.BlockSpec((tm, d), tok_rows))
    out_shape.append(jax.ShapeDtypeStruct((t, d), out_dtype))
    out_specs.append(pl.BlockSpec((tm, d), tok_rows))
    grid_spec = pltpu.PrefetchScalarGridSpec(
        num_scalar_prefetch=5, grid=(work[0].shape[0],), in_specs=in_specs, out_specs=out_specs,
        scratch_shapes=[pltpu.VMEM((tm, d), F32)])
    res = pl.pallas_call(
        functools.partial(_combine_kernel, has_mod, emit_res), grid_spec=grid_spec, out_shape=out_shape,
        compiler_params=_cparams("arbitrary"), name="moe_combine",
    )(*args)
    return res if emit_res else res[0]


def kernel(x, c, positions, w_ada, b_ada, norm_mix, norm_ffn, norm_final, w_in, w_out, ssm_a_re, ssm_a_im, ssm_log_dt, ssm_b_re, ssm_b_im, ssm_c_re, ssm_c_im, ssm_d, ssm_w_glu, diff_lam_q1, diff_lam_k1, diff_lam_q2, diff_lam_k2, diff_subln, gdn_conv, gdn_a_log, gdn_dt_bias, gdn_norm, ffn_w1, ffn_w3, ffn_w2, moe_router, moe_w1, moe_w3, moe_w2):
    bsz, seq, d = x.shape
    t = bsz * seq
    depth = w_in.shape[0]
    h_res = x.astype(F32).reshape(t, d)

    c_pad = jnp.zeros((SUBLANES, d), F32).at[:bsz].set(c.astype(F32))
    mod = _ada(c_pad, w_ada, b_ada)[:, :bsz]
    mods = [[m.reshape(bsz, 1, d) for m in jnp.split(mod[l], 6, axis=-1)] for l in range(depth)]
    rope_tables = _rope_tables(positions)

    hn = _resnorm(h_res, seq, None, None, norm_mix[0].astype(F32), mods[0][1], mods[0][0], False, BF16)
    out = None
    for l in range(depth):
        shift1, scale1, gate1, shift2, scale2, gate2 = mods[l]
        is_moe = l % 2 == 1
        proj = _in_proj(hn, w_in, l)

        scan_steps = max(1, math.ceil(math.log2(seq // SSM_CHUNK)))
        tables = _s5_tables(ssm_a_re[l], ssm_a_im[l], ssm_log_dt[l], ssm_b_re[l], ssm_b_im[l],
                            ssm_c_re[l], ssm_c_im[l], scan_steps)
        y_ssm = _s5_mixer(proj, bsz, seq, tables, ssm_d[l], ssm_w_glu[l].astype(BF16))

        lambda_init = 0.8 - 0.6 * math.exp(-0.3 * l)
        qk = _rope(proj, rope_tables)
        y_diff = _diff_attention(qk, proj, bsz, seq,
                                 (diff_lam_q1[l], diff_lam_k1[l], diff_lam_q2[l], diff_lam_k2[l]),
                                 diff_subln[l], lambda_init)

        qkv = _gdn_conv(proj, gdn_conv[l], seq)
        y_gdn = _gdn(qkv, proj, bsz, seq, gdn_a_log[l], gdn_dt_bias[l], gdn_norm[l])

        res = _outproj(y_ssm, y_diff, y_gdn, w_out[l].astype(BF16), h_res, seq, gate1,
                       norm_ffn[l], scale2, shift2, moe_router[l // 2] if is_moe else None)
        h_res, hn2 = res[0], res[1]

        last = l + 1 == depth
        if last:
            nxt = (norm_final.astype(F32), None, None, False, x.dtype)
        else:
            nxt = (norm_mix[l + 1].astype(F32), mods[l + 1][1], mods[l + 1][0], True, BF16)
        if is_moe:
            src_tok, row_gate, tile_expert, n_valid, work_sorted, work_token = _moe_plan(res[2], MOE_TILE)
            xs = _dispatch(hn2, src_tok, work_sorted, MOE_TILE)
            ys = _ffn(xs, moe_w1[l // 2].astype(BF16), moe_w3[l // 2].astype(BF16),
                      moe_w2[l // 2].astype(BF16), tile_expert, n_valid, row_gate, MOE_TILE, 1024)
            res = _combine_resnorm(ys, src_tok, work_token, MOE_TILE, h_res, seq, gate2, *nxt)
        else:
            n_tiles = t // min(512, t)
            ys = _ffn(hn2, ffn_w1[l // 2:l // 2 + 1].astype(BF16), ffn_w3[l // 2:l // 2 + 1].astype(BF16),
                      ffn_w2[l // 2:l // 2 + 1].astype(BF16), jnp.zeros((n_tiles,), jnp.int32),
                      jnp.full((1,), n_tiles, jnp.int32), None, 512, 512)
            res = _resnorm(h_res, seq, ys, gate2, *nxt)
        if last:
            out = res
        else:
            h_res, hn = res
    return out.reshape(bsz, seq, d)
```

```python
import functools
import math

import numpy as np
import jax
import jax.numpy as jnp
from jax import lax
from jax.experimental import pallas as pl
from jax.experimental.pallas import tpu as pltpu

F32 = jnp.float32
BF16 = jnp.bfloat16
HIGHEST = lax.Precision.HIGHEST

D_MODEL = 2048
SSM_WIDTH = 512
SSM_CH = 16
SSM_GROUPS = SSM_WIDTH // SSM_CH
SSM_STATE = 64
SSM_CHUNK = 16
DIFF_WIDTH = 768
HEAD_DIM = 128
DIFF_HEADS = DIFF_WIDTH // HEAD_DIM
DIFF_QK_DIM = HEAD_DIM // 2
GDN_WIDTH = 768
GDN_HEADS = GDN_WIDTH // HEAD_DIM
CONV_WIDTH = 4
GDN_CHUNK = 64
ROPE_THETA = 500000.0
ROPE_DIM = DIFF_QK_DIM // 4
ROPE_HALF = ROPE_DIM // 2
N_EXPERTS = 8
EPS = 1e-6
LANES = 128
SUBLANES = 8

SEG = 768
SEG_GQ, SEG_GK, SEG_GV, SEG_GZ, SEG_DQ, SEG_DK, SEG_DV, SEG_U = range(8)
PROJ_WIDTH = 8 * SEG
AB_OFF = SSM_WIDTH

VMEM_LIMIT = 56 * 1024 * 1024


def _cparams(*sem):
    return pltpu.CompilerParams(dimension_semantics=sem, vmem_limit_bytes=VMEM_LIMIT)


def _dot(a, b):
    return jnp.dot(a, b, preferred_element_type=F32)


def _dot_nt(a, b):
    return lax.dot_general(a, b, (((1,), (1,)), ((), ())), preferred_element_type=F32)


def _dot_tn(a, b):
    return lax.dot_general(a, b, (((0,), (0,)), ((), ())), preferred_element_type=F32)


def _sigmoid(x):
    return 1.0 / (1.0 + jnp.exp(-x))


def _silu(x):
    return x * _sigmoid(x)


def _ada_kernel(c_ref, w_ref, b_ref, o_ref):
    cond = _silu(c_ref[...])
    o_ref[...] = _dot(cond.astype(BF16), w_ref[...].astype(BF16)) + b_ref[...]


def _ada(c_pad, w_ada, b_ada):
    depth, d, n = w_ada.shape
    tn = 1024
    return pl.pallas_call(
        _ada_kernel,
        grid=(depth, n // tn),
        in_specs=[pl.BlockSpec((SUBLANES, d), lambda l, j: (0, 0)),
                  pl.BlockSpec((None, d, tn), lambda l, j: (l, 0, j)),
                  pl.BlockSpec((None, 1, tn), lambda l, j: (l, 0, j))],
        out_specs=pl.BlockSpec((None, SUBLANES, tn), lambda l, j: (l, 0, j)),
        out_shape=jax.ShapeDtypeStruct((depth, SUBLANES, n), F32),
        compiler_params=_cparams("arbitrary", "arbitrary"),
        name="ada",
    )(c_pad, w_ada, b_ada.reshape(depth, 1, n))


def _rms_mod(h, w, scale, shift):
    y = h * lax.rsqrt(jnp.mean(h * h, axis=-1, keepdims=True) + EPS) * w
    if scale is not None:
        y = y * (1.0 + scale) + shift
    return y


def _resnorm_kernel(has_delta, has_mod, emit_res, *refs):
    refs = list(refs)
    h = refs.pop(0)[...]
    if has_delta:
        y = refs.pop(0)[...].astype(F32)
        h = h + refs.pop(0)[...] * y
    w = refs.pop(0)[...]
    scale = shift = None
    if has_mod:
        scale = refs.pop(0)[...]
        shift = refs.pop(0)[...]
    if emit_res:
        refs.pop(0)[...] = h
    o_ref = refs.pop(0)
    o_ref[...] = _rms_mod(h, w, scale, shift).astype(o_ref.dtype)


def _resnorm(h_res, seq, delta, gate, norm_w, scale, shift, emit_res, out_dtype):
    t, d = h_res.shape
    tm = min(256, t)
    row = pl.BlockSpec((tm, d), lambda i: (i, 0))
    per_batch = pl.BlockSpec((None, 1, d), lambda i: ((i * tm) // seq, 0, 0))
    args, specs = [h_res], [row]
    if delta is not None:
        args += [delta, gate]
        specs += [row, per_batch]
    args.append(norm_w.reshape(1, d))
    specs.append(pl.BlockSpec((1, d), lambda i: (0, 0)))
    if scale is not None:
        args += [scale, shift]
        specs += [per_batch, per_batch]
    out_shape, out_specs = [], []
    if emit_res:
        out_shape.append(jax.ShapeDtypeStruct((t, d), F32))
        out_specs.append(row)
    out_shape.append(jax.ShapeDtypeStruct((t, d), out_dtype))
    out_specs.append(row)
    res = pl.pallas_call(
        functools.partial(_resnorm_kernel, delta is not None, scale is not None, emit_res),
        grid=(t // tm,), in_specs=specs, out_specs=out_specs, out_shape=out_shape,
        compiler_params=_cparams("parallel"), name="resnorm",
    )(*args)
    return res if emit_res else res[0]


IN_BLOCK = 2 * LANES


def _in_proj_blocks():
    src_segments = [(SEG_U, SSM_WIDTH), (SEG_DQ, DIFF_WIDTH), (SEG_DK, DIFF_WIDTH), (SEG_DV, DIFF_WIDTH),
                    (SEG_GQ, 3 * GDN_WIDTH), (SEG_GZ, GDN_WIDTH)]
    dest = []
    for seg, width in src_segments:
        dest += [seg * SEG // IN_BLOCK + b for b in range(width // IN_BLOCK)]
    dest.append((SEG_U * SEG + AB_OFF) // IN_BLOCK)
    return np.asarray(dest, np.int32)


def _in_proj_kernel(n_cols, dest_ref, x_ref, w_ref, o_ref, wb_ref):
    j = pl.program_id(1)

    @pl.when(pl.program_id(0) == 0)
    def _():
        col = lax.broadcasted_iota(jnp.int32, w_ref.shape, 1) + j * IN_BLOCK
        wb_ref[j] = jnp.where(col < n_cols, w_ref[...], 0.0).astype(BF16)

    o_ref[...] = _dot(x_ref[...], wb_ref[j]).astype(o_ref.dtype)


def _in_proj(x, w_in_l):
    m, k = x.shape
    n_cols = w_in_l.shape[1]
    dest = _in_proj_blocks()
    n_blocks = dest.shape[0]
    tm = min(2048, m)
    grid_spec = pltpu.PrefetchScalarGridSpec(
        num_scalar_prefetch=1, grid=(m // tm, n_blocks),
        in_specs=[pl.BlockSpec((tm, k), lambda i, j, dest: (i, 0)),
                  pl.BlockSpec((k, IN_BLOCK), lambda i, j, dest: (0, jnp.where(i == 0, j, n_blocks - 1)))],
        out_specs=pl.BlockSpec((tm, IN_BLOCK), lambda i, j, dest: (i, dest[j])),
        scratch_shapes=[pltpu.VMEM((n_blocks, k, IN_BLOCK), BF16)])
    return pl.pallas_call(
        functools.partial(_in_proj_kernel, n_cols), grid_spec=grid_spec,
        out_shape=jax.ShapeDtypeStruct((m, PROJ_WIDTH), BF16),
        compiler_params=_cparams("arbitrary", "arbitrary"), name="in_proj",
    )(jnp.asarray(dest), x, w_in_l)


def _s5_tables(a_re, a_im, log_dt, b_re, b_im, c_re, c_im, n_steps):
    L, G, P, H = SSM_CHUNK, SSM_GROUPS, SSM_STATE, SSM_CH
    lam = lax.complex(a_re.astype(F32), a_im.astype(F32))
    log_lam_bar = lam * jnp.exp(log_dt.astype(F32))[:, None]
    lam_bar = jnp.exp(log_lam_bar)
    b_bar = ((lam_bar - 1.0) / lam)[:, :, None] * lax.complex(b_re.astype(F32), b_im.astype(F32))
    c_mat = lax.complex(c_re.astype(F32), c_im.astype(F32))
    steps = jnp.arange(L + 1, dtype=F32)
    pw = jnp.exp(log_lam_bar[:, None, :] * steps[None, :, None])
    kern = jnp.real(jnp.einsum('ghp,gjp,gpi->gjih', c_mat, pw[:, :L], b_bar))
    lag = jnp.arange(L)[None, :] - jnp.arange(L)[:, None]
    tm = jnp.where((lag >= 0)[None, :, :, None, None], kern[:, jnp.clip(lag, 0, L - 1)], 0.0)
    tm = tm.transpose(0, 2, 4, 1, 3).reshape(G, L * H, L * H)
    zc = pw[:, L - 1 - jnp.arange(L)][:, :, :, None] * b_bar[:, None]
    zc = zc.transpose(0, 2, 1, 3).reshape(G, P, L * H)
    zm = jnp.concatenate([jnp.real(zc), jnp.imag(zc)], axis=1)
    cl = (c_mat[:, None] * pw[:, 1:L + 1][:, :, None, :]).reshape(G, L * H, P)
    ym = jnp.concatenate([jnp.real(cl), -jnp.imag(cl)], axis=2)
    a_pows = jnp.exp(log_lam_bar[:, None, :] * (L * 2.0 ** jnp.arange(n_steps, dtype=F32))[None, :, None])
    return (tm.astype(BF16), zm.astype(BF16), ym.astype(BF16),
            jnp.real(a_pows)[..., None], jnp.imag(a_pows)[..., None])


def _s5_core_kernel(n_steps, chunks_per_seq, u_ref, tm_ref, zm_ref, ym_ref, are_ref, aim_ref, o_ref):
    p = SSM_STATE
    n_pos, n_ch, n_lanes = u_ref.shape
    u = u_ref[...].reshape(n_pos * n_ch, n_lanes)
    z = _dot(zm_ref[...], u)
    x_re, x_im = z[:p], z[p:]
    c_idx = lax.broadcasted_iota(jnp.int32, x_re.shape, 1) % chunks_per_seq
    for k in range(n_steps):
        d = 1 << k
        a_re, a_im = are_ref[k], aim_ref[k]
        inside = c_idx >= d
        s_re = jnp.where(inside, pltpu.roll(x_re, d, axis=1), 0.0)
        s_im = jnp.where(inside, pltpu.roll(x_im, d, axis=1), 0.0)
        x_re, x_im = x_re + (a_re * s_re - a_im * s_im), x_im + (a_re * s_im + a_im * s_re)
    inside = c_idx >= 1
    x_prev = jnp.concatenate([jnp.where(inside, pltpu.roll(x_re, 1, axis=1), 0.0),
                              jnp.where(inside, pltpu.roll(x_im, 1, axis=1), 0.0)], axis=0)
    y = _dot(tm_ref[...], u) + _dot(ym_ref[...], x_prev.astype(BF16))
    o_ref[...] = y.astype(o_ref.dtype).reshape(o_ref.shape)


def _s5_core(u_t, tables, chunks_per_seq):
    n_pos, g, n_ch, r = u_t.shape
    tm, zm, ym, are, aim = tables
    p = SSM_STATE
    w = n_pos * n_ch
    n_steps = are.shape[1]

    def grp(*shape):
        return pl.BlockSpec((None,) + shape, lambda i: (i,) + (0,) * len(shape))

    slab = pl.BlockSpec((n_pos, None, n_ch, r), lambda i: (0, i, 0, 0))
    return pl.pallas_call(
        functools.partial(_s5_core_kernel, n_steps, chunks_per_seq),
        grid=(g,),
        in_specs=[slab, grp(w, w), grp(2 * p, w), grp(w, 2 * p), grp(n_steps, p, 1), grp(n_steps, p, 1)],
        out_specs=slab,
        out_shape=jax.ShapeDtypeStruct(u_t.shape, BF16),
        compiler_params=_cparams("parallel"), name="s5_core",
    )(u_t, tm, zm, ym, are, aim)


def _gelu_tanh(x):
    return 0.5 * x * (1.0 + jnp.tanh(math.sqrt(2.0 / math.pi) * (x + 0.044715 * (x * x * x))))


def _s5_post_kernel(y_ref, u_ref, d_ref, w_ref, o_ref):
    u = u_ref[:, :SSM_WIDTH].astype(F32)
    y = _gelu_tanh(y_ref[...].astype(F32) + d_ref[...] * u)
    o_ref[...] = (y * _sigmoid(_dot(y.astype(BF16), w_ref[...]))).astype(o_ref.dtype)


def _s5_post(y_core, proj, d_skip, w_glu):
    t = y_core.shape[0]
    tm = min(512, t)
    return pl.pallas_call(
        _s5_post_kernel,
        grid=(t // tm,),
        in_specs=[pl.BlockSpec((tm, SSM_WIDTH), lambda i: (i, 0)),
                  pl.BlockSpec((tm, SEG), lambda i: (i, SEG_U)),
                  pl.BlockSpec((1, SSM_WIDTH), lambda i: (0, 0)),
                  pl.BlockSpec((SSM_WIDTH, SSM_WIDTH), lambda i: (0, 0))],
        out_specs=pl.BlockSpec((tm, SSM_WIDTH), lambda i: (i, 0)),
        out_shape=jax.ShapeDtypeStruct((t, SSM_WIDTH), BF16),
        compiler_params=_cparams("parallel"), name="s5_post",
    )(y_core, proj, d_skip.reshape(1, SSM_WIDTH).astype(F32), w_glu)


def _s5_mixer(proj, bsz, seq, tables, d_skip, w_glu):
    L, G, H = SSM_CHUNK, SSM_GROUPS, SSM_CH
    t = bsz * seq
    nc = seq // L
    u = proj[:, SEG_U * SEG:SEG_U * SEG + SSM_WIDTH]
    u_t = u.reshape(t // L, L * G * H).T.reshape(L, G, H, t // L)
    y_t = _s5_core(u_t, tables, nc)
    y_core = y_t.reshape(L * G * H, t // L).T.reshape(t, SSM_WIDTH)
    return _s5_post(y_core, proj, d_skip, w_glu)


def _rope_tables(positions):
    inv_freq = ROPE_THETA ** (-jnp.arange(0, ROPE_DIM, 2, dtype=F32) / ROPE_DIM)
    ang = positions.astype(F32).reshape(-1)[:, None] * inv_freq
    cos, sin = jnp.cos(ang), jnp.sin(ang)
    r = np.arange(LANES) % DIFF_QK_DIM
    first = jnp.asarray(r < ROPE_HALF)[None, :]
    second = jnp.asarray((r >= ROPE_HALF) & (r < ROPE_DIM))[None, :]
    idx = jnp.asarray(r % ROPE_HALF)
    cos_l, sin_l = cos[:, idx], sin[:, idx]
    cosf = jnp.where(first | second, cos_l, 1.0)
    sin_a = jnp.where(first, -sin_l, 0.0)
    sin_b = jnp.where(second, sin_l, 0.0)
    return cosf, sin_a, sin_b


def _rope_kernel(x_ref, c_ref, sa_ref, sb_ref, o_ref):
    cosf, sin_a, sin_b = c_ref[...], sa_ref[...], sb_ref[...]
    n_slabs = x_ref.shape[1] // LANES
    for s in range(n_slabs):
        x = x_ref[:, s * LANES:(s + 1) * LANES].astype(F32)
        y = (x * cosf + pltpu.roll(x, LANES - ROPE_HALF, axis=1) * sin_a
             + pltpu.roll(x, ROPE_HALF, axis=1) * sin_b)
        if s < n_slabs // 2:
            y = y * (DIFF_QK_DIM ** -0.5 * math.log2(math.e))
        o_ref[:, s * LANES:(s + 1) * LANES] = y.astype(o_ref.dtype)


def _rope(proj, tables):
    t = proj.shape[0]
    tm = min(512, t)
    w = 2 * DIFF_WIDTH
    tab = pl.BlockSpec((tm, LANES), lambda i: (i, 0))
    return pl.pallas_call(
        _rope_kernel,
        grid=(t // tm,),
        in_specs=[pl.BlockSpec((tm, w), lambda i: (i, SEG_DQ // 2)), tab, tab, tab],
        out_specs=pl.BlockSpec((tm, w), lambda i: (i, 0)),
        out_shape=jax.ShapeDtypeStruct((t, w), BF16),
        compiler_params=_cparams("parallel"), name="rope",
    )(proj, *tables)


def _attn_kernel(tq, lambda_init, q_ref, k_ref, vt_ref, lq1_ref, lk1_ref, lq2_ref, lk2_ref, sw_ref, o_ref):
    qi = pl.program_id(2)
    q = q_ref[...]
    lane = lax.broadcasted_iota(jnp.int32, q.shape, 1)
    zero = jnp.zeros_like(q)
    qm = (jnp.where(lane < DIFF_QK_DIM, q, zero), jnp.where(lane >= DIFF_QK_DIM, q, zero))

    def block(kv, masked, carry):
        start = pl.multiple_of(kv * tq, tq)
        k = k_ref[pl.ds(start, tq), :]
        v_t = vt_ref[:, pl.ds(start, tq)]
        scores = [_dot_nt(k, qm[i]) for i in range(2)]
        new = []
        for i in range(2):
            m_old, l_old, acc = carry[i]
            s = scores[i]
            if masked:
                key = lax.broadcasted_iota(jnp.int32, s.shape, 0)
                qry = lax.broadcasted_iota(jnp.int32, s.shape, 1)
                s = jnp.where(key <= qry, s, -jnp.inf)
            m_new = jnp.maximum(m_old, jnp.max(s, axis=0, keepdims=True))
            alpha = jnp.exp2(m_old - m_new)
            p = jnp.exp2(s - m_new)
            l_new = alpha * l_old + jnp.sum(p, axis=0, keepdims=True)
            new.append((m_new, l_new, alpha * acc + _dot(v_t, p.astype(BF16))))
        return tuple(new)

    init = tuple((jnp.full((1, tq), -jnp.inf, F32), jnp.zeros((1, tq), F32), jnp.zeros((HEAD_DIM, tq), F32))
                 for _ in range(2))
    carry = lax.fori_loop(0, qi, lambda kv, c: block(kv, False, c), init)
    (_, l1, acc1), (_, l2, acc2) = block(qi, True, carry)
    lam = (jnp.exp(jnp.sum(lq1_ref[...] * lk1_ref[...], axis=-1, keepdims=True))
           - jnp.exp(jnp.sum(lq2_ref[...] * lk2_ref[...], axis=-1, keepdims=True)) + lambda_init)
    o_t = acc1 / l1 - lam * (acc2 / l2)
    o_t = o_t * lax.rsqrt(jnp.mean(o_t * o_t, axis=0, keepdims=True) + EPS) * sw_ref[...] * (1.0 - lambda_init)
    o_ref[...] = o_t.T.astype(o_ref.dtype)


def _diff_attention(qk, proj, bsz, seq, lam_params, subln_w, lambda_init):
    tq = min(512, seq)
    nh = DIFF_HEADS
    qk3 = qk.reshape(bsz, seq, 2 * DIFF_WIDTH)
    v_t = proj[:, SEG_DV * SEG:(SEG_DV + 1) * SEG].reshape(bsz, seq, DIFF_WIDTH).transpose(0, 2, 1)
    vec = pl.BlockSpec((1, DIFF_QK_DIM), lambda b, h, i: (0, 0))
    out = pl.pallas_call(
        functools.partial(_attn_kernel, tq, lambda_init),
        grid=(bsz, nh, seq // tq),
        in_specs=[pl.BlockSpec((None, tq, HEAD_DIM), lambda b, h, i: (b, i, h)),
                  pl.BlockSpec((None, seq, HEAD_DIM), lambda b, h, i: (b, 0, nh + h)),
                  pl.BlockSpec((None, HEAD_DIM, seq), lambda b, h, i: (b, h, 0)),
                  vec, vec, vec, vec,
                  pl.BlockSpec((HEAD_DIM, 1), lambda b, h, i: (0, 0))],
        out_specs=pl.BlockSpec((None, tq, HEAD_DIM), lambda b, h, i: (b, i, h)),
        out_shape=jax.ShapeDtypeStruct((bsz, seq, DIFF_WIDTH), BF16),
        compiler_params=_cparams("parallel", "parallel", "arbitrary"), name="diff_attn",
    )(qk3, qk3, v_t, *[p.reshape(1, DIFF_QK_DIM).astype(F32) for p in lam_params],
      subln_w.reshape(HEAD_DIM, 1).astype(F32))
    return out.reshape(bsz * seq, DIFF_WIDTH)


HALO_ROWS = 16


def _conv_kernel(tiles_per_seq, x_ref, prev_ref, w_ref, o_ref):
    i = pl.program_id(0)
    j = pl.program_id(1)
    x = x_ref[...].astype(F32)
    prev = prev_ref[...].astype(F32)[HALO_ROWS - SUBLANES:]
    prev = jnp.where(i % tiles_per_seq == 0, jnp.zeros_like(prev), prev)
    w = w_ref[...]
    head_rows = lax.broadcasted_iota(jnp.int32, (SUBLANES, x.shape[1]), 0)
    y = x * w[CONV_WIDTH - 1:CONV_WIDTH, :]
    y_head = y[:SUBLANES]
    for back in range(1, CONV_WIDTH):
        wk = w[CONV_WIDTH - 1 - back:CONV_WIDTH - back, :]
        y = y + pltpu.roll(x, back, axis=0) * wk
        mixed = jnp.where(head_rows < back, pltpu.roll(prev, back, axis=0), pltpu.roll(x[:SUBLANES], back, axis=0))
        y_head = y_head + mixed * wk
    is_qk = j < 2 * GDN_WIDTH // x.shape[1]
    q_scale = jnp.where(j < GDN_WIDTH // x.shape[1], HEAD_DIM ** -0.5, 1.0)

    def finish(v):
        v = _silu(v)
        outs = []
        for h in range(v.shape[1] // HEAD_DIM):
            vh = v[:, h * HEAD_DIM:(h + 1) * HEAD_DIM]
            nrm = lax.rsqrt(jnp.sum(vh * vh, axis=-1, keepdims=True) + EPS) * q_scale
            outs.append(vh * jnp.where(is_qk, nrm, 1.0))
        return jnp.concatenate(outs, axis=1)

    o_ref[...] = finish(jnp.concatenate([y_head, y[SUBLANES:]], axis=0)).astype(o_ref.dtype)


def _gdn_conv(proj, conv_w, seq):
    t = proj.shape[0]
    tm = min(512, seq)
    cw = GDN_WIDTH
    nj = 3 * GDN_WIDTH // cw
    halo = tm // HALO_ROWS
    return pl.pallas_call(
        functools.partial(_conv_kernel, seq // tm),
        grid=(t // tm, nj),
        in_specs=[pl.BlockSpec((tm, cw), lambda i, j: (i, j)),
                  pl.BlockSpec((HALO_ROWS, cw), lambda i, j: (jnp.maximum(i * halo - 1, 0), j)),
                  pl.BlockSpec((CONV_WIDTH, cw), lambda i, j: (0, j))],
        out_specs=pl.BlockSpec((tm, cw), lambda i, j: (i, j)),
        out_shape=jax.ShapeDtypeStruct((t, 3 * GDN_WIDTH), BF16),
        compiler_params=_cparams("parallel", "parallel"), name="gdn_conv",
    )(proj, proj, conv_w.astype(F32))


def _softplus(x):
    return jnp.maximum(x, 0.0) + jnp.log(1.0 + jnp.exp(-jnp.abs(x)))


def _gdn_kernel(n_chunks, q_ref, k_ref, v_ref, z_ref, ab_ref, abt_ref, alog_ref, dtb_ref, alog_t_ref, dtb_t_ref,
                gw_ref, o_ref, state_ref):
    C = GDN_CHUNK
    nh = GDN_HEADS
    rows = n_chunks * C

    @pl.when(pl.program_id(1) == 0)
    def _():
        state_ref[...] = jnp.zeros_like(state_ref)

    neg_a = -jnp.exp(alog_ref[...])
    g_col = neg_a * _softplus(ab_ref[:, :LANES].astype(F32) + dtb_ref[...])
    g_row = -jnp.exp(alog_t_ref[...]) * _softplus(abt_ref[...] + dtb_t_ref[...])
    r_i = lax.broadcasted_iota(jnp.int32, (rows, rows), 0)
    c_i = lax.broadcasted_iota(jnp.int32, (rows, rows), 1)
    same = (r_i // C) == (c_i // C)
    tri_l = jnp.where(same & (c_i <= r_i), 1.0, 0.0).astype(F32)
    tri_u = jnp.where(same & (r_i <= c_i), 1.0, 0.0).astype(F32)
    gc_all = jnp.dot(tri_l, g_col, precision=HIGHEST, preferred_element_type=F32)
    gr_all = jnp.dot(g_row, tri_u, precision=HIGHEST, preferred_element_type=F32)
    b_all = _sigmoid(ab_ref[:, :LANES].astype(F32))

    incl = same & (r_i >= c_i)
    strict = same & (r_i > c_i)
    eye = jnp.where(r_i == c_i, 1.0, 0.0).astype(F32)
    gw = gw_ref[...]

    heads = range(nh)
    sls = [slice(h * HEAD_DIM, (h + 1) * HEAD_DIM) for h in heads]
    q = [q_ref[:, sl] for sl in sls]
    k = [k_ref[:, sl] for sl in sls]
    kf = [x.astype(F32) for x in k]
    gcol = [gc_all[:, h:h + 1] for h in heads]
    beta = [b_all[:, nh + h:nh + h + 1] for h in heads]
    decay = [jnp.exp(jnp.where(incl, gcol[h] - gr_all[h:h + 1, :], -jnp.inf)) for h in heads]
    pw = [jnp.where(strict, -(beta[h] * _dot_nt(k[h], k[h]) * decay[h]), 0.0) for h in heads]
    t_mat = [eye + p for p in pw]
    pw = [p.astype(BF16) for p in pw]
    for _ in range(int(math.log2(C)) - 1):
        pw = [_dot(p, p).astype(BF16) for p in pw]
        t_mat = [t + _dot(t.astype(BF16), p) for t, p in zip(t_mat, pw)]
    eg = [jnp.exp(g) for g in gcol]
    uw = [_dot(t_mat[h].astype(BF16),
               jnp.concatenate([v_ref[:, sls[h]].astype(F32) * beta[h], kf[h] * (beta[h] * eg[h])],
                               axis=1).astype(BF16)) for h in heads]
    u_all = [x[:, :HEAD_DIM] for x in uw]
    w_all = [x[:, HEAD_DIM:].astype(BF16) for x in uw]
    qk_all = [jnp.where(incl, _dot_nt(q[h], k[h]) * decay[h], 0.0).astype(BF16) for h in heads]
    qe_all = [(q[h].astype(F32) * eg[h]).astype(BF16) for h in heads]
    g_last = [[g[(c + 1) * C - 1:(c + 1) * C, :] for c in range(n_chunks)] for g in gcol]
    k_dec = [(kf[h] * jnp.exp(jnp.concatenate([jnp.broadcast_to(g, (C, 1)) for g in g_last[h]], axis=0)
                              - gcol[h])).astype(BF16) for h in heads]

    state = [state_ref[h] for h in heads]
    for c in range(n_chunks):
        rs = slice(c * C, (c + 1) * C)
        sb = [s.astype(BF16) for s in state]
        v_new = [(u_all[h][rs] - _dot(w_all[h][rs], sb[h])).astype(BF16) for h in heads]
        o = [_dot(qe_all[h][rs], sb[h]) + _dot(qk_all[h][rs, rs], v_new[h]) for h in heads]
        state = [state[h] * jnp.exp(g_last[h][c]) + _dot_tn(k_dec[h][rs], v_new[h]) for h in heads]
        for h in heads:
            gated = gw * _silu(z_ref[rs, sls[h]].astype(F32))
            o_h = o[h] * lax.rsqrt(jnp.mean(o[h] * o[h], axis=-1, keepdims=True) + EPS) * gated
            o_ref[rs, sls[h]] = o_h.astype(o_ref.dtype)
    for h in heads:
        state_ref[h] = state[h]


def _gdn(qkv, proj, bsz, seq, a_log, dt_bias, gnorm_w):
    t = bsz * seq
    n_chunks = min(4, seq // GDN_CHUNK)
    rows = n_chunks * GDN_CHUNK
    steps = seq // rows
    ab_off = SEG_U * SEG + AB_OFF
    abt = proj[:, ab_off:ab_off + 2 * SUBLANES].astype(F32).T

    def lane_vec(p):
        return jnp.zeros((1, LANES), F32).at[0, :GDN_HEADS].set(p.astype(F32))

    def sublane_vec(p):
        return jnp.zeros((2 * SUBLANES, 1), F32).at[:GDN_HEADS, 0].set(p.astype(F32))

    def rowblk(seg):
        return pl.BlockSpec((rows, SEG), lambda b, s: (b * steps + s, seg))

    return pl.pallas_call(
        functools.partial(_gdn_kernel, n_chunks),
        grid=(bsz, steps),
        in_specs=[rowblk(0), rowblk(1), rowblk(2),
                  pl.BlockSpec((rows, SEG), lambda b, s: (b * steps + s, SEG_GZ)),
                  pl.BlockSpec((rows, 2 * LANES), lambda b, s: (b * steps + s, ab_off // (2 * LANES))),
                  pl.BlockSpec((2 * SUBLANES, rows), lambda b, s: (0, b * steps + s)),
                  pl.BlockSpec((1, LANES), lambda b, s: (0, 0)),
                  pl.BlockSpec((1, LANES), lambda b, s: (0, 0)),
                  pl.BlockSpec((2 * SUBLANES, 1), lambda b, s: (0, 0)),
                  pl.BlockSpec((2 * SUBLANES, 1), lambda b, s: (0, 0)),
                  pl.BlockSpec((1, HEAD_DIM), lambda b, s: (0, 0))],
        out_specs=pl.BlockSpec((rows, GDN_WIDTH), lambda b, s: (b * steps + s, 0)),
        out_shape=jax.ShapeDtypeStruct((t, GDN_WIDTH), BF16),
        scratch_shapes=[pltpu.VMEM((GDN_HEADS, HEAD_DIM, HEAD_DIM), F32)],
        compiler_params=_cparams("parallel", "arbitrary"), name="gdn",
    )(qkv, qkv, qkv, proj, proj, abt, lane_vec(a_log), lane_vec(dt_bias),
      sublane_vec(a_log), sublane_vec(dt_bias), gnorm_w.reshape(1, HEAD_DIM).astype(F32))


def _outproj_kernel(with_router, a1_ref, a2_ref, a3_ref, w1_ref, w2_ref, w3_ref, h_ref, gate_ref,
                    nw_ref, scale_ref, shift_ref, *rest):
    if with_router:
        wr_hi_ref, wr_lo_ref, hres_ref, hn_ref, route_ref = rest
    else:
        hres_ref, hn_ref = rest
    mix = _dot(a1_ref[...], w1_ref[...]) + _dot(a2_ref[...], w2_ref[...]) + _dot(a3_ref[...], w3_ref[...])
    h = h_ref[...] + gate_ref[...] * mix
    hres_ref[...] = h
    hn = _rms_mod(h, nw_ref[...], scale_ref[...], shift_ref[...])
    hn_ref[...] = hn.astype(hn_ref.dtype)
    if with_router:
        hn_hi = hn.astype(BF16)
        hn_lo = (hn - hn_hi.astype(F32)).astype(BF16)
        logits = (_dot(hn_hi, wr_hi_ref[...]) + _dot(hn_lo, wr_hi_ref[...])) + _dot(hn_hi, wr_lo_ref[...])
        lane = lax.broadcasted_iota(jnp.int32, logits.shape, 1)
        lg = jnp.where(lane < N_EXPERTS, logits, -jnp.inf)
        m1 = jnp.max(lg, axis=-1, keepdims=True)
        i1 = jnp.min(jnp.where(lg == m1, lane, LANES), axis=-1, keepdims=True)
        lg2 = jnp.where(lane == i1, -jnp.inf, lg)
        m2 = jnp.max(lg2, axis=-1, keepdims=True)
        i2 = jnp.min(jnp.where(lg2 == m2, lane, LANES), axis=-1, keepdims=True)
        e = jnp.exp(m2 - m1)
        g1 = 1.0 / (1.0 + e)
        g2 = e / (1.0 + e)
        route_ref[...] = jnp.where(lane == 0, i1.astype(F32),
                                   jnp.where(lane == 1, i2.astype(F32),
                                             jnp.where(lane == 2, g1, jnp.where(lane == 3, g2, 0.0))))


def _outproj(a1, a2, a3, w_out, h_res, seq, gate, norm_w, scale, shift, w_router):
    t, d = h_res.shape
    tm = min(256, t)
    with_router = w_router is not None
    k1, k2 = a1.shape[1], a2.shape[1]
    w1, w2, w3 = w_out[:k1], w_out[k1:k1 + k2], w_out[k1 + k2:]

    def rows(w):
        return pl.BlockSpec((tm, w), lambda i: (i, 0))

    def whole(a):
        return pl.BlockSpec(a.shape, lambda i: (0, 0))

    per_batch = pl.BlockSpec((None, 1, d), lambda i: ((i * tm) // seq, 0, 0))
    nw = norm_w.reshape(1, d).astype(F32)
    args = [a1, a2, a3, w1, w2, w3, h_res, gate, nw, scale, shift]
    specs = [rows(k1), rows(k2), rows(a3.shape[1]), whole(w1), whole(w2), whole(w3), rows(d), per_batch,
             whole(nw), per_batch, per_batch]
    out_shape = [jax.ShapeDtypeStruct((t, d), F32), jax.ShapeDtypeStruct((t, d), BF16)]
    out_specs = [rows(d), rows(d)]
    if with_router:
        wr = jnp.zeros((d, LANES), F32).at[:, :N_EXPERTS].set(w_router.astype(F32))
        wr_hi = wr.astype(BF16)
        wr_lo = (wr - wr_hi.astype(F32)).astype(BF16)
        args += [wr_hi, wr_lo]
        specs += [whole(wr_hi), whole(wr_lo)]
        out_shape.append(jax.ShapeDtypeStruct((t, LANES), F32))
        out_specs.append(rows(LANES))
    return pl.pallas_call(
        functools.partial(_outproj_kernel, with_router),
        grid=(t // tm,), in_specs=specs, out_specs=out_specs, out_shape=out_shape,
        compiler_params=_cparams("parallel"), name="out_proj",
    )(*args)


FFN_SUB = 256


def _ffn_kernel(row_gated, te_ref, nv_ref, x_ref, w1_ref, w3_ref, w2_ref, *rest):
    if row_gated:
        rg_ref, o_ref, acc_ref = rest
    else:
        o_ref, acc_ref = rest
    i = pl.program_id(0)
    f = pl.program_id(1)

    @pl.when(f == 0)
    def _():
        acc_ref[...] = jnp.zeros_like(acc_ref)

    @pl.when(i < nv_ref[0])
    def _():
        x = x_ref[...]
        n_sub = w1_ref.shape[1] // FFN_SUB

        def up(s):
            cols = slice(s * FFN_SUB, (s + 1) * FFN_SUB)
            return _dot(x, w1_ref[:, cols]), _dot(x, w3_ref[:, cols])

        def down(s, h):
            return _dot((_silu(h[0]) * h[1]).astype(BF16), w2_ref[s * FFN_SUB:(s + 1) * FFN_SUB, :])

        h = up(0)
        total = None
        for s in range(1, n_sub + 1):
            h_next = up(s) if s < n_sub else None
            part = down(s - 1, h)
            total = part if total is None else total + part
            h = h_next
        acc_ref[...] += total

    @pl.when(f == pl.num_programs(1) - 1)
    def _():
        y = acc_ref[...]
        if row_gated:
            y = y * rg_ref[...]
        o_ref[...] = y.astype(o_ref.dtype)


def _ffn(x, w1, w3, w2, tile_expert, n_valid, row_gate, tm, tf):
    r, d = x.shape
    ff = w1.shape[2]
    tm, tf = min(tm, r), min(tf, ff)
    in_specs = [pl.BlockSpec((tm, d), lambda i, f, te, nv: (i, 0)),
                pl.BlockSpec((None, d, tf), lambda i, f, te, nv: (te[i], 0, f)),
                pl.BlockSpec((None, d, tf), lambda i, f, te, nv: (te[i], 0, f)),
                pl.BlockSpec((None, tf, d), lambda i, f, te, nv: (te[i], f, 0))]
    args = [tile_expert, n_valid, x, w1, w3, w2]
    if row_gate is not None:
        in_specs.append(pl.BlockSpec((tm, 1), lambda i, f, te, nv: (i, 0)))
        args.append(row_gate)
    grid_spec = pltpu.PrefetchScalarGridSpec(
        num_scalar_prefetch=2, grid=(r // tm, ff // tf), in_specs=in_specs,
        out_specs=pl.BlockSpec((tm, d), lambda i, f, te, nv: (i, 0)),
        scratch_shapes=[pltpu.VMEM((tm, d), F32)])
    return pl.pallas_call(
        functools.partial(_ffn_kernel, row_gate is not None), grid_spec=grid_spec,
        out_shape=jax.ShapeDtypeStruct((r, d), BF16),
        compiler_params=_cparams("parallel", "arbitrary"), name="ffn",
    )(*args)


MOE_TILE = 512


def _moe_plan(route, tm):
    t = route.shape[0]
    e_flat = route[:, :2].astype(jnp.int32).reshape(-1)
    onehot = (e_flat[:, None] == jnp.arange(N_EXPERTS, dtype=jnp.int32)[None, :]).astype(jnp.int32)
    counts = jnp.sum(onehot, axis=0)
    rank = jnp.sum((jnp.cumsum(onehot, axis=0) - onehot) * onehot, axis=1)
    padded = ((counts + tm - 1) // tm) * tm
    ends = jnp.cumsum(padded)
    pos = (ends - padded)[e_flat] + rank
    n_rows = 2 * t + N_EXPERTS * tm
    n_tiles, n_blocks = n_rows // tm, t // tm
    tok_f = (jnp.arange(2 * t, dtype=jnp.int32) // 2).astype(F32)
    rows = jnp.stack([jnp.full((n_rows,), -1.0, F32), jnp.zeros((n_rows,), F32)], axis=1)
    rows = rows.at[pos].set(jnp.stack([tok_f, route[:, 2:4].reshape(-1)], axis=1))
    src_tok = rows[:, 0].astype(jnp.int32)
    row_gate = rows[:, 1]
    n_valid = (ends[-1] // tm).astype(jnp.int32)
    tile_start = jnp.arange(n_tiles, dtype=jnp.int32) * tm
    tile_expert = jnp.sum((tile_start[:, None] >= ends[None, :]).astype(jnp.int32), axis=1)
    last_expert = jnp.sum((tile_start[n_valid - 1] >= ends).astype(jnp.int32))
    tile_expert = jnp.where(tile_start < ends[-1], tile_expert, last_expert).astype(jnp.int32)

    blk = jnp.where(src_tok >= 0, src_tok // tm, -1).reshape(n_tiles, tm, 1)
    incidence = jnp.any(blk == jnp.arange(n_blocks, dtype=jnp.int32)[None, None, :], axis=1)
    unused = (tile_start >= ends[-1])[:, None] & (jnp.arange(n_blocks) == 0)[None, :]
    w_max = n_tiles + N_EXPERTS * n_blocks + N_EXPERTS

    def work_list(m):
        flat = m.reshape(-1)
        n = jnp.sum(flat.astype(jnp.int32))
        idx = jnp.nonzero(flat, size=w_max, fill_value=0)[0].astype(jnp.int32)
        w = jnp.arange(w_max, dtype=jnp.int32)
        valid = w < n
        idx = jnp.where(valid, idx, idx[n - 1])
        major, minor = idx // m.shape[1], idx % m.shape[1]
        first = valid & ((w == 0) | (major != jnp.roll(major, 1)))
        last = valid & ((w == n - 1) | (major != jnp.roll(major, -1)))
        return [a.astype(jnp.int32) for a in (major, minor, first, last, valid)]

    return (src_tok, row_gate.reshape(n_rows, 1), tile_expert, n_valid.reshape(1),
            work_list(incidence | unused), work_list(incidence.T))


def _dispatch_kernel(wi_ref, wj_ref, first_ref, last_ref, valid_ref, x_ref, tok_ref, o_ref):
    w = pl.program_id(0)
    tm = x_ref.shape[0]

    @pl.when(valid_ref[w] == 1)
    def _():
        col_tok = lax.broadcasted_iota(jnp.int32, (tm, tm), 1) + wj_ref[w] * tm
        onehot = jnp.where(tok_ref[...] == col_tok, 1.0, 0.0).astype(BF16)
        picked = _dot(onehot, x_ref[...]).astype(o_ref.dtype)

        @pl.when(first_ref[w] == 1)
        def _():
            o_ref[...] = picked

        @pl.when(first_ref[w] == 0)
        def _():
            o_ref[...] += picked


def _dispatch(x, src_tok, work, tm):
    t, d = x.shape
    n_rows = src_tok.shape[0]
    grid_spec = pltpu.PrefetchScalarGridSpec(
        num_scalar_prefetch=5, grid=(work[0].shape[0],),
        in_specs=[pl.BlockSpec((tm, d), lambda w, wi, wj, *_: (wj[w], 0)),
                  pl.BlockSpec((tm, 1), lambda w, wi, wj, *_: (wi[w], 0))],
        out_specs=pl.BlockSpec((tm, d), lambda w, wi, wj, *_: (wi[w], 0)))
    return pl.pallas_call(
        _dispatch_kernel, grid_spec=grid_spec,
        out_shape=jax.ShapeDtypeStruct((n_rows, d), x.dtype),
        compiler_params=_cparams("arbitrary"), name="moe_dispatch",
    )(*work, x, src_tok.reshape(n_rows, 1))


def _combine_kernel(has_mod, emit_res, vj_ref, vi_ref, first_ref, last_ref, valid_ref, y_ref, tok_ref, h_ref,
                    gate_ref, nw_ref, *rest):
    rest = list(rest)
    scale_ref = shift_ref = None
    if has_mod:
        scale_ref, shift_ref = rest.pop(0), rest.pop(0)
    hres_ref = rest.pop(0) if emit_res else None
    o_ref, acc_ref = rest
    w = pl.program_id(0)
    tm = y_ref.shape[0]

    @pl.when(valid_ref[w] == 1)
    def _():
        row_tok = lax.broadcasted_iota(jnp.int32, (tm, tm), 0) + vj_ref[w] * tm
        onehot = jnp.where(tok_ref[...] == row_tok, 1.0, 0.0).astype(BF16)
        part = _dot(onehot, y_ref[...])

        @pl.when(first_ref[w] == 1)
        def _():
            acc_ref[...] = part

        @pl.when(first_ref[w] == 0)
        def _():
            acc_ref[...] += part

        @pl.when(last_ref[w] == 1)
        def _():
            h = h_ref[...] + gate_ref[...] * acc_ref[...]
            if emit_res:
                hres_ref[...] = h
            scale = scale_ref[...] if has_mod else None
            shift = shift_ref[...] if has_mod else None
            o_ref[...] = _rms_mod(h, nw_ref[...], scale, shift).astype(o_ref.dtype)


def _combine_resnorm(ys, src_tok, work, tm, h_res, seq, gate, norm_w, scale, shift, emit_res, out_dtype):
    t, d = h_res.shape
    n_tiles = ys.shape[0] // tm
    has_mod = scale is not None

    def tok_rows(w, vj, vi, *_):
        return (vj[w], 0)

    per_batch = pl.BlockSpec((None, 1, d), lambda w, vj, *_: ((vj[w] * tm) // seq, 0, 0))
    in_specs = [pl.BlockSpec((tm, d), lambda w, vj, vi, *_: (vi[w], 0)),
                pl.BlockSpec((None, 1, tm), lambda w, vj, vi, *_: (vi[w], 0, 0)),
                pl.BlockSpec((tm, d), tok_rows), per_batch,
                pl.BlockSpec((1, d), lambda w, *_: (0, 0))]
    args = list(work) + [ys, src_tok.reshape(n_tiles, 1, tm), h_res, gate, norm_w.reshape(1, d)]
    if has_mod:
        in_specs += [per_batch, per_batch]
        args += [scale, shift]
    out_shape, out_specs = [], []
    if emit_res:
        out_shape.append(jax.ShapeDtypeStruct((t, d), F32))
        out_specs.append(pl.BlockSpec((tm, d), tok_rows))
    out_shape.append(jax.ShapeDtypeStruct((t, d), out_dtype))
    out_specs.append(pl.BlockSpec((tm, d), tok_rows))
    grid_spec = pltpu.PrefetchScalarGridSpec(
        num_scalar_prefetch=5, grid=(work[0].shape[0],), in_specs=in_specs, out_specs=out_specs,
        scratch_shapes=[pltpu.VMEM((tm, d), F32)])
    res = pl.pallas_call(
        functools.partial(_combine_kernel, has_mod, emit_res), grid_spec=grid_spec, out_shape=out_shape,
        compiler_params=_cparams("arbitrary"), name="moe_combine",
    )(*args)
    return res if emit_res else res[0]


def kernel(x, c, positions, w_ada, b_ada, norm_mix, norm_ffn, norm_final, w_in, w_out, ssm_a_re, ssm_a_im, ssm_log_dt, ssm_b_re, ssm_b_im, ssm_c_re, ssm_c_im, ssm_d, ssm_w_glu, diff_lam_q1, diff_lam_k1, diff_lam_q2, diff_lam_k2, diff_subln, gdn_conv, gdn_a_log, gdn_dt_bias, gdn_norm, ffn_w1, ffn_w3, ffn_w2, moe_router, moe_w1, moe_w3, moe_w2):
    bsz, seq, d = x.shape
    t = bsz * seq
    depth = w_in.shape[0]
    h_res = x.astype(F32).reshape(t, d)

    c_pad = jnp.zeros((SUBLANES, d), F32).at[:bsz].set(c.astype(F32))
    mod = _ada(c_pad, w_ada, b_ada)[:, :bsz]
    mods = [[m.reshape(bsz, 1, d) for m in jnp.split(mod[l], 6, axis=-1)] for l in range(depth)]
    rope_tables = _rope_tables(positions)

    hn = _resnorm(h_res, seq, None, None, norm_mix[0].astype(F32), mods[0][1], mods[0][0], False, BF16)
    out = None
    for l in range(depth):
        shift1, scale1, gate1, shift2, scale2, gate2 = mods[l]
        is_moe = l % 2 == 1
        proj = _in_proj(hn, w_in[l])

        scan_steps = max(1, math.ceil(math.log2(seq // SSM_CHUNK)))
        tables = _s5_tables(ssm_a_re[l], ssm_a_im[l], ssm_log_dt[l], ssm_b_re[l], ssm_b_im[l],
                            ssm_c_re[l], ssm_c_im[l], scan_steps)
        y_ssm = _s5_mixer(proj, bsz, seq, tables, ssm_d[l], ssm_w_glu[l].astype(BF16))

        lambda_init = 0.8 - 0.6 * math.exp(-0.3 * l)
        qk = _rope(proj, rope_tables)
        y_diff = _diff_attention(qk, proj, bsz, seq,
                                 (diff_lam_q1[l], diff_lam_k1[l], diff_lam_q2[l], diff_lam_k2[l]),
                                 diff_subln[l], lambda_init)

        qkv = _gdn_conv(proj, gdn_conv[l], seq)
        y_gdn = _gdn(qkv, proj, bsz, seq, gdn_a_log[l], gdn_dt_bias[l], gdn_norm[l])

        res = _outproj(y_ssm, y_diff, y_gdn, w_out[l].astype(BF16), h_res, seq, gate1,
                       norm_ffn[l], scale2, shift2, moe_router[l // 2] if is_moe else None)
        h_res, hn2 = res[0], res[1]

        last = l + 1 == depth
        if last:
            nxt = (norm_final.astype(F32), None, None, False, x.dtype)
        else:
            nxt = (norm_mix[l + 1].astype(F32), mods[l + 1][1], mods[l + 1][0], True, BF16)
        if is_moe:
            src_tok, row_gate, tile_expert, n_valid, work_sorted, work_token = _moe_plan(res[2], MOE_TILE)
            xs = _dispatch(hn2, src_tok, work_sorted, MOE_TILE)
            ys = _ffn(xs, moe_w1[l // 2].astype(BF16), moe_w3[l // 2].astype(BF16),
                      moe_w2[l // 2].astype(BF16), tile_expert, n_valid, row_gate, MOE_TILE, 1024)
            res = _combine_resnorm(ys, src_tok, work_token, MOE_TILE, h_res, seq, gate2, *nxt)
        else:
            n_tiles = t // min(512, t)
            ys = _ffn(hn2, ffn_w1[l // 2:l // 2 + 1].astype(BF16), ffn_w3[l // 2:l // 2 + 1].astype(BF16),
                      ffn_w2[l // 2:l // 2 + 1].astype(BF16), jnp.zeros((n_tiles,), jnp.int32),
                      jnp.full((1,), n_tiles, jnp.int32), None, 512, 512)
            res = _resnorm(h_res, seq, ys, gate2, *nxt)
        if last:
            out = res
        else:
            h_res, hn = res
    return out.reshape(bsz, seq, d)
```

```python
import functools
import math

import numpy as np
import jax
import jax.numpy as jnp
from jax import lax
from jax.experimental import pallas as pl
from jax.experimental.pallas import tpu as pltpu

F32 = jnp.float32
BF16 = jnp.bfloat16
HIGHEST = lax.Precision.HIGHEST

D_MODEL = 2048
SSM_WIDTH = 512
SSM_CH = 16
SSM_GROUPS = SSM_WIDTH // SSM_CH
SSM_STATE = 64
SSM_CHUNK = 16
DIFF_WIDTH = 768
HEAD_DIM = 128
DIFF_HEADS = DIFF_WIDTH // HEAD_DIM
DIFF_QK_DIM = HEAD_DIM // 2
GDN_WIDTH = 768
GDN_HEADS = GDN_WIDTH // HEAD_DIM
CONV_WIDTH = 4
GDN_CHUNK = 64
ROPE_THETA = 500000.0
ROPE_DIM = DIFF_QK_DIM // 4
ROPE_HALF = ROPE_DIM // 2
N_EXPERTS = 8
EPS = 1e-6
LANES = 128
SUBLANES = 8

SEG = 768
SEG_GQ, SEG_GK, SEG_GV, SEG_GZ, SEG_DQ, SEG_DK, SEG_DV, SEG_U = range(8)
PROJ_WIDTH = 8 * SEG
AB_OFF = SSM_WIDTH

VMEM_LIMIT = 56 * 1024 * 1024


def _cparams(*sem):
    return pltpu.CompilerParams(dimension_semantics=sem, vmem_limit_bytes=VMEM_LIMIT)


def _dot(a, b):
    return jnp.dot(a, b, preferred_element_type=F32)


def _dot_nt(a, b):
    return lax.dot_general(a, b, (((1,), (1,)), ((), ())), preferred_element_type=F32)


def _dot_tn(a, b):
    return lax.dot_general(a, b, (((0,), (0,)), ((), ())), preferred_element_type=F32)


def _sigmoid(x):
    return 1.0 / (1.0 + jnp.exp(-x))


def _silu(x):
    return x * _sigmoid(x)


def _ada_kernel(c_ref, w_ref, b_ref, o_ref):
    cond = _silu(c_ref[...])
    o_ref[...] = _dot(cond.astype(BF16), w_ref[...].astype(BF16)) + b_ref[...]


def _ada(c_pad, w_ada, b_ada):
    depth, d, n = w_ada.shape
    tn = 1024
    return pl.pallas_call(
        _ada_kernel,
        grid=(depth, n // tn),
        in_specs=[pl.BlockSpec((SUBLANES, d), lambda l, j: (0, 0)),
                  pl.BlockSpec((None, d, tn), lambda l, j: (l, 0, j)),
                  pl.BlockSpec((None, 1, tn), lambda l, j: (l, 0, j))],
        out_specs=pl.BlockSpec((None, SUBLANES, tn), lambda l, j: (l, 0, j)),
        out_shape=jax.ShapeDtypeStruct((depth, SUBLANES, n), F32),
        compiler_params=_cparams("arbitrary", "arbitrary"),
        name="ada",
    )(c_pad, w_ada, b_ada.reshape(depth, 1, n))


def _rms_mod(h, w, scale, shift):
    y = h * lax.rsqrt(jnp.mean(h * h, axis=-1, keepdims=True) + EPS) * w
    if scale is not None:
        y = y * (1.0 + scale) + shift
    return y


def _resnorm_kernel(has_delta, has_mod, emit_res, *refs):
    refs = list(refs)
    h = refs.pop(0)[...]
    if has_delta:
        y = refs.pop(0)[...].astype(F32)
        h = h + refs.pop(0)[...] * y
    w = refs.pop(0)[...]
    scale = shift = None
    if has_mod:
        scale = refs.pop(0)[...]
        shift = refs.pop(0)[...]
    if emit_res:
        refs.pop(0)[...] = h
    o_ref = refs.pop(0)
    o_ref[...] = _rms_mod(h, w, scale, shift).astype(o_ref.dtype)


def _resnorm(h_res, seq, delta, gate, norm_w, scale, shift, emit_res, out_dtype):
    t, d = h_res.shape
    tm = min(256, t)
    row = pl.BlockSpec((tm, d), lambda i: (i, 0))
    per_batch = pl.BlockSpec((None, 1, d), lambda i: ((i * tm) // seq, 0, 0))
    args, specs = [h_res], [row]
    if delta is not None:
        args += [delta, gate]
        specs += [row, per_batch]
    args.append(norm_w.reshape(1, d))
    specs.append(pl.BlockSpec((1, d), lambda i: (0, 0)))
    if scale is not None:
        args += [scale, shift]
        specs += [per_batch, per_batch]
    out_shape, out_specs = [], []
    if emit_res:
        out_shape.append(jax.ShapeDtypeStruct((t, d), F32))
        out_specs.append(row)
    out_shape.append(jax.ShapeDtypeStruct((t, d), out_dtype))
    out_specs.append(row)
    res = pl.pallas_call(
        functools.partial(_resnorm_kernel, delta is not None, scale is not None, emit_res),
        grid=(t // tm,), in_specs=specs, out_specs=out_specs, out_shape=out_shape,
        compiler_params=_cparams("parallel"), name="resnorm",
    )(*args)
    return res if emit_res else res[0]


IN_BLOCK = 2 * LANES


def _in_proj_blocks():
    src_segments = [(SEG_U, SSM_WIDTH), (SEG_DQ, DIFF_WIDTH), (SEG_DK, DIFF_WIDTH), (SEG_DV, DIFF_WIDTH),
                    (SEG_GQ, 3 * GDN_WIDTH), (SEG_GZ, GDN_WIDTH)]
    dest = []
    for seg, width in src_segments:
        dest += [seg * SEG // IN_BLOCK + b for b in range(width // IN_BLOCK)]
    dest.append((SEG_U * SEG + AB_OFF) // IN_BLOCK)
    return np.asarray(dest, np.int32)


def _in_proj_kernel(n_cols, dest_ref, x_ref, w_ref, o_ref, wb_ref):
    j = pl.program_id(1)

    @pl.when(pl.program_id(0) == 0)
    def _():
        col = lax.broadcasted_iota(jnp.int32, w_ref.shape, 1) + j * IN_BLOCK
        wb_ref[j] = jnp.where(col < n_cols, w_ref[...], 0.0).astype(BF16)

    o_ref[...] = _dot(x_ref[...], wb_ref[j]).astype(o_ref.dtype)


def _in_proj(x, w_in_l):
    m, k = x.shape
    n_cols = w_in_l.shape[1]
    dest = _in_proj_blocks()
    n_blocks = dest.shape[0]
    tm = min(2048, m)
    grid_spec = pltpu.PrefetchScalarGridSpec(
        num_scalar_prefetch=1, grid=(m // tm, n_blocks),
        in_specs=[pl.BlockSpec((tm, k), lambda i, j, dest: (i, 0)),
                  pl.BlockSpec((k, IN_BLOCK), lambda i, j, dest: (0, jnp.where(i == 0, j, n_blocks - 1)))],
        out_specs=pl.BlockSpec((tm, IN_BLOCK), lambda i, j, dest: (i, dest[j])),
        scratch_shapes=[pltpu.VMEM((n_blocks, k, IN_BLOCK), BF16)])
    return pl.pallas_call(
        functools.partial(_in_proj_kernel, n_cols), grid_spec=grid_spec,
        out_shape=jax.ShapeDtypeStruct((m, PROJ_WIDTH), BF16),
        compiler_params=_cparams("arbitrary", "arbitrary"), name="in_proj",
    )(jnp.asarray(dest), x, w_in_l)


def _s5_tables(a_re, a_im, log_dt, b_re, b_im, c_re, c_im, n_steps):
    L, G, P, H = SSM_CHUNK, SSM_GROUPS, SSM_STATE, SSM_CH
    lam = lax.complex(a_re.astype(F32), a_im.astype(F32))
    log_lam_bar = lam * jnp.exp(log_dt.astype(F32))[:, None]
    lam_bar = jnp.exp(log_lam_bar)
    b_bar = ((lam_bar - 1.0) / lam)[:, :, None] * lax.complex(b_re.astype(F32), b_im.astype(F32))
    c_mat = lax.complex(c_re.astype(F32), c_im.astype(F32))
    steps = jnp.arange(L + 1, dtype=F32)
    pw = jnp.exp(log_lam_bar[:, None, :] * steps[None, :, None])
    kern = jnp.real(jnp.einsum('ghp,gjp,gpi->gjih', c_mat, pw[:, :L], b_bar))
    lag = jnp.arange(L)[None, :] - jnp.arange(L)[:, None]
    tm = jnp.where((lag >= 0)[None, :, :, None, None], kern[:, jnp.clip(lag, 0, L - 1)], 0.0)
    tm = tm.transpose(0, 2, 4, 1, 3).reshape(G, L * H, L * H)
    zc = pw[:, L - 1 - jnp.arange(L)][:, :, :, None] * b_bar[:, None]
    zc = zc.transpose(0, 2, 1, 3).reshape(G, P, L * H)
    zm = jnp.concatenate([jnp.real(zc), jnp.imag(zc)], axis=1)
    cl = (c_mat[:, None] * pw[:, 1:L + 1][:, :, None, :]).reshape(G, L * H, P)
    ym = jnp.concatenate([jnp.real(cl), -jnp.imag(cl)], axis=2)
    a_pows = jnp.exp(log_lam_bar[:, None, :] * (L * 2.0 ** jnp.arange(n_steps, dtype=F32))[None, :, None])
    return (tm.astype(BF16), zm.astype(BF16), ym.astype(BF16),
            jnp.real(a_pows)[..., None], jnp.imag(a_pows)[..., None])


def _s5_core_kernel(n_steps, chunks_per_seq, u_ref, tm_ref, zm_ref, ym_ref, are_ref, aim_ref, o_ref):
    p = SSM_STATE
    n_pos, n_ch, n_lanes = u_ref.shape
    u = u_ref[...].reshape(n_pos * n_ch, n_lanes)
    z = _dot(zm_ref[...], u)
    x_re, x_im = z[:p], z[p:]
    c_idx = lax.broadcasted_iota(jnp.int32, x_re.shape, 1) % chunks_per_seq
    for k in range(n_steps):
        d = 1 << k
        a_re, a_im = are_ref[k], aim_ref[k]
        inside = c_idx >= d
        s_re = jnp.where(inside, pltpu.roll(x_re, d, axis=1), 0.0)
        s_im = jnp.where(inside, pltpu.roll(x_im, d, axis=1), 0.0)
        x_re, x_im = x_re + (a_re * s_re - a_im * s_im), x_im + (a_re * s_im + a_im * s_re)
    inside = c_idx >= 1
    x_prev = jnp.concatenate([jnp.where(inside, pltpu.roll(x_re, 1, axis=1), 0.0),
                              jnp.where(inside, pltpu.roll(x_im, 1, axis=1), 0.0)], axis=0)
    y = _dot(tm_ref[...], u) + _dot(ym_ref[...], x_prev.astype(BF16))
    o_ref[...] = y.astype(o_ref.dtype).reshape(o_ref.shape)


def _s5_core(u_t, tables, chunks_per_seq):
    n_pos, g, n_ch, r = u_t.shape
    tm, zm, ym, are, aim = tables
    p = SSM_STATE
    w = n_pos * n_ch
    n_steps = are.shape[1]

    def grp(*shape):
        return pl.BlockSpec((None,) + shape, lambda i: (i,) + (0,) * len(shape))

    slab = pl.BlockSpec((n_pos, None, n_ch, r), lambda i: (0, i, 0, 0))
    return pl.pallas_call(
        functools.partial(_s5_core_kernel, n_steps, chunks_per_seq),
        grid=(g,),
        in_specs=[slab, grp(w, w), grp(2 * p, w), grp(w, 2 * p), grp(n_steps, p, 1), grp(n_steps, p, 1)],
        out_specs=slab,
        out_shape=jax.ShapeDtypeStruct(u_t.shape, BF16),
        compiler_params=_cparams("parallel"), name="s5_core",
    )(u_t, tm, zm, ym, are, aim)


def _gelu_tanh(x):
    return 0.5 * x * (1.0 + jnp.tanh(math.sqrt(2.0 / math.pi) * (x + 0.044715 * (x * x * x))))


def _s5_post_kernel(y_ref, u_ref, d_ref, w_ref, o_ref):
    u = u_ref[:, :SSM_WIDTH].astype(F32)
    y = _gelu_tanh(y_ref[...].astype(F32) + d_ref[...] * u)
    o_ref[...] = (y * _sigmoid(_dot(y.astype(BF16), w_ref[...]))).astype(o_ref.dtype)


def _s5_post(y_core, proj, d_skip, w_glu):
    t = y_core.shape[0]
    tm = min(512, t)
    return pl.pallas_call(
        _s5_post_kernel,
        grid=(t // tm,),
        in_specs=[pl.BlockSpec((tm, SSM_WIDTH), lambda i: (i, 0)),
                  pl.BlockSpec((tm, SEG), lambda i: (i, SEG_U)),
                  pl.BlockSpec((1, SSM_WIDTH), lambda i: (0, 0)),
                  pl.BlockSpec((SSM_WIDTH, SSM_WIDTH), lambda i: (0, 0))],
        out_specs=pl.BlockSpec((tm, SSM_WIDTH), lambda i: (i, 0)),
        out_shape=jax.ShapeDtypeStruct((t, SSM_WIDTH), BF16),
        compiler_params=_cparams("parallel"), name="s5_post",
    )(y_core, proj, d_skip.reshape(1, SSM_WIDTH).astype(F32), w_glu)


def _s5_mixer(proj, bsz, seq, tables, d_skip, w_glu):
    L, G, H = SSM_CHUNK, SSM_GROUPS, SSM_CH
    t = bsz * seq
    nc = seq // L
    u = proj[:, SEG_U * SEG:SEG_U * SEG + SSM_WIDTH]
    u_t = u.reshape(t // L, L * G * H).T.reshape(L, G, H, t // L)
    y_t = _s5_core(u_t, tables, nc)
    y_core = y_t.reshape(L * G * H, t // L).T.reshape(t, SSM_WIDTH)
    return _s5_post(y_core, proj, d_skip, w_glu)


def _rope_tables(positions):
    inv_freq = ROPE_THETA ** (-jnp.arange(0, ROPE_DIM, 2, dtype=F32) / ROPE_DIM)
    ang = positions.astype(F32).reshape(-1)[:, None] * inv_freq
    cos, sin = jnp.cos(ang), jnp.sin(ang)
    r = np.arange(LANES) % DIFF_QK_DIM
    first = jnp.asarray(r < ROPE_HALF)[None, :]
    second = jnp.asarray((r >= ROPE_HALF) & (r < ROPE_DIM))[None, :]
    idx = jnp.asarray(r % ROPE_HALF)
    cos_l, sin_l = cos[:, idx], sin[:, idx]
    cosf = jnp.where(first | second, cos_l, 1.0)
    sin_a = jnp.where(first, -sin_l, 0.0)
    sin_b = jnp.where(second, sin_l, 0.0)
    return cosf, sin_a, sin_b


def _rope_kernel(x_ref, c_ref, sa_ref, sb_ref, o_ref):
    cosf, sin_a, sin_b = c_ref[...], sa_ref[...], sb_ref[...]
    n_slabs = x_ref.shape[1] // LANES
    for s in range(n_slabs):
        x = x_ref[:, s * LANES:(s + 1) * LANES].astype(F32)
        y = (x * cosf + pltpu.roll(x, LANES - ROPE_HALF, axis=1) * sin_a
             + pltpu.roll(x, ROPE_HALF, axis=1) * sin_b)
        if s < n_slabs // 2:
            y = y * (DIFF_QK_DIM ** -0.5 * math.log2(math.e))
        o_ref[:, s * LANES:(s + 1) * LANES] = y.astype(o_ref.dtype)


def _rope(proj, tables):
    t = proj.shape[0]
    tm = min(512, t)
    w = 2 * DIFF_WIDTH
    tab = pl.BlockSpec((tm, LANES), lambda i: (i, 0))
    return pl.pallas_call(
        _rope_kernel,
        grid=(t // tm,),
        in_specs=[pl.BlockSpec((tm, w), lambda i: (i, SEG_DQ // 2)), tab, tab, tab],
        out_specs=pl.BlockSpec((tm, w), lambda i: (i, 0)),
        out_shape=jax.ShapeDtypeStruct((t, w), BF16),
        compiler_params=_cparams("parallel"), name="rope",
    )(proj, *tables)


def _attn_kernel(tq, lambda_init, cast_blocks, q_ref, k_ref, vt_ref, lq1_ref, lk1_ref, lq2_ref, lk2_ref, sw_ref,
                 *rest):
    n_cast = len(cast_blocks)
    cast_in, o_ref, cast_out = rest[:n_cast], rest[n_cast], rest[n_cast + 1:]
    step = (pl.program_id(0) * pl.num_programs(1) + pl.program_id(1)) * pl.num_programs(2) + pl.program_id(2)
    for n_blk, src, dst in zip(cast_blocks, cast_in, cast_out):
        @pl.when(step < n_blk)
        def _(src=src, dst=dst):
            dst[...] = src[...].astype(dst.dtype)

    qi = pl.program_id(2)
    q = q_ref[...]
    lane = lax.broadcasted_iota(jnp.int32, q.shape, 1)
    zero = jnp.zeros_like(q)
    qm = (jnp.where(lane < DIFF_QK_DIM, q, zero), jnp.where(lane >= DIFF_QK_DIM, q, zero))

    def block(kv, masked, carry):
        start = pl.multiple_of(kv * tq, tq)
        k = k_ref[pl.ds(start, tq), :]
        v_t = vt_ref[:, pl.ds(start, tq)]
        scores = [_dot_nt(k, qm[i]) for i in range(2)]
        new = []
        for i in range(2):
            m_old, l_old, acc = carry[i]
            s = scores[i]
            if masked:
                key = lax.broadcasted_iota(jnp.int32, s.shape, 0)
                qry = lax.broadcasted_iota(jnp.int32, s.shape, 1)
                s = jnp.where(key <= qry, s, -jnp.inf)
            m_new = jnp.maximum(m_old, jnp.max(s, axis=0, keepdims=True))
            alpha = jnp.exp2(m_old - m_new)
            p = jnp.exp2(s - m_new)
            l_new = alpha * l_old + jnp.sum(p, axis=0, keepdims=True)
            new.append((m_new, l_new, alpha * acc + _dot(v_t, p.astype(BF16))))
        return tuple(new)

    init = tuple((jnp.full((1, tq), -jnp.inf, F32), jnp.zeros((1, tq), F32), jnp.zeros((HEAD_DIM, tq), F32))
                 for _ in range(2))
    carry = lax.fori_loop(0, qi, lambda kv, c: block(kv, False, c), init)
    (_, l1, acc1), (_, l2, acc2) = block(qi, True, carry)
    lam = (jnp.exp(jnp.sum(lq1_ref[...] * lk1_ref[...], axis=-1, keepdims=True))
           - jnp.exp(jnp.sum(lq2_ref[...] * lk2_ref[...], axis=-1, keepdims=True)) + lambda_init)
    o_t = acc1 / l1 - lam * (acc2 / l2)
    o_t = o_t * lax.rsqrt(jnp.mean(o_t * o_t, axis=0, keepdims=True) + EPS) * sw_ref[...] * (1.0 - lambda_init)
    o_ref[...] = o_t.T.astype(o_ref.dtype)


def _cast_blocks_per_expert(w, n_steps):
    e, r, _ = w.shape
    per = 1
    while e * per * 2 <= n_steps and r % (per * 2) == 0 and (r // (per * 2)) % (2 * SUBLANES) == 0:
        per *= 2
    return per if e * per <= n_steps and (r // per) % (2 * SUBLANES) == 0 else 0


def _diff_attention(qk, proj, bsz, seq, lam_params, subln_w, lambda_init, cast=()):
    tq = min(512, seq)
    nh = DIFF_HEADS
    nq = seq // tq
    n_steps = bsz * nh * nq
    qk3 = qk.reshape(bsz, seq, 2 * DIFF_WIDTH)
    v_t = proj[:, SEG_DV * SEG:(SEG_DV + 1) * SEG].reshape(bsz, seq, DIFF_WIDTH).transpose(0, 2, 1)
    vec = pl.BlockSpec((1, DIFF_QK_DIM), lambda b, h, i: (0, 0))
    hosted = [w for w in cast if _cast_blocks_per_expert(w, n_steps)]
    cast_specs, cast_blocks = [], []
    for w in hosted:
        per = _cast_blocks_per_expert(w, n_steps)
        n_blk = w.shape[0] * per

        def block_of(b, h, i, per=per, n_blk=n_blk):
            blk = jnp.minimum((b * nh + h) * nq + i, n_blk - 1)
            return (blk // per, blk % per, 0)

        cast_specs.append(pl.BlockSpec((None, w.shape[1] // per, w.shape[2]), block_of))
        cast_blocks.append(n_blk)
    attn_spec = pl.BlockSpec((None, tq, HEAD_DIM), lambda b, h, i: (b, i, h))
    res = pl.pallas_call(
        functools.partial(_attn_kernel, tq, lambda_init, tuple(cast_blocks)),
        grid=(bsz, nh, nq),
        in_specs=[attn_spec,
                  pl.BlockSpec((None, seq, HEAD_DIM), lambda b, h, i: (b, 0, nh + h)),
                  pl.BlockSpec((None, HEAD_DIM, seq), lambda b, h, i: (b, h, 0)),
                  vec, vec, vec, vec,
                  pl.BlockSpec((HEAD_DIM, 1), lambda b, h, i: (0, 0))] + cast_specs,
        out_specs=[attn_spec] + cast_specs,
        out_shape=[jax.ShapeDtypeStruct((bsz, seq, DIFF_WIDTH), BF16)]
                  + [jax.ShapeDtypeStruct(w.shape, BF16) for w in hosted],
        compiler_params=_cparams("arbitrary", "arbitrary", "arbitrary"), name="diff_attn",
    )(qk3, qk3, v_t, *[p.reshape(1, DIFF_QK_DIM).astype(F32) for p in lam_params],
      subln_w.reshape(HEAD_DIM, 1).astype(F32), *hosted)
    converted = iter(res[1:])
    copies = [next(converted) if _cast_blocks_per_expert(w, n_steps) else w.astype(BF16) for w in cast]
    return res[0].reshape(bsz * seq, DIFF_WIDTH), copies


HALO_ROWS = 16


def _conv_kernel(tiles_per_seq, x_ref, prev_ref, w_ref, o_ref):
    i = pl.program_id(0)
    j = pl.program_id(1)
    x = x_ref[...].astype(F32)
    prev = prev_ref[...].astype(F32)[HALO_ROWS - SUBLANES:]
    prev = jnp.where(i % tiles_per_seq == 0, jnp.zeros_like(prev), prev)
    w = w_ref[...]
    head_rows = lax.broadcasted_iota(jnp.int32, (SUBLANES, x.shape[1]), 0)
    y = x * w[CONV_WIDTH - 1:CONV_WIDTH, :]
    y_head = y[:SUBLANES]
    for back in range(1, CONV_WIDTH):
        wk = w[CONV_WIDTH - 1 - back:CONV_WIDTH - back, :]
        y = y + pltpu.roll(x, back, axis=0) * wk
        mixed = jnp.where(head_rows < back, pltpu.roll(prev, back, axis=0), pltpu.roll(x[:SUBLANES], back, axis=0))
        y_head = y_head + mixed * wk
    is_qk = j < 2 * GDN_WIDTH // x.shape[1]
    q_scale = jnp.where(j < GDN_WIDTH // x.shape[1], HEAD_DIM ** -0.5, 1.0)

    def finish(v):
        v = _silu(v)
        outs = []
        for h in range(v.shape[1] // HEAD_DIM):
            vh = v[:, h * HEAD_DIM:(h + 1) * HEAD_DIM]
            nrm = lax.rsqrt(jnp.sum(vh * vh, axis=-1, keepdims=True) + EPS) * q_scale
            outs.append(vh * jnp.where(is_qk, nrm, 1.0))
        return jnp.concatenate(outs, axis=1)

    o_ref[...] = finish(jnp.concatenate([y_head, y[SUBLANES:]], axis=0)).astype(o_ref.dtype)


def _gdn_conv(proj, conv_w, seq):
    t = proj.shape[0]
    tm = min(512, seq)
    cw = GDN_WIDTH
    nj = 3 * GDN_WIDTH // cw
    halo = tm // HALO_ROWS
    return pl.pallas_call(
        functools.partial(_conv_kernel, seq // tm),
        grid=(t // tm, nj),
        in_specs=[pl.BlockSpec((tm, cw), lambda i, j: (i, j)),
                  pl.BlockSpec((HALO_ROWS, cw), lambda i, j: (jnp.maximum(i * halo - 1, 0), j)),
                  pl.BlockSpec((CONV_WIDTH, cw), lambda i, j: (0, j))],
        out_specs=pl.BlockSpec((tm, cw), lambda i, j: (i, j)),
        out_shape=jax.ShapeDtypeStruct((t, 3 * GDN_WIDTH), BF16),
        compiler_params=_cparams("parallel", "parallel"), name="gdn_conv",
    )(proj, proj, conv_w.astype(F32))


def _softplus(x):
    return jnp.maximum(x, 0.0) + jnp.log(1.0 + jnp.exp(-jnp.abs(x)))


def _gdn_kernel(n_chunks, q_ref, k_ref, v_ref, z_ref, ab_ref, abt_ref, alog_ref, dtb_ref, alog_t_ref, dtb_t_ref,
                gw_ref, o_ref, state_ref):
    C = GDN_CHUNK
    nh = GDN_HEADS
    rows = n_chunks * C

    @pl.when(pl.program_id(1) == 0)
    def _():
        state_ref[...] = jnp.zeros_like(state_ref)

    neg_a = -jnp.exp(alog_ref[...])
    g_col = neg_a * _softplus(ab_ref[:, :LANES].astype(F32) + dtb_ref[...])
    g_row = -jnp.exp(alog_t_ref[...]) * _softplus(abt_ref[...] + dtb_t_ref[...])
    r_i = lax.broadcasted_iota(jnp.int32, (rows, rows), 0)
    c_i = lax.broadcasted_iota(jnp.int32, (rows, rows), 1)
    same = (r_i // C) == (c_i // C)
    tri_l = jnp.where(same & (c_i <= r_i), 1.0, 0.0).astype(F32)
    tri_u = jnp.where(same & (r_i <= c_i), 1.0, 0.0).astype(F32)
    gc_all = jnp.dot(tri_l, g_col, precision=HIGHEST, preferred_element_type=F32)
    gr_all = jnp.dot(g_row, tri_u, precision=HIGHEST, preferred_element_type=F32)
    b_all = _sigmoid(ab_ref[:, :LANES].astype(F32))

    incl = same & (r_i >= c_i)
    strict = same & (r_i > c_i)
    eye = jnp.where(r_i == c_i, 1.0, 0.0).astype(F32)
    gw = gw_ref[...]

    heads = range(nh)
    sls = [slice(h * HEAD_DIM, (h + 1) * HEAD_DIM) for h in heads]
    q = [q_ref[:, sl] for sl in sls]
    k = [k_ref[:, sl] for sl in sls]
    kf = [x.astype(F32) for x in k]
    gcol = [gc_all[:, h:h + 1] for h in heads]
    beta = [b_all[:, nh + h:nh + h + 1] for h in heads]
    decay = [jnp.exp(jnp.where(incl, gcol[h] - gr_all[h:h + 1, :], -jnp.inf)) for h in heads]
    pw = [jnp.where(strict, -(beta[h] * _dot_nt(k[h], k[h]) * decay[h]), 0.0) for h in heads]
    t_mat = [eye + p for p in pw]
    pw = [p.astype(BF16) for p in pw]
    for _ in range(int(math.log2(C)) - 1):
        pw = [_dot(p, p).astype(BF16) for p in pw]
        t_mat = [t + _dot(t.astype(BF16), p) for t, p in zip(t_mat, pw)]
    eg = [jnp.exp(g) for g in gcol]
    uw = [_dot(t_mat[h].astype(BF16),
               jnp.concatenate([v_ref[:, sls[h]].astype(F32) * beta[h], kf[h] * (beta[h] * eg[h])],
                               axis=1).astype(BF16)) for h in heads]
    u_all = [x[:, :HEAD_DIM] for x in uw]
    w_all = [x[:, HEAD_DIM:].astype(BF16) for x in uw]
    qk_all = [jnp.where(incl, _dot_nt(q[h], k[h]) * decay[h], 0.0).astype(BF16) for h in heads]
    qe_all = [(q[h].astype(F32) * eg[h]).astype(BF16) for h in heads]
    g_last = [[g[(c + 1) * C - 1:(c + 1) * C, :] for c in range(n_chunks)] for g in gcol]
    k_dec = [(kf[h] * jnp.exp(jnp.concatenate([jnp.broadcast_to(g, (C, 1)) for g in g_last[h]], axis=0)
                              - gcol[h])).astype(BF16) for h in heads]

    wq_all = [jnp.concatenate([x[c * C:(c + 1) * C] for c in range(n_chunks) for x in (w_all[h], qe_all[h])],
                              axis=0) for h in heads]

    state = [state_ref[h] for h in heads]
    for c in range(n_chunks):
        rs = slice(c * C, (c + 1) * C)
        sb = [s.astype(BF16) for s in state]
        ws_qs = [_dot(wq_all[h][2 * c * C:2 * (c + 1) * C], sb[h]) for h in heads]
        v_new = [(u_all[h][rs] - ws_qs[h][:C]).astype(BF16) for h in heads]
        o = [ws_qs[h][C:] + _dot(qk_all[h][rs, rs], v_new[h]) for h in heads]
        state = [state[h] * jnp.exp(g_last[h][c]) + _dot_tn(k_dec[h][rs], v_new[h]) for h in heads]
        for h in heads:
            gated = gw * _silu(z_ref[rs, sls[h]].astype(F32))
            o_h = o[h] * lax.rsqrt(jnp.mean(o[h] * o[h], axis=-1, keepdims=True) + EPS) * gated
            o_ref[rs, sls[h]] = o_h.astype(o_ref.dtype)
    for h in heads:
        state_ref[h] = state[h]


def _gdn(qkv, proj, bsz, seq, a_log, dt_bias, gnorm_w):
    t = bsz * seq
    n_chunks = min(4, seq // GDN_CHUNK)
    rows = n_chunks * GDN_CHUNK
    steps = seq // rows
    ab_off = SEG_U * SEG + AB_OFF
    abt = proj[:, ab_off:ab_off + 2 * SUBLANES].astype(F32).T

    def lane_vec(p):
        return jnp.zeros((1, LANES), F32).at[0, :GDN_HEADS].set(p.astype(F32))

    def sublane_vec(p):
        return jnp.zeros((2 * SUBLANES, 1), F32).at[:GDN_HEADS, 0].set(p.astype(F32))

    def rowblk(seg):
        return pl.BlockSpec((rows, SEG), lambda b, s: (b * steps + s, seg))

    return pl.pallas_call(
        functools.partial(_gdn_kernel, n_chunks),
        grid=(bsz, steps),
        in_specs=[rowblk(0), rowblk(1), rowblk(2),
                  pl.BlockSpec((rows, SEG), lambda b, s: (b * steps + s, SEG_GZ)),
                  pl.BlockSpec((rows, 2 * LANES), lambda b, s: (b * steps + s, ab_off // (2 * LANES))),
                  pl.BlockSpec((2 * SUBLANES, rows), lambda b, s: (0, b * steps + s)),
                  pl.BlockSpec((1, LANES), lambda b, s: (0, 0)),
                  pl.BlockSpec((1, LANES), lambda b, s: (0, 0)),
                  pl.BlockSpec((2 * SUBLANES, 1), lambda b, s: (0, 0)),
                  pl.BlockSpec((2 * SUBLANES, 1), lambda b, s: (0, 0)),
                  pl.BlockSpec((1, HEAD_DIM), lambda b, s: (0, 0))],
        out_specs=pl.BlockSpec((rows, GDN_WIDTH), lambda b, s: (b * steps + s, 0)),
        out_shape=jax.ShapeDtypeStruct((t, GDN_WIDTH), BF16),
        scratch_shapes=[pltpu.VMEM((GDN_HEADS, HEAD_DIM, HEAD_DIM), F32)],
        compiler_params=_cparams("parallel", "arbitrary"), name="gdn",
    )(qkv, qkv, qkv, proj, proj, abt, lane_vec(a_log), lane_vec(dt_bias),
      sublane_vec(a_log), sublane_vec(dt_bias), gnorm_w.reshape(1, HEAD_DIM).astype(F32))


def _outproj_kernel(with_router, a1_ref, a2_ref, a3_ref, w1_ref, w2_ref, w3_ref, h_ref, gate_ref,
                    nw_ref, scale_ref, shift_ref, *rest):
    if with_router:
        wr_hi_ref, wr_lo_ref, hres_ref, hn_ref, route_ref = rest
    else:
        hres_ref, hn_ref = rest
    mix = _dot(a1_ref[...], w1_ref[...]) + _dot(a2_ref[...], w2_ref[...]) + _dot(a3_ref[...], w3_ref[...])
    h = h_ref[...] + gate_ref[...] * mix
    hres_ref[...] = h
    hn = _rms_mod(h, nw_ref[...], scale_ref[...], shift_ref[...])
    hn_ref[...] = hn.astype(hn_ref.dtype)
    if with_router:
        hn_hi = hn.astype(BF16)
        hn_lo = (hn - hn_hi.astype(F32)).astype(BF16)
        logits = (_dot(hn_hi, wr_hi_ref[...]) + _dot(hn_lo, wr_hi_ref[...])) + _dot(hn_hi, wr_lo_ref[...])
        lane = lax.broadcasted_iota(jnp.int32, logits.shape, 1)
        lg = jnp.where(lane < N_EXPERTS, logits, -jnp.inf)
        m1 = jnp.max(lg, axis=-1, keepdims=True)
        i1 = jnp.min(jnp.where(lg == m1, lane, LANES), axis=-1, keepdims=True)
        lg2 = jnp.where(lane == i1, -jnp.inf, lg)
        m2 = jnp.max(lg2, axis=-1, keepdims=True)
        i2 = jnp.min(jnp.where(lg2 == m2, lane, LANES), axis=-1, keepdims=True)
        e = jnp.exp(m2 - m1)
        g1 = 1.0 / (1.0 + e)
        g2 = e / (1.0 + e)
        route_ref[...] = jnp.where(lane == 0, i1.astype(F32),
                                   jnp.where(lane == 1, i2.astype(F32),
                                             jnp.where(lane == 2, g1, jnp.where(lane == 3, g2, 0.0))))


def _outproj(a1, a2, a3, w_out, h_res, seq, gate, norm_w, scale, shift, w_router):
    t, d = h_res.shape
    tm = min(256, t)
    with_router = w_router is not None
    k1, k2 = a1.shape[1], a2.shape[1]
    w1, w2, w3 = w_out[:k1], w_out[k1:k1 + k2], w_out[k1 + k2:]

    def rows(w):
        return pl.BlockSpec((tm, w), lambda i: (i, 0))

    def whole(a):
        return pl.BlockSpec(a.shape, lambda i: (0, 0))

    per_batch = pl.BlockSpec((None, 1, d), lambda i: ((i * tm) // seq, 0, 0))
    nw = norm_w.reshape(1, d).astype(F32)
    args = [a1, a2, a3, w1, w2, w3, h_res, gate, nw, scale, shift]
    specs = [rows(k1), rows(k2), rows(a3.shape[1]), whole(w1), whole(w2), whole(w3), rows(d), per_batch,
             whole(nw), per_batch, per_batch]
    out_shape = [jax.ShapeDtypeStruct((t, d), F32), jax.ShapeDtypeStruct((t, d), BF16)]
    out_specs = [rows(d), rows(d)]
    if with_router:
        wr = jnp.zeros((d, LANES), F32).at[:, :N_EXPERTS].set(w_router.astype(F32))
        wr_hi = wr.astype(BF16)
        wr_lo = (wr - wr_hi.astype(F32)).astype(BF16)
        args += [wr_hi, wr_lo]
        specs += [whole(wr_hi), whole(wr_lo)]
        out_shape.append(jax.ShapeDtypeStruct((t, LANES), F32))
        out_specs.append(rows(LANES))
    return pl.pallas_call(
        functools.partial(_outproj_kernel, with_router),
        grid=(t // tm,), in_specs=specs, out_specs=out_specs, out_shape=out_shape,
        compiler_params=_cparams("parallel"), name="out_proj",
    )(*args)


FFN_SUB = 256


def _ffn_kernel(row_gated, fused_norm, has_mod, emit_res, te_ref, nv_ref, x_ref, w1_ref, w3_ref, w2_ref, *rest):
    rest = list(rest)
    rg_ref = rest.pop(0) if row_gated else None
    if fused_norm:
        h_ref, gate_ref, nw_ref = rest.pop(0), rest.pop(0), rest.pop(0)
        scale_ref, shift_ref = (rest.pop(0), rest.pop(0)) if has_mod else (None, None)
        hres_ref = rest.pop(0) if emit_res else None
    o_ref, acc_ref = rest
    i = pl.program_id(0)
    f = pl.program_id(1)

    @pl.when(f == 0)
    def _():
        acc_ref[...] = jnp.zeros_like(acc_ref)

    @pl.when(i < nv_ref[0])
    def _():
        x = x_ref[...]
        n_sub = w1_ref.shape[1] // FFN_SUB

        def up(s):
            cols = slice(s * FFN_SUB, (s + 1) * FFN_SUB)
            return _dot(x, w1_ref[:, cols]), _dot(x, w3_ref[:, cols])

        def down(s, h):
            return _dot((_silu(h[0]) * h[1]).astype(BF16), w2_ref[s * FFN_SUB:(s + 1) * FFN_SUB, :])

        h = up(0)
        total = acc_ref[...]
        for s in range(1, n_sub + 1):
            h_next = up(s) if s < n_sub else None
            total = total + down(s - 1, h)
            h = h_next
        acc_ref[...] = total

    @pl.when(f == pl.num_programs(1) - 1)
    def _():
        y = acc_ref[...]
        if row_gated:
            y = y * rg_ref[...]
        if fused_norm:
            hh = h_ref[...] + gate_ref[...] * y
            if emit_res:
                hres_ref[...] = hh
            scale = scale_ref[...] if has_mod else None
            shift = shift_ref[...] if has_mod else None
            y = _rms_mod(hh, nw_ref[...], scale, shift)
        o_ref[...] = y.astype(o_ref.dtype)


def _ffn(x, w1, w3, w2, tile_expert, n_valid, row_gate, tm, tf, norm=None):
    r, d = x.shape
    ff = w1.shape[2]
    tm, tf = min(tm, r), min(tf, ff)
    rows = pl.BlockSpec((tm, d), lambda i, f, te, nv: (i, 0))
    in_specs = [rows,
                pl.BlockSpec((None, d, tf), lambda i, f, te, nv: (te[i], 0, f)),
                pl.BlockSpec((None, d, tf), lambda i, f, te, nv: (te[i], 0, f)),
                pl.BlockSpec((None, tf, d), lambda i, f, te, nv: (te[i], f, 0))]
    args = [tile_expert, n_valid, x, w1, w3, w2]
    if row_gate is not None:
        in_specs.append(pl.BlockSpec((tm, 1), lambda i, f, te, nv: (i, 0)))
        args.append(row_gate)
    out_shape, out_specs = [], []
    has_mod = emit_res = False
    out_dtype = BF16
    if norm is not None:
        h_res, seq, gate, norm_w, scale, shift, emit_res, out_dtype = norm
        has_mod = scale is not None
        per_batch = pl.BlockSpec((None, 1, d), lambda i, f, te, nv: ((i * tm) // seq, 0, 0))
        in_specs += [rows, per_batch, pl.BlockSpec((1, d), lambda i, f, te, nv: (0, 0))]
        args += [h_res, gate, norm_w.reshape(1, d)]
        if has_mod:
            in_specs += [per_batch, per_batch]
            args += [scale, shift]
        if emit_res:
            out_shape.append(jax.ShapeDtypeStruct((r, d), F32))
            out_specs.append(rows)
    out_shape.append(jax.ShapeDtypeStruct((r, d), out_dtype))
    out_specs.append(rows)
    grid_spec = pltpu.PrefetchScalarGridSpec(
        num_scalar_prefetch=2, grid=(r // tm, ff // tf), in_specs=in_specs, out_specs=out_specs,
        scratch_shapes=[pltpu.VMEM((tm, d), F32)])
    res = pl.pallas_call(
        functools.partial(_ffn_kernel, row_gate is not None, norm is not None, has_mod, emit_res),
        grid_spec=grid_spec, out_shape=out_shape,
        compiler_params=_cparams("parallel", "arbitrary"), name="ffn",
    )(*args)
    return res if emit_res else res[0]


MOE_TILE = 512


def _moe_plan(route, tm):
    t = route.shape[0]
    e_flat = route[:, :2].astype(jnp.int32).reshape(-1)
    onehot = (e_flat[:, None] == jnp.arange(N_EXPERTS, dtype=jnp.int32)[None, :]).astype(jnp.int32)
    counts = jnp.sum(onehot, axis=0)
    rank = jnp.sum((jnp.cumsum(onehot, axis=0) - onehot) * onehot, axis=1)
    padded = ((counts + tm - 1) // tm) * tm
    ends = jnp.cumsum(padded)
    pos = (ends - padded)[e_flat] + rank
    n_rows = 2 * t + N_EXPERTS * tm
    n_tiles, n_blocks = n_rows // tm, t // tm
    tok_f = (jnp.arange(2 * t, dtype=jnp.int32) // 2).astype(F32)
    rows = jnp.stack([jnp.full((n_rows,), -1.0, F32), jnp.zeros((n_rows,), F32)], axis=1)
    rows = rows.at[pos].set(jnp.stack([tok_f, route[:, 2:4].reshape(-1)], axis=1))
    src_tok = rows[:, 0].astype(jnp.int32)
    row_gate = rows[:, 1]
    n_valid = (ends[-1] // tm).astype(jnp.int32)
    tile_start = jnp.arange(n_tiles, dtype=jnp.int32) * tm
    tile_expert = jnp.sum((tile_start[:, None] >= ends[None, :]).astype(jnp.int32), axis=1)
    last_expert = jnp.sum((tile_start[n_valid - 1] >= ends).astype(jnp.int32))
    tile_expert = jnp.where(tile_start < ends[-1], tile_expert, last_expert).astype(jnp.int32)

    blk = jnp.where(src_tok >= 0, src_tok // tm, -1).reshape(n_tiles, tm, 1)
    incidence = jnp.any(blk == jnp.arange(n_blocks, dtype=jnp.int32)[None, None, :], axis=1)
    unused = (tile_start >= ends[-1])[:, None] & (jnp.arange(n_blocks) == 0)[None, :]
    w_max = n_tiles + N_EXPERTS * n_blocks + N_EXPERTS

    def work_list(m):
        flat = m.reshape(-1)
        n = jnp.sum(flat.astype(jnp.int32))
        idx = jnp.nonzero(flat, size=w_max, fill_value=0)[0].astype(jnp.int32)
        w = jnp.arange(w_max, dtype=jnp.int32)
        valid = w < n
        idx = jnp.where(valid, idx, idx[n - 1])
        major, minor = idx // m.shape[1], idx % m.shape[1]
        first = valid & ((w == 0) | (major != jnp.roll(major, 1)))
        last = valid & ((w == n - 1) | (major != jnp.roll(major, -1)))
        return [a.astype(jnp.int32) for a in (major, minor, first, last, valid)]

    return (src_tok, row_gate.reshape(n_rows, 1), tile_expert, n_valid.reshape(1),
            work_list(incidence | unused), work_list(incidence.T))


def _dispatch_kernel(wi_ref, wj_ref, first_ref, last_ref, valid_ref, x_ref, tok_ref, o_ref):
    w = pl.program_id(0)
    tm = x_ref.shape[0]

    @pl.when(valid_ref[w] == 1)
    def _():
        col_tok = lax.broadcasted_iota(jnp.int32, (tm, tm), 1) + wj_ref[w] * tm
        onehot = jnp.where(tok_ref[...] == col_tok, 1.0, 0.0).astype(BF16)
        picked = _dot(onehot, x_ref[...]).astype(o_ref.dtype)

        @pl.when(first_ref[w] == 1)
        def _():
            o_ref[...] = picked

        @pl.when(first_ref[w] == 0)
        def _():
            o_ref[...] += picked


def _dispatch(x, src_tok, work, tm):
    t, d = x.shape
    n_rows = src_tok.shape[0]
    grid_spec = pltpu.PrefetchScalarGridSpec(
        num_scalar_prefetch=5, grid=(work[0].shape[0],),
        in_specs=[pl.BlockSpec((tm, d), lambda w, wi, wj, *_: (wj[w], 0)),
                  pl.BlockSpec((tm, 1), lambda w, wi, wj, *_: (wi[w], 0))],
        out_specs=pl.BlockSpec((tm, d), lambda w, wi, wj, *_: (wi[w], 0)))
    return pl.pallas_call(
        _dispatch_kernel, grid_spec=grid_spec,
        out_shape=jax.ShapeDtypeStruct((n_rows, d), x.dtype),
        compiler_params=_cparams("arbitrary"), name="moe_dispatch",
    )(*work, x, src_tok.reshape(n_rows, 1))


def _combine_kernel(has_mod, emit_res, vj_ref, vi_ref, first_ref, last_ref, valid_ref, y_ref, tok_ref, h_ref,
                    gate_ref, nw_ref, *rest):
    rest = list(rest)
    scale_ref = shift_ref = None
    if has_mod:
        scale_ref, shift_ref = rest.pop(0), rest.pop(0)
    hres_ref = rest.pop(0) if emit_res else None
    o_ref, acc_ref = rest
    w = pl.program_id(0)
    tm = y_ref.shape[0]

    @pl.when(valid_ref[w] == 1)
    def _():
        row_tok = lax.broadcasted_iota(jnp.int32, (tm, tm), 0) + vj_ref[w] * tm
        onehot = jnp.where(tok_ref[...] == row_tok, 1.0, 0.0).astype(BF16)
        part = _dot(onehot, y_ref[...])

        @pl.when(first_ref[w] == 1)
        def _():
            acc_ref[...] = part

        @pl.when(first_ref[w] == 0)
        def _():
            acc_ref[...] += part

        @pl.when(last_ref[w] == 1)
        def _():
            h = h_ref[...] + gate_ref[...] * acc_ref[...]
            if emit_res:
                hres_ref[...] = h
            scale = scale_ref[...] if has_mod else None
            shift = shift_ref[...] if has_mod else None
            o_ref[...] = _rms_mod(h, nw_ref[...], scale, shift).astype(o_ref.dtype)


def _combine_resnorm(ys, src_tok, work, tm, h_res, seq, gate, norm_w, scale, shift, emit_res, out_dtype):
    t, d = h_res.shape
    n_tiles = ys.shape[0] // tm
    has_mod = scale is not None

    def tok_rows(w, vj, vi, *_):
        return (vj[w], 0)

    per_batch = pl.BlockSpec((None, 1, d), lambda w, vj, *_: ((vj[w] * tm) // seq, 0, 0))
    in_specs = [pl.BlockSpec((tm, d), lambda w, vj, vi, *_: (vi[w], 0)),
                pl.BlockSpec((None, 1, tm), lambda w, vj, vi, *_: (vi[w], 0, 0)),
                pl.BlockSpec((tm, d), tok_rows), per_batch,
                pl.BlockSpec((1, d), lambda w, *_: (0, 0))]
    args = list(work) + [ys, src_tok.reshape(n_tiles, 1, tm), h_res, gate, norm_w.reshape(1, d)]
    if has_mod:
        in_specs += [per_batch, per_batch]
        args += [scale, shift]
    out_shape, out_specs = [], []
    if emit_res:
        out_shape.append(jax.ShapeDtypeStruct((t, d), F32))
        out_specs.append(pl.BlockSpec((tm, d), tok_rows))
    out_shape.append(jax.ShapeDtypeStruct((t, d), out_dtype))
    out_specs.append(pl.BlockSpec((tm, d), tok_rows))
    grid_spec = pltpu.PrefetchScalarGridSpec(
        num_scalar_prefetch=5, grid=(work[0].shape[0],), in_specs=in_specs, out_specs=out_specs,
        scratch_shapes=[pltpu.VMEM((tm, d), F32)])
    res = pl.pallas_call(
        functools.partial(_combine_kernel, has_mod, emit_res), grid_spec=grid_spec, out_shape=out_shape,
        compiler_params=_cparams("arbitrary"), name="moe_combine",
    )(*args)
    return res if emit_res else res[0]


def kernel(x, c, positions, w_ada, b_ada, norm_mix, norm_ffn, norm_final, w_in, w_out, ssm_a_re, ssm_a_im, ssm_log_dt, ssm_b_re, ssm_b_im, ssm_c_re, ssm_c_im, ssm_d, ssm_w_glu, diff_lam_q1, diff_lam_k1, diff_lam_q2, diff_lam_k2, diff_subln, gdn_conv, gdn_a_log, gdn_dt_bias, gdn_norm, ffn_w1, ffn_w3, ffn_w2, moe_router, moe_w1, moe_w3, moe_w2):
    bsz, seq, d = x.shape
    t = bsz * seq
    depth = w_in.shape[0]
    h_res = x.astype(F32).reshape(t, d)

    c_pad = jnp.zeros((SUBLANES, d), F32).at[:bsz].set(c.astype(F32))
    mod = _ada(c_pad, w_ada, b_ada)[:, :bsz]
    mods = [[m.reshape(bsz, 1, d) for m in jnp.split(mod[l], 6, axis=-1)] for l in range(depth)]
    rope_tables = _rope_tables(positions)

    hn = _resnorm(h_res, seq, None, None, norm_mix[0].astype(F32), mods[0][1], mods[0][0], False, BF16)
    out = None
    moe_bf16 = []
    for l in range(depth):
        shift1, scale1, gate1, shift2, scale2, gate2 = mods[l]
        is_moe = l % 2 == 1
        proj = _in_proj(hn, w_in[l])

        scan_steps = max(1, math.ceil(math.log2(seq // SSM_CHUNK)))
        tables = _s5_tables(ssm_a_re[l], ssm_a_im[l], ssm_log_dt[l], ssm_b_re[l], ssm_b_im[l],
                            ssm_c_re[l], ssm_c_im[l], scan_steps)
        y_ssm = _s5_mixer(proj, bsz, seq, tables, ssm_d[l], ssm_w_glu[l].astype(BF16))

        lambda_init = 0.8 - 0.6 * math.exp(-0.3 * l)
        qk = _rope(proj, rope_tables)
        if is_moe:
            to_cast = (moe_w2[l // 2],)
        elif l + 1 < depth:
            to_cast = (moe_w1[(l + 1) // 2], moe_w3[(l + 1) // 2])
        else:
            to_cast = ()
        y_diff, copies = _diff_attention(qk, proj, bsz, seq,
                                         (diff_lam_q1[l], diff_lam_k1[l], diff_lam_q2[l], diff_lam_k2[l]),
                                         diff_subln[l], lambda_init, to_cast)
        if is_moe:
            moe_bf16 = moe_bf16 + copies
        elif copies:
            moe_bf16 = copies

        qkv = _gdn_conv(proj, gdn_conv[l], seq)
        y_gdn = _gdn(qkv, proj, bsz, seq, gdn_a_log[l], gdn_dt_bias[l], gdn_norm[l])

        res = _outproj(y_ssm, y_diff, y_gdn, w_out[l].astype(BF16), h_res, seq, gate1,
                       norm_ffn[l], scale2, shift2, moe_router[l // 2] if is_moe else None)
        h_res, hn2 = res[0], res[1]

        last = l + 1 == depth
        if last:
            nxt = (norm_final.astype(F32), None, None, False, x.dtype)
        else:
            nxt = (norm_mix[l + 1].astype(F32), mods[l + 1][1], mods[l + 1][0], True, BF16)
        if is_moe:
            src_tok, row_gate, tile_expert, n_valid, work_sorted, work_token = _moe_plan(res[2], MOE_TILE)
            xs = _dispatch(hn2, src_tok, work_sorted, MOE_TILE)
            ys = _ffn(xs, *moe_bf16, tile_expert, n_valid, row_gate, MOE_TILE, 1024)
            res = _combine_resnorm(ys, src_tok, work_token, MOE_TILE, h_res, seq, gate2, *nxt)
        else:
            n_tiles = t // min(512, t)
            res = _ffn(hn2, ffn_w1[l // 2:l // 2 + 1].astype(BF16), ffn_w3[l // 2:l // 2 + 1].astype(BF16),
                       ffn_w2[l // 2:l // 2 + 1].astype(BF16), jnp.zeros((n_tiles,), jnp.int32),
                       jnp.full((1,), n_tiles, jnp.int32), None, 512, 512,
                       norm=(h_res, seq, gate2) + nxt)
        if last:
            out = res
        else:
            h_res, hn = res
    return out.reshape(bsz, seq, d)
```

```python
import functools
import math

import numpy as np
import jax
import jax.numpy as jnp
from jax import lax
from jax.experimental import pallas as pl
from jax.experimental.pallas import tpu as pltpu

F32 = jnp.float32
BF16 = jnp.bfloat16
HIGHEST = lax.Precision.HIGHEST

D_MODEL = 2048
SSM_WIDTH = 512
SSM_CH = 16
SSM_GROUPS = SSM_WIDTH // SSM_CH
SSM_STATE = 64
SSM_CHUNK = 16
DIFF_WIDTH = 768
HEAD_DIM = 128
DIFF_HEADS = DIFF_WIDTH // HEAD_DIM
DIFF_QK_DIM = HEAD_DIM // 2
GDN_WIDTH = 768
GDN_HEADS = GDN_WIDTH // HEAD_DIM
CONV_WIDTH = 4
GDN_CHUNK = 64
ROPE_THETA = 500000.0
ROPE_DIM = DIFF_QK_DIM // 4
ROPE_HALF = ROPE_DIM // 2
N_EXPERTS = 8
EPS = 1e-6
LANES = 128
SUBLANES = 8

SEG = 768
SEG_GQ, SEG_GK, SEG_GV, SEG_GZ, SEG_DQ, SEG_DK, SEG_DV, SEG_U = range(8)
PROJ_WIDTH = 8 * SEG
AB_OFF = SSM_WIDTH

VMEM_LIMIT = 56 * 1024 * 1024


def _cparams(*sem):
    return pltpu.CompilerParams(dimension_semantics=sem, vmem_limit_bytes=VMEM_LIMIT)


def _dot(a, b):
    return jnp.dot(a, b, preferred_element_type=F32)


def _dot_nt(a, b):
    return lax.dot_general(a, b, (((1,), (1,)), ((), ())), preferred_element_type=F32)


def _dot_tn(a, b):
    return lax.dot_general(a, b, (((0,), (0,)), ((), ())), preferred_element_type=F32)


def _sigmoid(x):
    return 1.0 / (1.0 + jnp.exp(-x))


def _silu(x):
    return x * _sigmoid(x)


def _ada_kernel(c_ref, w_ref, b_ref, o_ref):
    cond = _silu(c_ref[...])
    o_ref[...] = _dot(cond.astype(BF16), w_ref[...].astype(BF16)) + b_ref[...]


def _ada(c_pad, w_ada, b_ada):
    depth, d, n = w_ada.shape
    tn = 1024
    return pl.pallas_call(
        _ada_kernel,
        grid=(depth, n // tn),
        in_specs=[pl.BlockSpec((SUBLANES, d), lambda l, j: (0, 0)),
                  pl.BlockSpec((None, d, tn), lambda l, j: (l, 0, j)),
                  pl.BlockSpec((None, 1, tn), lambda l, j: (l, 0, j))],
        out_specs=pl.BlockSpec((None, SUBLANES, tn), lambda l, j: (l, 0, j)),
        out_shape=jax.ShapeDtypeStruct((depth, SUBLANES, n), F32),
        compiler_params=_cparams("arbitrary", "arbitrary"),
        name="ada",
    )(c_pad, w_ada, b_ada.reshape(depth, 1, n))


def _rms_mod(h, w, scale, shift):
    y = h * lax.rsqrt(jnp.mean(h * h, axis=-1, keepdims=True) + EPS) * w
    if scale is not None:
        y = y * (1.0 + scale) + shift
    return y


def _resnorm_kernel(has_delta, has_mod, emit_res, *refs):
    refs = list(refs)
    h = refs.pop(0)[...]
    if has_delta:
        y = refs.pop(0)[...].astype(F32)
        h = h + refs.pop(0)[...] * y
    w = refs.pop(0)[...]
    scale = shift = None
    if has_mod:
        scale = refs.pop(0)[...]
        shift = refs.pop(0)[...]
    if emit_res:
        refs.pop(0)[...] = h
    o_ref = refs.pop(0)
    o_ref[...] = _rms_mod(h, w, scale, shift).astype(o_ref.dtype)


def _resnorm(h_res, seq, delta, gate, norm_w, scale, shift, emit_res, out_dtype):
    t, d = h_res.shape
    tm = min(256, t)
    row = pl.BlockSpec((tm, d), lambda i: (i, 0))
    per_batch = pl.BlockSpec((None, 1, d), lambda i: ((i * tm) // seq, 0, 0))
    args, specs = [h_res], [row]
    if delta is not None:
        args += [delta, gate]
        specs += [row, per_batch]
    args.append(norm_w.reshape(1, d))
    specs.append(pl.BlockSpec((1, d), lambda i: (0, 0)))
    if scale is not None:
        args += [scale, shift]
        specs += [per_batch, per_batch]
    out_shape, out_specs = [], []
    if emit_res:
        out_shape.append(jax.ShapeDtypeStruct((t, d), F32))
        out_specs.append(row)
    out_shape.append(jax.ShapeDtypeStruct((t, d), out_dtype))
    out_specs.append(row)
    res = pl.pallas_call(
        functools.partial(_resnorm_kernel, delta is not None, scale is not None, emit_res),
        grid=(t // tm,), in_specs=specs, out_specs=out_specs, out_shape=out_shape,
        compiler_params=_cparams("parallel"), name="resnorm",
    )(*args)
    return res if emit_res else res[0]


IN_BLOCK = 2 * LANES


def _in_proj_blocks():
    src_segments = [(SEG_U, SSM_WIDTH), (SEG_DQ, DIFF_WIDTH), (SEG_DK, DIFF_WIDTH), (SEG_DV, DIFF_WIDTH),
                    (SEG_GQ, 3 * GDN_WIDTH), (SEG_GZ, GDN_WIDTH)]
    dest = []
    for seg, width in src_segments:
        dest += [seg * SEG // IN_BLOCK + b for b in range(width // IN_BLOCK)]
    dest.append((SEG_U * SEG + AB_OFF) // IN_BLOCK)
    return np.asarray(dest, np.int32)


def _in_proj_kernel(n_cols, dest_ref, x_ref, w_ref, o_ref, wb_ref):
    j = pl.program_id(1)

    @pl.when(pl.program_id(0) == 0)
    def _():
        col = lax.broadcasted_iota(jnp.int32, w_ref.shape, 1) + j * IN_BLOCK
        wb_ref[j] = jnp.where(col < n_cols, w_ref[...], 0.0).astype(BF16)

    o_ref[...] = _dot(x_ref[...], wb_ref[j]).astype(o_ref.dtype)


def _in_proj(x, w_in_l):
    m, k = x.shape
    n_cols = w_in_l.shape[1]
    dest = _in_proj_blocks()
    n_blocks = dest.shape[0]
    tm = min(2048, m)
    grid_spec = pltpu.PrefetchScalarGridSpec(
        num_scalar_prefetch=1, grid=(m // tm, n_blocks),
        in_specs=[pl.BlockSpec((tm, k), lambda i, j, dest: (i, 0)),
                  pl.BlockSpec((k, IN_BLOCK), lambda i, j, dest: (0, jnp.where(i == 0, j, n_blocks - 1)))],
        out_specs=pl.BlockSpec((tm, IN_BLOCK), lambda i, j, dest: (i, dest[j])),
        scratch_shapes=[pltpu.VMEM((n_blocks, k, IN_BLOCK), BF16)])
    return pl.pallas_call(
        functools.partial(_in_proj_kernel, n_cols), grid_spec=grid_spec,
        out_shape=jax.ShapeDtypeStruct((m, PROJ_WIDTH), BF16),
        compiler_params=_cparams("arbitrary", "arbitrary"), name="in_proj",
    )(jnp.asarray(dest), x, w_in_l)


def _s5_tables(a_re, a_im, log_dt, b_re, b_im, c_re, c_im, n_steps):
    L, G, P, H = SSM_CHUNK, SSM_GROUPS, SSM_STATE, SSM_CH
    lam = lax.complex(a_re.astype(F32), a_im.astype(F32))
    log_lam_bar = lam * jnp.exp(log_dt.astype(F32))[:, None]
    lam_bar = jnp.exp(log_lam_bar)
    b_bar = ((lam_bar - 1.0) / lam)[:, :, None] * lax.complex(b_re.astype(F32), b_im.astype(F32))
    c_mat = lax.complex(c_re.astype(F32), c_im.astype(F32))
    steps = jnp.arange(L + 1, dtype=F32)
    pw = jnp.exp(log_lam_bar[:, None, :] * steps[None, :, None])
    kern = jnp.real(jnp.einsum('ghp,gjp,gpi->gjih', c_mat, pw[:, :L], b_bar))
    lag = jnp.arange(L)[None, :] - jnp.arange(L)[:, None]
    tm = jnp.where((lag >= 0)[None, :, :, None, None], kern[:, jnp.clip(lag, 0, L - 1)], 0.0)
    tm = tm.transpose(0, 2, 4, 1, 3).reshape(G, L * H, L * H)
    zc = pw[:, L - 1 - jnp.arange(L)][:, :, :, None] * b_bar[:, None]
    zc = zc.transpose(0, 2, 1, 3).reshape(G, P, L * H)
    zm = jnp.concatenate([jnp.real(zc), jnp.imag(zc)], axis=1)
    cl = (c_mat[:, None] * pw[:, 1:L + 1][:, :, None, :]).reshape(G, L * H, P)
    ym = jnp.concatenate([jnp.real(cl), -jnp.imag(cl)], axis=2)
    a_pows = jnp.exp(log_lam_bar[:, None, :] * (L * 2.0 ** jnp.arange(n_steps, dtype=F32))[None, :, None])
    return (tm.astype(BF16), zm.astype(BF16), ym.astype(BF16),
            jnp.real(a_pows)[..., None], jnp.imag(a_pows)[..., None])


def _s5_core_kernel(n_steps, chunks_per_seq, u_ref, tm_ref, zm_ref, ym_ref, are_ref, aim_ref, o_ref):
    p = SSM_STATE
    n_pos, n_ch, n_lanes = u_ref.shape
    u = u_ref[...].reshape(n_pos * n_ch, n_lanes)
    z = _dot(zm_ref[...], u)
    x_re, x_im = z[:p], z[p:]
    c_idx = lax.broadcasted_iota(jnp.int32, x_re.shape, 1) % chunks_per_seq
    for k in range(n_steps):
        d = 1 << k
        a_re, a_im = are_ref[k], aim_ref[k]
        inside = c_idx >= d
        s_re = jnp.where(inside, pltpu.roll(x_re, d, axis=1), 0.0)
        s_im = jnp.where(inside, pltpu.roll(x_im, d, axis=1), 0.0)
        x_re, x_im = x_re + (a_re * s_re - a_im * s_im), x_im + (a_re * s_im + a_im * s_re)
    inside = c_idx >= 1
    x_prev = jnp.concatenate([jnp.where(inside, pltpu.roll(x_re, 1, axis=1), 0.0),
                              jnp.where(inside, pltpu.roll(x_im, 1, axis=1), 0.0)], axis=0)
    y = _dot(tm_ref[...], u) + _dot(ym_ref[...], x_prev.astype(BF16))
    o_ref[...] = y.astype(o_ref.dtype).reshape(o_ref.shape)


def _s5_core(u_t, tables, chunks_per_seq):
    n_pos, g, n_ch, r = u_t.shape
    tm, zm, ym, are, aim = tables
    p = SSM_STATE
    w = n_pos * n_ch
    n_steps = are.shape[1]

    def grp(*shape):
        return pl.BlockSpec((None,) + shape, lambda i: (i,) + (0,) * len(shape))

    slab = pl.BlockSpec((n_pos, None, n_ch, r), lambda i: (0, i, 0, 0))
    return pl.pallas_call(
        functools.partial(_s5_core_kernel, n_steps, chunks_per_seq),
        grid=(g,),
        in_specs=[slab, grp(w, w), grp(2 * p, w), grp(w, 2 * p), grp(n_steps, p, 1), grp(n_steps, p, 1)],
        out_specs=slab,
        out_shape=jax.ShapeDtypeStruct(u_t.shape, BF16),
        compiler_params=_cparams("parallel"), name="s5_core",
    )(u_t, tm, zm, ym, are, aim)


def _gelu_tanh(x):
    return 0.5 * x * (1.0 + jnp.tanh(math.sqrt(2.0 / math.pi) * (x + 0.044715 * (x * x * x))))


def _s5_post_kernel(y_ref, u_ref, d_ref, w_ref, o_ref):
    u = u_ref[:, :SSM_WIDTH].astype(F32)
    y = _gelu_tanh(y_ref[...].astype(F32) + d_ref[...] * u)
    o_ref[...] = (y * _sigmoid(_dot(y.astype(BF16), w_ref[...]))).astype(o_ref.dtype)


def _s5_post(y_core, proj, d_skip, w_glu):
    t = y_core.shape[0]
    tm = min(512, t)
    return pl.pallas_call(
        _s5_post_kernel,
        grid=(t // tm,),
        in_specs=[pl.BlockSpec((tm, SSM_WIDTH), lambda i: (i, 0)),
                  pl.BlockSpec((tm, SEG), lambda i: (i, SEG_U)),
                  pl.BlockSpec((1, SSM_WIDTH), lambda i: (0, 0)),
                  pl.BlockSpec((SSM_WIDTH, SSM_WIDTH), lambda i: (0, 0))],
        out_specs=pl.BlockSpec((tm, SSM_WIDTH), lambda i: (i, 0)),
        out_shape=jax.ShapeDtypeStruct((t, SSM_WIDTH), BF16),
        compiler_params=_cparams("parallel"), name="s5_post",
    )(y_core, proj, d_skip.reshape(1, SSM_WIDTH).astype(F32), w_glu)


def _s5_mixer(proj, bsz, seq, tables, d_skip, w_glu):
    L, G, H = SSM_CHUNK, SSM_GROUPS, SSM_CH
    t = bsz * seq
    nc = seq // L
    u = proj[:, SEG_U * SEG:SEG_U * SEG + SSM_WIDTH]
    u_t = u.reshape(t // L, L * G * H).T.reshape(L, G, H, t // L)
    y_t = _s5_core(u_t, tables, nc)
    y_core = y_t.reshape(L * G * H, t // L).T.reshape(t, SSM_WIDTH)
    return _s5_post(y_core, proj, d_skip, w_glu)


def _rope_tables(positions):
    inv_freq = ROPE_THETA ** (-jnp.arange(0, ROPE_DIM, 2, dtype=F32) / ROPE_DIM)
    ang = positions.astype(F32).reshape(-1)[:, None] * inv_freq
    cos, sin = jnp.cos(ang), jnp.sin(ang)
    r = np.arange(LANES) % DIFF_QK_DIM
    first = jnp.asarray(r < ROPE_HALF)[None, :]
    second = jnp.asarray((r >= ROPE_HALF) & (r < ROPE_DIM))[None, :]
    idx = jnp.asarray(r % ROPE_HALF)
    cos_l, sin_l = cos[:, idx], sin[:, idx]
    cosf = jnp.where(first | second, cos_l, 1.0)
    sin_a = jnp.where(first, -sin_l, 0.0)
    sin_b = jnp.where(second, sin_l, 0.0)
    return cosf, sin_a, sin_b


def _rope_kernel(x_ref, c_ref, sa_ref, sb_ref, o_ref):
    cosf, sin_a, sin_b = c_ref[...], sa_ref[...], sb_ref[...]
    n_slabs = x_ref.shape[1] // LANES
    for s in range(n_slabs):
        x = x_ref[:, s * LANES:(s + 1) * LANES].astype(F32)
        y = (x * cosf + pltpu.roll(x, LANES - ROPE_HALF, axis=1) * sin_a
             + pltpu.roll(x, ROPE_HALF, axis=1) * sin_b)
        if s < n_slabs // 2:
            y = y * (DIFF_QK_DIM ** -0.5 * math.log2(math.e))
        o_ref[:, s * LANES:(s + 1) * LANES] = y.astype(o_ref.dtype)


def _rope(proj, tables):
    t = proj.shape[0]
    tm = min(512, t)
    w = 2 * DIFF_WIDTH
    tab = pl.BlockSpec((tm, LANES), lambda i: (i, 0))
    return pl.pallas_call(
        _rope_kernel,
        grid=(t // tm,),
        in_specs=[pl.BlockSpec((tm, w), lambda i: (i, SEG_DQ // 2)), tab, tab, tab],
        out_specs=pl.BlockSpec((tm, w), lambda i: (i, 0)),
        out_shape=jax.ShapeDtypeStruct((t, w), BF16),
        compiler_params=_cparams("parallel"), name="rope",
    )(proj, *tables)


def _attn_kernel(tq, lambda_init, cast_blocks, q_ref, k_ref, vt_ref, lq1_ref, lk1_ref, lq2_ref, lk2_ref, sw_ref,
                 *rest):
    n_cast = len(cast_blocks)
    cast_in, o_ref, cast_out = rest[:n_cast], rest[n_cast], rest[n_cast + 1:]
    step = (pl.program_id(0) * pl.num_programs(1) + pl.program_id(1)) * pl.num_programs(2) + pl.program_id(2)
    for n_blk, src, dst in zip(cast_blocks, cast_in, cast_out):
        @pl.when(step < n_blk)
        def _(src=src, dst=dst):
            dst[...] = src[...].astype(dst.dtype)

    qi = pl.program_id(2)
    q = q_ref[...]
    lane = lax.broadcasted_iota(jnp.int32, q.shape, 1)
    zero = jnp.zeros_like(q)
    qm = (jnp.where(lane < DIFF_QK_DIM, q, zero), jnp.where(lane >= DIFF_QK_DIM, q, zero))

    def block(kv, masked, carry):
        start = pl.multiple_of(kv * tq, tq)
        k = k_ref[pl.ds(start, tq), :]
        v_t = vt_ref[:, pl.ds(start, tq)]
        scores = [_dot_nt(k, qm[i]) for i in range(2)]
        new = []
        for i in range(2):
            m_old, l_old, acc = carry[i]
            s = scores[i]
            if masked:
                key = lax.broadcasted_iota(jnp.int32, s.shape, 0)
                qry = lax.broadcasted_iota(jnp.int32, s.shape, 1)
                s = jnp.where(key <= qry, s, -jnp.inf)
            m_new = jnp.maximum(m_old, jnp.max(s, axis=0, keepdims=True))
            alpha = jnp.exp2(m_old - m_new)
            p = jnp.exp2(s - m_new)
            l_new = alpha * l_old + jnp.sum(p, axis=0, keepdims=True)
            new.append((m_new, l_new, alpha * acc + _dot(v_t, p.astype(BF16))))
        return tuple(new)

    init = tuple((jnp.full((1, tq), -jnp.inf, F32), jnp.zeros((1, tq), F32), jnp.zeros((HEAD_DIM, tq), F32))
                 for _ in range(2))
    carry = lax.fori_loop(0, qi, lambda kv, c: block(kv, False, c), init)
    (_, l1, acc1), (_, l2, acc2) = block(qi, True, carry)
    lam = (jnp.exp(jnp.sum(lq1_ref[...] * lk1_ref[...], axis=-1, keepdims=True))
           - jnp.exp(jnp.sum(lq2_ref[...] * lk2_ref[...], axis=-1, keepdims=True)) + lambda_init)
    o_t = acc1 / l1 - lam * (acc2 / l2)
    o_t = o_t * lax.rsqrt(jnp.mean(o_t * o_t, axis=0, keepdims=True) + EPS) * sw_ref[...] * (1.0 - lambda_init)
    o_ref[...] = o_t.T.astype(o_ref.dtype)


def _cast_blocks_per_expert(w, n_steps):
    e, r, _ = w.shape
    per = 1
    while e * per * 2 <= n_steps and r % (per * 2) == 0 and (r // (per * 2)) % (2 * SUBLANES) == 0:
        per *= 2
    return per if e * per <= n_steps and (r // per) % (2 * SUBLANES) == 0 else 0


def _diff_attention(qk, proj, bsz, seq, lam_params, subln_w, lambda_init, cast=()):
    tq = min(512, seq)
    nh = DIFF_HEADS
    nq = seq // tq
    n_steps = bsz * nh * nq
    qk3 = qk.reshape(bsz, seq, 2 * DIFF_WIDTH)
    v_t = proj[:, SEG_DV * SEG:(SEG_DV + 1) * SEG].reshape(bsz, seq, DIFF_WIDTH).transpose(0, 2, 1)
    vec = pl.BlockSpec((1, DIFF_QK_DIM), lambda b, h, i: (0, 0))
    hosted = [w for w in cast if _cast_blocks_per_expert(w, n_steps)]
    cast_specs, cast_blocks = [], []
    for w in hosted:
        per = _cast_blocks_per_expert(w, n_steps)
        n_blk = w.shape[0] * per

        def block_of(b, h, i, per=per, n_blk=n_blk):
            blk = jnp.minimum((b * nh + h) * nq + i, n_blk - 1)
            return (blk // per, blk % per, 0)

        cast_specs.append(pl.BlockSpec((None, w.shape[1] // per, w.shape[2]), block_of))
        cast_blocks.append(n_blk)
    attn_spec = pl.BlockSpec((None, tq, HEAD_DIM), lambda b, h, i: (b, i, h))
    res = pl.pallas_call(
        functools.partial(_attn_kernel, tq, lambda_init, tuple(cast_blocks)),
        grid=(bsz, nh, nq),
        in_specs=[attn_spec,
                  pl.BlockSpec((None, seq, HEAD_DIM), lambda b, h, i: (b, 0, nh + h)),
                  pl.BlockSpec((None, HEAD_DIM, seq), lambda b, h, i: (b, h, 0)),
                  vec, vec, vec, vec,
                  pl.BlockSpec((HEAD_DIM, 1), lambda b, h, i: (0, 0))] + cast_specs,
        out_specs=[attn_spec] + cast_specs,
        out_shape=[jax.ShapeDtypeStruct((bsz, seq, DIFF_WIDTH), BF16)]
                  + [jax.ShapeDtypeStruct(w.shape, BF16) for w in hosted],
        compiler_params=_cparams("arbitrary", "arbitrary", "arbitrary"), name="diff_attn",
    )(qk3, qk3, v_t, *[p.reshape(1, DIFF_QK_DIM).astype(F32) for p in lam_params],
      subln_w.reshape(HEAD_DIM, 1).astype(F32), *hosted)
    converted = iter(res[1:])
    copies = [next(converted) if _cast_blocks_per_expert(w, n_steps) else w.astype(BF16) for w in cast]
    return res[0].reshape(bsz * seq, DIFF_WIDTH), copies


HALO_ROWS = 16


def _conv_kernel(tiles_per_seq, x_ref, prev_ref, w_ref, o_ref):
    i = pl.program_id(0)
    j = pl.program_id(1)
    x = x_ref[...].astype(F32)
    prev = prev_ref[...].astype(F32)[HALO_ROWS - SUBLANES:]
    prev = jnp.where(i % tiles_per_seq == 0, jnp.zeros_like(prev), prev)
    w = w_ref[...]
    head_rows = lax.broadcasted_iota(jnp.int32, (SUBLANES, x.shape[1]), 0)
    y = x * w[CONV_WIDTH - 1:CONV_WIDTH, :]
    y_head = y[:SUBLANES]
    for back in range(1, CONV_WIDTH):
        wk = w[CONV_WIDTH - 1 - back:CONV_WIDTH - back, :]
        y = y + pltpu.roll(x, back, axis=0) * wk
        mixed = jnp.where(head_rows < back, pltpu.roll(prev, back, axis=0), pltpu.roll(x[:SUBLANES], back, axis=0))
        y_head = y_head + mixed * wk
    is_qk = j < 2 * GDN_WIDTH // x.shape[1]
    q_scale = jnp.where(j < GDN_WIDTH // x.shape[1], HEAD_DIM ** -0.5, 1.0)

    def finish(v):
        v = _silu(v)
        outs = []
        for h in range(v.shape[1] // HEAD_DIM):
            vh = v[:, h * HEAD_DIM:(h + 1) * HEAD_DIM]
            nrm = lax.rsqrt(jnp.sum(vh * vh, axis=-1, keepdims=True) + EPS) * q_scale
            outs.append(vh * jnp.where(is_qk, nrm, 1.0))
        return jnp.concatenate(outs, axis=1)

    o_ref[...] = finish(jnp.concatenate([y_head, y[SUBLANES:]], axis=0)).astype(o_ref.dtype)


def _gdn_conv(proj, conv_w, seq):
    t = proj.shape[0]
    tm = min(512, seq)
    cw = GDN_WIDTH
    nj = 3 * GDN_WIDTH // cw
    halo = tm // HALO_ROWS
    return pl.pallas_call(
        functools.partial(_conv_kernel, seq // tm),
        grid=(t // tm, nj),
        in_specs=[pl.BlockSpec((tm, cw), lambda i, j: (i, j)),
                  pl.BlockSpec((HALO_ROWS, cw), lambda i, j: (jnp.maximum(i * halo - 1, 0), j)),
                  pl.BlockSpec((CONV_WIDTH, cw), lambda i, j: (0, j))],
        out_specs=pl.BlockSpec((tm, cw), lambda i, j: (i, j)),
        out_shape=jax.ShapeDtypeStruct((t, 3 * GDN_WIDTH), BF16),
        compiler_params=_cparams("parallel", "parallel"), name="gdn_conv",
    )(proj, proj, conv_w.astype(F32))


def _softplus(x):
    return jnp.maximum(x, 0.0) + jnp.log(1.0 + jnp.exp(-jnp.abs(x)))


def _gdn_kernel(n_chunks, q_ref, k_ref, v_ref, z_ref, ab_ref, abt_ref, alog_ref, dtb_ref, alog_t_ref, dtb_t_ref,
                gw_ref, o_ref, state_ref):
    C = GDN_CHUNK
    nh = GDN_HEADS
    rows = n_chunks * C

    @pl.when(pl.program_id(1) == 0)
    def _():
        state_ref[...] = jnp.zeros_like(state_ref)

    neg_a = -jnp.exp(alog_ref[...])
    g_col = neg_a * _softplus(ab_ref[:, :LANES].astype(F32) + dtb_ref[...])
    g_row = -jnp.exp(alog_t_ref[...]) * _softplus(abt_ref[...] + dtb_t_ref[...])
    r_i = lax.broadcasted_iota(jnp.int32, (rows, rows), 0)
    c_i = lax.broadcasted_iota(jnp.int32, (rows, rows), 1)
    same = (r_i // C) == (c_i // C)
    tri_l = jnp.where(same & (c_i <= r_i), 1.0, 0.0).astype(F32)
    tri_u = jnp.where(same & (r_i <= c_i), 1.0, 0.0).astype(F32)
    gc_all = jnp.dot(tri_l, g_col, precision=HIGHEST, preferred_element_type=F32)
    gr_all = jnp.dot(g_row, tri_u, precision=HIGHEST, preferred_element_type=F32)
    b_all = _sigmoid(ab_ref[:, :LANES].astype(F32))

    incl = same & (r_i >= c_i)
    strict = same & (r_i > c_i)
    eye = jnp.where(r_i == c_i, 1.0, 0.0).astype(F32)
    gw = gw_ref[...]

    heads = range(nh)
    sls = [slice(h * HEAD_DIM, (h + 1) * HEAD_DIM) for h in heads]
    q = [q_ref[:, sl] for sl in sls]
    k = [k_ref[:, sl] for sl in sls]
    kf = [x.astype(F32) for x in k]
    gcol = [gc_all[:, h:h + 1] for h in heads]
    beta = [b_all[:, nh + h:nh + h + 1] for h in heads]
    decay = [jnp.exp(jnp.where(incl, gcol[h] - gr_all[h:h + 1, :], -jnp.inf)) for h in heads]
    pw = [jnp.where(strict, -(beta[h] * _dot_nt(k[h], k[h]) * decay[h]), 0.0) for h in heads]
    t_mat = [eye + p for p in pw]
    pw = [p.astype(BF16) for p in pw]
    for _ in range(int(math.log2(C)) - 1):
        pw = [_dot(p, p).astype(BF16) for p in pw]
        t_mat = [t + _dot(t.astype(BF16), p) for t, p in zip(t_mat, pw)]
    eg = [jnp.exp(g) for g in gcol]
    uw = [_dot(t_mat[h].astype(BF16),
               jnp.concatenate([v_ref[:, sls[h]].astype(F32) * beta[h], kf[h] * (beta[h] * eg[h])],
                               axis=1).astype(BF16)) for h in heads]
    u_all = [x[:, :HEAD_DIM] for x in uw]
    w_all = [x[:, HEAD_DIM:].astype(BF16) for x in uw]
    qk_all = [jnp.where(incl, _dot_nt(q[h], k[h]) * decay[h], 0.0).astype(BF16) for h in heads]
    qe_all = [(q[h].astype(F32) * eg[h]).astype(BF16) for h in heads]
    g_last = [[g[(c + 1) * C - 1:(c + 1) * C, :] for c in range(n_chunks)] for g in gcol]
    k_dec = [(kf[h] * jnp.exp(jnp.concatenate([jnp.broadcast_to(g, (C, 1)) for g in g_last[h]], axis=0)
                              - gcol[h])).astype(BF16) for h in heads]

    wq_all = [jnp.concatenate([x[c * C:(c + 1) * C] for c in range(n_chunks) for x in (w_all[h], qe_all[h])],
                              axis=0) for h in heads]

    state = [state_ref[h] for h in heads]
    for c in range(n_chunks):
        rs = slice(c * C, (c + 1) * C)
        sb = [s.astype(BF16) for s in state]
        ws_qs = [_dot(wq_all[h][2 * c * C:2 * (c + 1) * C], sb[h]) for h in heads]
        v_new = [(u_all[h][rs] - ws_qs[h][:C]).astype(BF16) for h in heads]
        o = [ws_qs[h][C:] + _dot(qk_all[h][rs, rs], v_new[h]) for h in heads]
        state = [state[h] * jnp.exp(g_last[h][c]) + _dot_tn(k_dec[h][rs], v_new[h]) for h in heads]
        for h in heads:
            gated = gw * _silu(z_ref[rs, sls[h]].astype(F32))
            o_h = o[h] * lax.rsqrt(jnp.mean(o[h] * o[h], axis=-1, keepdims=True) + EPS) * gated
            o_ref[rs, sls[h]] = o_h.astype(o_ref.dtype)
    for h in heads:
        state_ref[h] = state[h]


def _gdn(qkv, proj, bsz, seq, a_log, dt_bias, gnorm_w):
    t = bsz * seq
    n_chunks = min(4, seq // GDN_CHUNK)
    rows = n_chunks * GDN_CHUNK
    steps = seq // rows
    ab_off = SEG_U * SEG + AB_OFF
    abt = proj[:, ab_off:ab_off + 2 * SUBLANES].astype(F32).T

    def lane_vec(p):
        return jnp.zeros((1, LANES), F32).at[0, :GDN_HEADS].set(p.astype(F32))

    def sublane_vec(p):
        return jnp.zeros((2 * SUBLANES, 1), F32).at[:GDN_HEADS, 0].set(p.astype(F32))

    def rowblk(seg):
        return pl.BlockSpec((rows, SEG), lambda b, s: (b * steps + s, seg))

    return pl.pallas_call(
        functools.partial(_gdn_kernel, n_chunks),
        grid=(bsz, steps),
        in_specs=[rowblk(0), rowblk(1), rowblk(2),
                  pl.BlockSpec((rows, SEG), lambda b, s: (b * steps + s, SEG_GZ)),
                  pl.BlockSpec((rows, 2 * LANES), lambda b, s: (b * steps + s, ab_off // (2 * LANES))),
                  pl.BlockSpec((2 * SUBLANES, rows), lambda b, s: (0, b * steps + s)),
                  pl.BlockSpec((1, LANES), lambda b, s: (0, 0)),
                  pl.BlockSpec((1, LANES), lambda b, s: (0, 0)),
                  pl.BlockSpec((2 * SUBLANES, 1), lambda b, s: (0, 0)),
                  pl.BlockSpec((2 * SUBLANES, 1), lambda b, s: (0, 0)),
                  pl.BlockSpec((1, HEAD_DIM), lambda b, s: (0, 0))],
        out_specs=pl.BlockSpec((rows, GDN_WIDTH), lambda b, s: (b * steps + s, 0)),
        out_shape=jax.ShapeDtypeStruct((t, GDN_WIDTH), BF16),
        scratch_shapes=[pltpu.VMEM((GDN_HEADS, HEAD_DIM, HEAD_DIM), F32)],
        compiler_params=_cparams("parallel", "arbitrary"), name="gdn",
    )(qkv, qkv, qkv, proj, proj, abt, lane_vec(a_log), lane_vec(dt_bias),
      sublane_vec(a_log), sublane_vec(dt_bias), gnorm_w.reshape(1, HEAD_DIM).astype(F32))


def _outproj_kernel(with_router, a1_ref, a2_ref, a3_ref, w1_ref, w2_ref, w3_ref, h_ref, gate_ref,
                    nw_ref, scale_ref, shift_ref, *rest):
    if with_router:
        wr_hi_ref, wr_lo_ref, hres_ref, hn_ref, route_ref = rest
    else:
        hres_ref, hn_ref = rest
    mix = _dot(a1_ref[...], w1_ref[...]) + _dot(a2_ref[...], w2_ref[...]) + _dot(a3_ref[...], w3_ref[...])
    h = h_ref[...] + gate_ref[...] * mix
    hres_ref[...] = h
    hn = _rms_mod(h, nw_ref[...], scale_ref[...], shift_ref[...])
    hn_ref[...] = hn.astype(hn_ref.dtype)
    if with_router:
        hn_hi = hn.astype(BF16)
        hn_lo = (hn - hn_hi.astype(F32)).astype(BF16)
        logits = (_dot(hn_hi, wr_hi_ref[...]) + _dot(hn_lo, wr_hi_ref[...])) + _dot(hn_hi, wr_lo_ref[...])
        lane = lax.broadcasted_iota(jnp.int32, logits.shape, 1)
        lg = jnp.where(lane < N_EXPERTS, logits, -jnp.inf)
        m1 = jnp.max(lg, axis=-1, keepdims=True)
        i1 = jnp.min(jnp.where(lg == m1, lane, LANES), axis=-1, keepdims=True)
        lg2 = jnp.where(lane == i1, -jnp.inf, lg)
        m2 = jnp.max(lg2, axis=-1, keepdims=True)
        i2 = jnp.min(jnp.where(lg2 == m2, lane, LANES), axis=-1, keepdims=True)
        e = jnp.exp(m2 - m1)
        g1 = 1.0 / (1.0 + e)
        g2 = e / (1.0 + e)
        route_ref[...] = jnp.where(lane == 0, i1.astype(F32),
                                   jnp.where(lane == 1, i2.astype(F32),
                                             jnp.where(lane == 2, g1, jnp.where(lane == 3, g2, 0.0))))


def _outproj(a1, a2, a3, w_out, h_res, seq, gate, norm_w, scale, shift, w_router):
    t, d = h_res.shape
    tm = min(256, t)
    with_router = w_router is not None
    k1, k2 = a1.shape[1], a2.shape[1]
    w1, w2, w3 = w_out[:k1], w_out[k1:k1 + k2], w_out[k1 + k2:]

    def rows(w):
        return pl.BlockSpec((tm, w), lambda i: (i, 0))

    def whole(a):
        return pl.BlockSpec(a.shape, lambda i: (0, 0))

    per_batch = pl.BlockSpec((None, 1, d), lambda i: ((i * tm) // seq, 0, 0))
    nw = norm_w.reshape(1, d).astype(F32)
    args = [a1, a2, a3, w1, w2, w3, h_res, gate, nw, scale, shift]
    specs = [rows(k1), rows(k2), rows(a3.shape[1]), whole(w1), whole(w2), whole(w3), rows(d), per_batch,
             whole(nw), per_batch, per_batch]
    out_shape = [jax.ShapeDtypeStruct((t, d), F32), jax.ShapeDtypeStruct((t, d), BF16)]
    out_specs = [rows(d), rows(d)]
    if with_router:
        wr = jnp.zeros((d, LANES), F32).at[:, :N_EXPERTS].set(w_router.astype(F32))
        wr_hi = wr.astype(BF16)
        wr_lo = (wr - wr_hi.astype(F32)).astype(BF16)
        args += [wr_hi, wr_lo]
        specs += [whole(wr_hi), whole(wr_lo)]
        out_shape.append(jax.ShapeDtypeStruct((t, LANES), F32))
        out_specs.append(rows(LANES))
    return pl.pallas_call(
        functools.partial(_outproj_kernel, with_router),
        grid=(t // tm,), in_specs=specs, out_specs=out_specs, out_shape=out_shape,
        compiler_params=_cparams("parallel"), name="out_proj",
    )(*args)


FFN_SUB = 256


def _ffn_kernel(row_gated, fused_norm, has_mod, emit_res, te_ref, nv_ref, x_ref, w1_ref, w3_ref, w2_ref, *rest):
    rest = list(rest)
    rg_ref = rest.pop(0) if row_gated else None
    if fused_norm:
        h_ref, gate_ref, nw_ref = rest.pop(0), rest.pop(0), rest.pop(0)
        scale_ref, shift_ref = (rest.pop(0), rest.pop(0)) if has_mod else (None, None)
        hres_ref = rest.pop(0) if emit_res else None
    o_ref, acc_ref = rest
    i = pl.program_id(0)
    f = pl.program_id(1)

    @pl.when(f == 0)
    def _():
        acc_ref[...] = jnp.zeros_like(acc_ref)

    @pl.when(i < nv_ref[0])
    def _():
        x = x_ref[...]
        n_sub = w1_ref.shape[1] // FFN_SUB

        def up(s):
            cols = slice(s * FFN_SUB, (s + 1) * FFN_SUB)
            return _dot(x, w1_ref[:, cols]), _dot(x, w3_ref[:, cols])

        def down(s, h):
            return _dot((_silu(h[0]) * h[1]).astype(BF16), w2_ref[s * FFN_SUB:(s + 1) * FFN_SUB, :])

        h = up(0)
        total = acc_ref[...]
        for s in range(1, n_sub + 1):
            h_next = up(s) if s < n_sub else None
            total = total + down(s - 1, h)
            h = h_next
        acc_ref[...] = total

    @pl.when(f == pl.num_programs(1) - 1)
    def _():
        y = acc_ref[...]
        if row_gated:
            y = y * rg_ref[...]
        if fused_norm:
            hh = h_ref[...] + gate_ref[...] * y
            if emit_res:
                hres_ref[...] = hh
            scale = scale_ref[...] if has_mod else None
            shift = shift_ref[...] if has_mod else None
            y = _rms_mod(hh, nw_ref[...], scale, shift)
        o_ref[...] = y.astype(o_ref.dtype)


def _ffn(x, w1, w3, w2, tile_expert, n_valid, row_gate, tm, tf, norm=None):
    r, d = x.shape
    ff = w1.shape[2]
    tm, tf = min(tm, r), min(tf, ff)
    rows = pl.BlockSpec((tm, d), lambda i, f, te, nv: (i, 0))
    in_specs = [rows,
                pl.BlockSpec((None, d, tf), lambda i, f, te, nv: (te[i], 0, f)),
                pl.BlockSpec((None, d, tf), lambda i, f, te, nv: (te[i], 0, f)),
                pl.BlockSpec((None, tf, d), lambda i, f, te, nv: (te[i], f, 0))]
    args = [tile_expert, n_valid, x, w1, w3, w2]
    if row_gate is not None:
        in_specs.append(pl.BlockSpec((tm, 1), lambda i, f, te, nv: (i, 0)))
        args.append(row_gate)
    out_shape, out_specs = [], []
    has_mod = emit_res = False
    out_dtype = BF16
    if norm is not None:
        h_res, seq, gate, norm_w, scale, shift, emit_res, out_dtype = norm
        has_mod = scale is not None
        per_batch = pl.BlockSpec((None, 1, d), lambda i, f, te, nv: ((i * tm) // seq, 0, 0))
        in_specs += [rows, per_batch, pl.BlockSpec((1, d), lambda i, f, te, nv: (0, 0))]
        args += [h_res, gate, norm_w.reshape(1, d)]
        if has_mod:
            in_specs += [per_batch, per_batch]
            args += [scale, shift]
        if emit_res:
            out_shape.append(jax.ShapeDtypeStruct((r, d), F32))
            out_specs.append(rows)
    out_shape.append(jax.ShapeDtypeStruct((r, d), out_dtype))
    out_specs.append(rows)
    grid_spec = pltpu.PrefetchScalarGridSpec(
        num_scalar_prefetch=2, grid=(r // tm, ff // tf), in_specs=in_specs, out_specs=out_specs,
        scratch_shapes=[pltpu.VMEM((tm, d), F32)])
    res = pl.pallas_call(
        functools.partial(_ffn_kernel, row_gate is not None, norm is not None, has_mod, emit_res),
        grid_spec=grid_spec, out_shape=out_shape,
        compiler_params=_cparams("parallel", "arbitrary"), name="ffn",
    )(*args)
    return res if emit_res else res[0]


MOE_TILE = 512


def _moe_plan(route, tm):
    t = route.shape[0]
    e_flat = route[:, :2].astype(jnp.int32).reshape(-1)
    onehot = (e_flat[:, None] == jnp.arange(N_EXPERTS, dtype=jnp.int32)[None, :]).astype(jnp.int32)
    counts = jnp.sum(onehot, axis=0)
    rank = jnp.sum((jnp.cumsum(onehot, axis=0) - onehot) * onehot, axis=1)
    padded = ((counts + tm - 1) // tm) * tm
    ends = jnp.cumsum(padded)
    pos = (ends - padded)[e_flat] + rank
    n_rows = 2 * t + N_EXPERTS * tm
    n_tiles, n_blocks = n_rows // tm, t // tm
    tok_f = (jnp.arange(2 * t, dtype=jnp.int32) // 2).astype(F32)
    rows = jnp.stack([jnp.full((n_rows,), -1.0, F32), jnp.zeros((n_rows,), F32)], axis=1)
    rows = rows.at[pos].set(jnp.stack([tok_f, route[:, 2:4].reshape(-1)], axis=1))
    src_tok = rows[:, 0].astype(jnp.int32)
    row_gate = rows[:, 1]
    n_valid = (ends[-1] // tm).astype(jnp.int32)
    tile_start = jnp.arange(n_tiles, dtype=jnp.int32) * tm
    tile_expert = jnp.sum((tile_start[:, None] >= ends[None, :]).astype(jnp.int32), axis=1)
    last_expert = jnp.sum((tile_start[n_valid - 1] >= ends).astype(jnp.int32))
    tile_expert = jnp.where(tile_start < ends[-1], tile_expert, last_expert).astype(jnp.int32)

    blk = jnp.where(src_tok >= 0, src_tok // tm, -1).reshape(n_tiles, tm, 1)
    in_block = blk == jnp.arange(n_blocks, dtype=jnp.int32)[None, None, :]
    incidence = jnp.any(in_block, axis=1)
    row_sub = (jnp.arange(tm, dtype=jnp.int32) // COMBINE_SUB)[None, :, None]
    sub_lo = jnp.min(jnp.where(in_block, row_sub, tm // COMBINE_SUB), axis=1)
    sub_hi = jnp.max(jnp.where(in_block, row_sub, -1), axis=1)
    unused = (tile_start >= ends[-1])[:, None] & (jnp.arange(n_blocks) == 0)[None, :]
    w_max = n_tiles + N_EXPERTS * n_blocks + N_EXPERTS

    def work_list(m, *per_pair):
        flat = m.reshape(-1)
        n = jnp.sum(flat.astype(jnp.int32))
        idx = jnp.nonzero(flat, size=w_max, fill_value=0)[0].astype(jnp.int32)
        w = jnp.arange(w_max, dtype=jnp.int32)
        valid = w < n
        idx = jnp.where(valid, idx, idx[n - 1])
        major, minor = idx // m.shape[1], idx % m.shape[1]
        first = valid & ((w == 0) | (major != jnp.roll(major, 1)))
        last = valid & ((w == n - 1) | (major != jnp.roll(major, -1)))
        extra = [a.reshape(-1)[idx] for a in per_pair]
        return [a.astype(jnp.int32) for a in (major, minor, first, last, valid, *extra)]

    return (src_tok, row_gate.reshape(n_rows, 1), tile_expert, n_valid.reshape(1),
            work_list(incidence | unused), work_list(incidence.T, sub_lo.T, sub_hi.T))


def _dispatch_kernel(wi_ref, wj_ref, first_ref, last_ref, valid_ref, x_ref, tok_ref, o_ref):
    w = pl.program_id(0)
    tm = x_ref.shape[0]

    @pl.when(valid_ref[w] == 1)
    def _():
        col_tok = lax.broadcasted_iota(jnp.int32, (tm, tm), 1) + wj_ref[w] * tm
        onehot = jnp.where(tok_ref[...] == col_tok, 1.0, 0.0).astype(BF16)
        picked = _dot(onehot, x_ref[...]).astype(o_ref.dtype)

        @pl.when(first_ref[w] == 1)
        def _():
            o_ref[...] = picked

        @pl.when(first_ref[w] == 0)
        def _():
            o_ref[...] += picked


def _dispatch(x, src_tok, work, tm):
    t, d = x.shape
    n_rows = src_tok.shape[0]
    grid_spec = pltpu.PrefetchScalarGridSpec(
        num_scalar_prefetch=5, grid=(work[0].shape[0],),
        in_specs=[pl.BlockSpec((tm, d), lambda w, wi, wj, *_: (wj[w], 0)),
                  pl.BlockSpec((tm, 1), lambda w, wi, wj, *_: (wi[w], 0))],
        out_specs=pl.BlockSpec((tm, d), lambda w, wi, wj, *_: (wi[w], 0)))
    return pl.pallas_call(
        _dispatch_kernel, grid_spec=grid_spec,
        out_shape=jax.ShapeDtypeStruct((n_rows, d), x.dtype),
        compiler_params=_cparams("arbitrary"), name="moe_dispatch",
    )(*work, x, src_tok.reshape(n_rows, 1))


COMBINE_SUB = LANES


def _combine_kernel(has_mod, emit_res, vj_ref, vi_ref, first_ref, last_ref, valid_ref, lo_ref, hi_ref,
                    y_ref, tok_ref, h_ref, gate_ref, nw_ref, *rest):
    rest = list(rest)
    scale_ref = shift_ref = None
    if has_mod:
        scale_ref, shift_ref = rest.pop(0), rest.pop(0)
    hres_ref = rest.pop(0) if emit_res else None
    o_ref, acc_ref = rest
    w = pl.program_id(0)
    tm = y_ref.shape[0]

    n_sub = tm // COMBINE_SUB
    window = 2 * COMBINE_SUB

    @pl.when(valid_ref[w] == 1)
    def _():
        @pl.when(first_ref[w] == 1)
        def _():
            acc_ref[...] = jnp.zeros_like(acc_ref)

        def gathered(tok_row, rows):
            row_tok = lax.broadcasted_iota(jnp.int32, (tm, tok_row.shape[1]), 0) + vj_ref[w] * tm
            return _dot(jnp.where(tok_row == row_tok, 1.0, 0.0).astype(BF16), rows)

        first_sub = jnp.minimum(lo_ref[w], n_sub - 2)
        narrow = hi_ref[w] <= first_sub + 1

        @pl.when(narrow)
        def _():
            toks = tok_ref[pl.ds(first_sub, 2)]
            rows = y_ref[pl.ds(pl.multiple_of(first_sub * COMBINE_SUB, COMBINE_SUB), window), :]
            acc_ref[...] += gathered(jnp.concatenate([toks[0], toks[1]], axis=1), rows)

        @pl.when(jnp.logical_not(narrow))
        def _():
            toks = tok_ref[...]
            acc_ref[...] += gathered(jnp.concatenate([toks[s] for s in range(n_sub)], axis=1), y_ref[...])

        @pl.when(last_ref[w] == 1)
        def _():
            h = h_ref[...] + gate_ref[...] * acc_ref[...]
            if emit_res:
                hres_ref[...] = h
            scale = scale_ref[...] if has_mod else None
            shift = shift_ref[...] if has_mod else None
            o_ref[...] = _rms_mod(h, nw_ref[...], scale, shift).astype(o_ref.dtype)


def _combine_resnorm(ys, src_tok, work, tm, h_res, seq, gate, norm_w, scale, shift, emit_res, out_dtype):
    t, d = h_res.shape
    n_tiles = ys.shape[0] // tm
    has_mod = scale is not None

    def tok_rows(w, vj, vi, *_):
        return (vj[w], 0)

    per_batch = pl.BlockSpec((None, 1, d), lambda w, vj, *_: ((vj[w] * tm) // seq, 0, 0))
    in_specs = [pl.BlockSpec((tm, d), lambda w, vj, vi, *_: (vi[w], 0)),
                pl.BlockSpec((None, tm // COMBINE_SUB, 1, COMBINE_SUB), lambda w, vj, vi, *_: (vi[w], 0, 0, 0)),
                pl.BlockSpec((tm, d), tok_rows), per_batch,
                pl.BlockSpec((1, d), lambda w, *_: (0, 0))]
    args = list(work) + [ys, src_tok.reshape(n_tiles, tm // COMBINE_SUB, 1, COMBINE_SUB), h_res, gate,
                         norm_w.reshape(1, d)]
    if has_mod:
        in_specs += [per_batch, per_batch]
        args += [scale, shift]
    out_shape, out_specs = [], []
    if emit_res:
        out_shape.append(jax.ShapeDtypeStruct((t, d), F32))
        out_specs.append(pl.BlockSpec((tm, d), tok_rows))
    out_shape.append(jax.ShapeDtypeStruct((t, d), out_dtype))
    out_specs.append(pl.BlockSpec((tm, d), tok_rows))
    grid_spec = pltpu.PrefetchScalarGridSpec(
        num_scalar_prefetch=len(work), grid=(work[0].shape[0],), in_specs=in_specs, out_specs=out_specs,
        scratch_shapes=[pltpu.VMEM((tm, d), F32)])
    res = pl.pallas_call(
        functools.partial(_combine_kernel, has_mod, emit_res), grid_spec=grid_spec, out_shape=out_shape,
        compiler_params=_cparams("arbitrary"), name="moe_combine",
    )(*args)
    return res if emit_res else res[0]


def kernel(x, c, positions, w_ada, b_ada, norm_mix, norm_ffn, norm_final, w_in, w_out, ssm_a_re, ssm_a_im, ssm_log_dt, ssm_b_re, ssm_b_im, ssm_c_re, ssm_c_im, ssm_d, ssm_w_glu, diff_lam_q1, diff_lam_k1, diff_lam_q2, diff_lam_k2, diff_subln, gdn_conv, gdn_a_log, gdn_dt_bias, gdn_norm, ffn_w1, ffn_w3, ffn_w2, moe_router, moe_w1, moe_w3, moe_w2):
    bsz, seq, d = x.shape
    t = bsz * seq
    depth = w_in.shape[0]
    h_res = x.astype(F32).reshape(t, d)

    c_pad = jnp.zeros((SUBLANES, d), F32).at[:bsz].set(c.astype(F32))
    mod = _ada(c_pad, w_ada, b_ada)[:, :bsz]
    mods = [[m.reshape(bsz, 1, d) for m in jnp.split(mod[l], 6, axis=-1)] for l in range(depth)]
    rope_tables = _rope_tables(positions)
    scan_steps = max(1, math.ceil(math.log2(seq // SSM_CHUNK)))
    s5_tables = jax.vmap(functools.partial(_s5_tables, n_steps=scan_steps))(
        ssm_a_re, ssm_a_im, ssm_log_dt, ssm_b_re, ssm_b_im, ssm_c_re, ssm_c_im)

    hn = _resnorm(h_res, seq, None, None, norm_mix[0].astype(F32), mods[0][1], mods[0][0], False, BF16)
    out = None
    moe_bf16 = []
    for l in range(depth):
        shift1, scale1, gate1, shift2, scale2, gate2 = mods[l]
        is_moe = l % 2 == 1
        proj = _in_proj(hn, w_in[l])

        y_ssm = _s5_mixer(proj, bsz, seq, [tab[l] for tab in s5_tables], ssm_d[l], ssm_w_glu[l].astype(BF16))

        lambda_init = 0.8 - 0.6 * math.exp(-0.3 * l)
        qk = _rope(proj, rope_tables)
        if is_moe:
            to_cast = (moe_w2[l // 2],)
        elif l + 1 < depth:
            to_cast = (moe_w1[(l + 1) // 2], moe_w3[(l + 1) // 2])
        else:
            to_cast = ()
        y_diff, copies = _diff_attention(qk, proj, bsz, seq,
                                         (diff_lam_q1[l], diff_lam_k1[l], diff_lam_q2[l], diff_lam_k2[l]),
                                         diff_subln[l], lambda_init, to_cast)
        if is_moe:
            moe_bf16 = moe_bf16 + copies
        elif copies:
            moe_bf16 = copies

        qkv = _gdn_conv(proj, gdn_conv[l], seq)
        y_gdn = _gdn(qkv, proj, bsz, seq, gdn_a_log[l], gdn_dt_bias[l], gdn_norm[l])

        res = _outproj(y_ssm, y_diff, y_gdn, w_out[l].astype(BF16), h_res, seq, gate1,
                       norm_ffn[l], scale2, shift2, moe_router[l // 2] if is_moe else None)
        h_res, hn2 = res[0], res[1]

        last = l + 1 == depth
        if last:
            nxt = (norm_final.astype(F32), None, None, False, x.dtype)
        else:
            nxt = (norm_mix[l + 1].astype(F32), mods[l + 1][1], mods[l + 1][0], True, BF16)
        if is_moe:
            src_tok, row_gate, tile_expert, n_valid, work_sorted, work_token = _moe_plan(res[2], MOE_TILE)
            xs = _dispatch(hn2, src_tok, work_sorted, MOE_TILE)
            ys = _ffn(xs, *moe_bf16, tile_expert, n_valid, row_gate, MOE_TILE, 1024)
            res = _combine_resnorm(ys, src_tok, work_token, MOE_TILE, h_res, seq, gate2, *nxt)
        else:
            n_tiles = t // min(512, t)
            res = _ffn(hn2, ffn_w1[l // 2:l // 2 + 1].astype(BF16), ffn_w3[l // 2:l // 2 + 1].astype(BF16),
                       ffn_w2[l // 2:l // 2 + 1].astype(BF16), jnp.zeros((n_tiles,), jnp.int32),
                       jnp.full((1,), n_tiles, jnp.int32), None, 512, 512,
                       norm=(h_res, seq, gate2) + nxt)
        if last:
            out = res
        else:
            h_res, hn = res
    return out.reshape(bsz, seq, d)
```

```python
import functools
import math

import numpy as np
import jax
import jax.numpy as jnp
from jax import lax
from jax.experimental import pallas as pl
from jax.experimental.pallas import tpu as pltpu

F32 = jnp.float32
BF16 = jnp.bfloat16
HIGHEST = lax.Precision.HIGHEST

D_MODEL = 2048
SSM_WIDTH = 512
SSM_CH = 16
SSM_GROUPS = SSM_WIDTH // SSM_CH
SSM_STATE = 64
SSM_CHUNK = 16
DIFF_WIDTH = 768
HEAD_DIM = 128
DIFF_HEADS = DIFF_WIDTH // HEAD_DIM
DIFF_QK_DIM = HEAD_DIM // 2
GDN_WIDTH = 768
GDN_HEADS = GDN_WIDTH // HEAD_DIM
CONV_WIDTH = 4
GDN_CHUNK = 64
ROPE_THETA = 500000.0
ROPE_DIM = DIFF_QK_DIM // 4
ROPE_HALF = ROPE_DIM // 2
N_EXPERTS = 8
EPS = 1e-6
LANES = 128
SUBLANES = 8

SEG = 768
SEG_GQ, SEG_GK, SEG_GV, SEG_GZ, SEG_DQ, SEG_DK, SEG_DV, SEG_U = range(8)
PROJ_WIDTH = 8 * SEG
AB_OFF = SSM_WIDTH

VMEM_LIMIT = 56 * 1024 * 1024


def _cparams(*sem):
    return pltpu.CompilerParams(dimension_semantics=sem, vmem_limit_bytes=VMEM_LIMIT)


def _dot(a, b):
    return jnp.dot(a, b, preferred_element_type=F32)


def _dot_nt(a, b):
    return lax.dot_general(a, b, (((1,), (1,)), ((), ())), preferred_element_type=F32)


def _dot_tn(a, b):
    return lax.dot_general(a, b, (((0,), (0,)), ((), ())), preferred_element_type=F32)


def _sigmoid(x):
    return 1.0 / (1.0 + jnp.exp(-x))


def _silu(x):
    return x * _sigmoid(x)


def _ada_kernel(c_ref, w_ref, b_ref, o_ref):
    cond = _silu(c_ref[...])
    o_ref[...] = _dot(cond.astype(BF16), w_ref[...].astype(BF16)) + b_ref[...]


def _ada(c_pad, w_ada, b_ada):
    depth, d, n = w_ada.shape
    tn = 1024
    return pl.pallas_call(
        _ada_kernel,
        grid=(depth, n // tn),
        in_specs=[pl.BlockSpec((SUBLANES, d), lambda l, j: (0, 0)),
                  pl.BlockSpec((None, d, tn), lambda l, j: (l, 0, j)),
                  pl.BlockSpec((None, 1, tn), lambda l, j: (l, 0, j))],
        out_specs=pl.BlockSpec((None, SUBLANES, tn), lambda l, j: (l, 0, j)),
        out_shape=jax.ShapeDtypeStruct((depth, SUBLANES, n), F32),
        compiler_params=_cparams("arbitrary", "arbitrary"),
        name="ada",
    )(c_pad, w_ada, b_ada.reshape(depth, 1, n))


def _rms_mod(h, w, scale, shift):
    y = h * lax.rsqrt(jnp.mean(h * h, axis=-1, keepdims=True) + EPS) * w
    if scale is not None:
        y = y * (1.0 + scale) + shift
    return y


def _resnorm_kernel(has_delta, has_mod, emit_res, *refs):
    refs = list(refs)
    h = refs.pop(0)[...]
    if has_delta:
        y = refs.pop(0)[...].astype(F32)
        h = h + refs.pop(0)[...] * y
    w = refs.pop(0)[...]
    scale = shift = None
    if has_mod:
        scale = refs.pop(0)[...]
        shift = refs.pop(0)[...]
    if emit_res:
        refs.pop(0)[...] = h
    o_ref = refs.pop(0)
    o_ref[...] = _rms_mod(h, w, scale, shift).astype(o_ref.dtype)


def _resnorm(h_res, seq, delta, gate, norm_w, scale, shift, emit_res, out_dtype):
    t, d = h_res.shape
    tm = min(256, t)
    row = pl.BlockSpec((tm, d), lambda i: (i, 0))
    per_batch = pl.BlockSpec((None, 1, d), lambda i: ((i * tm) // seq, 0, 0))
    args, specs = [h_res], [row]
    if delta is not None:
        args += [delta, gate]
        specs += [row, per_batch]
    args.append(norm_w.reshape(1, d))
    specs.append(pl.BlockSpec((1, d), lambda i: (0, 0)))
    if scale is not None:
        args += [scale, shift]
        specs += [per_batch, per_batch]
    out_shape, out_specs = [], []
    if emit_res:
        out_shape.append(jax.ShapeDtypeStruct((t, d), F32))
        out_specs.append(row)
    out_shape.append(jax.ShapeDtypeStruct((t, d), out_dtype))
    out_specs.append(row)
    res = pl.pallas_call(
        functools.partial(_resnorm_kernel, delta is not None, scale is not None, emit_res),
        grid=(t // tm,), in_specs=specs, out_specs=out_specs, out_shape=out_shape,
        compiler_params=_cparams("parallel"), name="resnorm",
    )(*args)
    return res if emit_res else res[0]


IN_BLOCK = 2 * LANES


def _in_proj_blocks():
    src_segments = [(SEG_U, SSM_WIDTH), (SEG_DQ, DIFF_WIDTH), (SEG_DK, DIFF_WIDTH), (SEG_DV, DIFF_WIDTH),
                    (SEG_GQ, 3 * GDN_WIDTH), (SEG_GZ, GDN_WIDTH)]
    dest = []
    for seg, width in src_segments:
        dest += [seg * SEG // IN_BLOCK + b for b in range(width // IN_BLOCK)]
    dest.append((SEG_U * SEG + AB_OFF) // IN_BLOCK)
    return np.asarray(dest, np.int32)


def _in_proj_kernel(n_cols, dest_ref, x_ref, w_ref, o_ref, wb_ref):
    j = pl.program_id(1)

    @pl.when(pl.program_id(0) == 0)
    def _():
        col = lax.broadcasted_iota(jnp.int32, w_ref.shape, 1) + j * IN_BLOCK
        wb_ref[j] = jnp.where(col < n_cols, w_ref[...], 0.0).astype(BF16)

    o_ref[...] = _dot(x_ref[...], wb_ref[j]).astype(o_ref.dtype)


def _in_proj(x, w_in_l):
    m, k = x.shape
    n_cols = w_in_l.shape[1]
    dest = _in_proj_blocks()
    n_blocks = dest.shape[0]
    tm = min(2048, m)
    grid_spec = pltpu.PrefetchScalarGridSpec(
        num_scalar_prefetch=1, grid=(m // tm, n_blocks),
        in_specs=[pl.BlockSpec((tm, k), lambda i, j, dest: (i, 0)),
                  pl.BlockSpec((k, IN_BLOCK), lambda i, j, dest: (0, jnp.where(i == 0, j, n_blocks - 1)))],
        out_specs=pl.BlockSpec((tm, IN_BLOCK), lambda i, j, dest: (i, dest[j])),
        scratch_shapes=[pltpu.VMEM((n_blocks, k, IN_BLOCK), BF16)])
    return pl.pallas_call(
        functools.partial(_in_proj_kernel, n_cols), grid_spec=grid_spec,
        out_shape=jax.ShapeDtypeStruct((m, PROJ_WIDTH), BF16),
        compiler_params=_cparams("arbitrary", "arbitrary"), name="in_proj",
    )(jnp.asarray(dest), x, w_in_l)


def _s5_tables(a_re, a_im, log_dt, b_re, b_im, c_re, c_im, n_steps):
    L, G, P, H = SSM_CHUNK, SSM_GROUPS, SSM_STATE, SSM_CH
    lam = lax.complex(a_re.astype(F32), a_im.astype(F32))
    log_lam_bar = lam * jnp.exp(log_dt.astype(F32))[:, None]
    lam_bar = jnp.exp(log_lam_bar)
    b_bar = ((lam_bar - 1.0) / lam)[:, :, None] * lax.complex(b_re.astype(F32), b_im.astype(F32))
    c_mat = lax.complex(c_re.astype(F32), c_im.astype(F32))
    steps = jnp.arange(L + 1, dtype=F32)
    pw = jnp.exp(log_lam_bar[:, None, :] * steps[None, :, None])
    kern = jnp.real(jnp.einsum('ghp,gjp,gpi->gjih', c_mat, pw[:, :L], b_bar))
    lag = jnp.arange(L)[None, :] - jnp.arange(L)[:, None]
    tm = jnp.where((lag >= 0)[None, :, :, None, None], kern[:, jnp.clip(lag, 0, L - 1)], 0.0)
    tm = tm.transpose(0, 2, 4, 1, 3).reshape(G, L * H, L * H)
    zc = pw[:, L - 1 - jnp.arange(L)][:, :, :, None] * b_bar[:, None]
    zc = zc.transpose(0, 2, 1, 3).reshape(G, P, L * H)
    zm = jnp.concatenate([jnp.real(zc), jnp.imag(zc)], axis=1)
    cl = (c_mat[:, None] * pw[:, 1:L + 1][:, :, None, :]).reshape(G, L * H, P)
    ym = jnp.concatenate([jnp.real(cl), -jnp.imag(cl)], axis=2)
    a_pows = jnp.exp(log_lam_bar[:, None, :] * (L * 2.0 ** jnp.arange(n_steps, dtype=F32))[None, :, None])
    return (tm.astype(BF16), zm.astype(BF16), ym.astype(BF16),
            jnp.real(a_pows)[..., None], jnp.imag(a_pows)[..., None])


def _s5_core_kernel(n_steps, chunks_per_seq, u_ref, tm_ref, zm_ref, ym_ref, are_ref, aim_ref, o_ref):
    p = SSM_STATE
    n_pos, n_ch, n_lanes = u_ref.shape
    u = u_ref[...].reshape(n_pos * n_ch, n_lanes)
    z = _dot(zm_ref[...], u)
    x_re, x_im = z[:p], z[p:]
    c_idx = lax.broadcasted_iota(jnp.int32, x_re.shape, 1) % chunks_per_seq
    for k in range(n_steps):
        d = 1 << k
        a_re, a_im = are_ref[k], aim_ref[k]
        inside = c_idx >= d
        s_re = jnp.where(inside, pltpu.roll(x_re, d, axis=1), 0.0)
        s_im = jnp.where(inside, pltpu.roll(x_im, d, axis=1), 0.0)
        x_re, x_im = x_re + (a_re * s_re - a_im * s_im), x_im + (a_re * s_im + a_im * s_re)
    inside = c_idx >= 1
    x_prev = jnp.concatenate([jnp.where(inside, pltpu.roll(x_re, 1, axis=1), 0.0),
                              jnp.where(inside, pltpu.roll(x_im, 1, axis=1), 0.0)], axis=0)
    y = _dot(tm_ref[...], u) + _dot(ym_ref[...], x_prev.astype(BF16))
    o_ref[...] = y.astype(o_ref.dtype).reshape(o_ref.shape)


def _s5_core(u_t, tables, chunks_per_seq):
    n_pos, g, n_ch, r = u_t.shape
    tm, zm, ym, are, aim = tables
    p = SSM_STATE
    w = n_pos * n_ch
    n_steps = are.shape[1]

    def grp(*shape):
        return pl.BlockSpec((None,) + shape, lambda i: (i,) + (0,) * len(shape))

    slab = pl.BlockSpec((n_pos, None, n_ch, r), lambda i: (0, i, 0, 0))
    return pl.pallas_call(
        functools.partial(_s5_core_kernel, n_steps, chunks_per_seq),
        grid=(g,),
        in_specs=[slab, grp(w, w), grp(2 * p, w), grp(w, 2 * p), grp(n_steps, p, 1), grp(n_steps, p, 1)],
        out_specs=slab,
        out_shape=jax.ShapeDtypeStruct(u_t.shape, BF16),
        compiler_params=_cparams("parallel"), name="s5_core",
    )(u_t, tm, zm, ym, are, aim)


def _gelu_tanh(x):
    return 0.5 * x * (1.0 + jnp.tanh(math.sqrt(2.0 / math.pi) * (x + 0.044715 * (x * x * x))))


def _s5_post_kernel(y_ref, u_ref, d_ref, w_ref, o_ref):
    u = u_ref[:, :SSM_WIDTH].astype(F32)
    y = _gelu_tanh(y_ref[...].astype(F32) + d_ref[...] * u)
    o_ref[...] = (y * _sigmoid(_dot(y.astype(BF16), w_ref[...]))).astype(o_ref.dtype)


def _s5_post(y_core, proj, d_skip, w_glu):
    t = y_core.shape[0]
    tm = min(512, t)
    return pl.pallas_call(
        _s5_post_kernel,
        grid=(t // tm,),
        in_specs=[pl.BlockSpec((tm, SSM_WIDTH), lambda i: (i, 0)),
                  pl.BlockSpec((tm, SEG), lambda i: (i, SEG_U)),
                  pl.BlockSpec((1, SSM_WIDTH), lambda i: (0, 0)),
                  pl.BlockSpec((SSM_WIDTH, SSM_WIDTH), lambda i: (0, 0))],
        out_specs=pl.BlockSpec((tm, SSM_WIDTH), lambda i: (i, 0)),
        out_shape=jax.ShapeDtypeStruct((t, SSM_WIDTH), BF16),
        compiler_params=_cparams("parallel"), name="s5_post",
    )(y_core, proj, d_skip.reshape(1, SSM_WIDTH).astype(F32), w_glu)


def _s5_mixer(proj, bsz, seq, tables, d_skip, w_glu):
    L, G, H = SSM_CHUNK, SSM_GROUPS, SSM_CH
    t = bsz * seq
    nc = seq // L
    u = proj[:, SEG_U * SEG:SEG_U * SEG + SSM_WIDTH]
    u_t = u.reshape(t // L, L * G * H).T.reshape(L, G, H, t // L)
    y_t = _s5_core(u_t, tables, nc)
    y_core = y_t.reshape(L * G * H, t // L).T.reshape(t, SSM_WIDTH)
    return _s5_post(y_core, proj, d_skip, w_glu)


def _rope_tables(positions):
    inv_freq = ROPE_THETA ** (-jnp.arange(0, ROPE_DIM, 2, dtype=F32) / ROPE_DIM)
    ang = positions.astype(F32).reshape(-1)[:, None] * inv_freq
    cos, sin = jnp.cos(ang), jnp.sin(ang)
    r = np.arange(LANES) % DIFF_QK_DIM
    first = jnp.asarray(r < ROPE_HALF)[None, :]
    second = jnp.asarray((r >= ROPE_HALF) & (r < ROPE_DIM))[None, :]
    idx = jnp.asarray(r % ROPE_HALF)
    cos_l, sin_l = cos[:, idx], sin[:, idx]
    cosf = jnp.where(first | second, cos_l, 1.0)
    sin_a = jnp.where(first, -sin_l, 0.0)
    sin_b = jnp.where(second, sin_l, 0.0)
    return cosf, sin_a, sin_b


def _rope_kernel(x_ref, c_ref, sa_ref, sb_ref, o_ref):
    cosf, sin_a, sin_b = c_ref[...], sa_ref[...], sb_ref[...]
    n_slabs = x_ref.shape[1] // LANES
    for s in range(n_slabs):
        x = x_ref[:, s * LANES:(s + 1) * LANES].astype(F32)
        y = (x * cosf + pltpu.roll(x, LANES - ROPE_HALF, axis=1) * sin_a
             + pltpu.roll(x, ROPE_HALF, axis=1) * sin_b)
        if s < n_slabs // 2:
            y = y * (DIFF_QK_DIM ** -0.5 * math.log2(math.e))
        o_ref[:, s * LANES:(s + 1) * LANES] = y.astype(o_ref.dtype)


def _rope(proj, tables):
    t = proj.shape[0]
    tm = min(512, t)
    w = 2 * DIFF_WIDTH
    tab = pl.BlockSpec((tm, LANES), lambda i: (i, 0))
    return pl.pallas_call(
        _rope_kernel,
        grid=(t // tm,),
        in_specs=[pl.BlockSpec((tm, w), lambda i: (i, SEG_DQ // 2)), tab, tab, tab],
        out_specs=pl.BlockSpec((tm, w), lambda i: (i, 0)),
        out_shape=jax.ShapeDtypeStruct((t, w), BF16),
        compiler_params=_cparams("parallel"), name="rope",
    )(proj, *tables)


def _attn_kernel(tq, lambda_init, cast_blocks, q_ref, k_ref, vt_ref, lq1_ref, lk1_ref, lq2_ref, lk2_ref, sw_ref,
                 *rest):
    n_cast = len(cast_blocks)
    cast_in, o_ref, cast_out = rest[:n_cast], rest[n_cast], rest[n_cast + 1:]
    step = (pl.program_id(0) * pl.num_programs(1) + pl.program_id(1)) * pl.num_programs(2) + pl.program_id(2)
    for n_blk, src, dst in zip(cast_blocks, cast_in, cast_out):
        @pl.when(step < n_blk)
        def _(src=src, dst=dst):
            dst[...] = src[...].astype(dst.dtype)

    qi = pl.program_id(2)
    q = q_ref[...]
    lane = lax.broadcasted_iota(jnp.int32, q.shape, 1)
    zero = jnp.zeros_like(q)
    qm = (jnp.where(lane < DIFF_QK_DIM, q, zero), jnp.where(lane >= DIFF_QK_DIM, q, zero))

    def block(kv, masked, carry):
        start = pl.multiple_of(kv * tq, tq)
        k = k_ref[pl.ds(start, tq), :]
        v_t = vt_ref[:, pl.ds(start, tq)]
        scores = [_dot_nt(k, qm[i]) for i in range(2)]
        new = []
        for i in range(2):
            m_old, l_old, acc = carry[i]
            s = scores[i]
            if masked:
                key = lax.broadcasted_iota(jnp.int32, s.shape, 0)
                qry = lax.broadcasted_iota(jnp.int32, s.shape, 1)
                s = jnp.where(key <= qry, s, -jnp.inf)
            m_new = jnp.maximum(m_old, jnp.max(s, axis=0, keepdims=True))
            alpha = jnp.exp2(m_old - m_new)
            p = jnp.exp2(s - m_new)
            l_new = alpha * l_old + jnp.sum(p, axis=0, keepdims=True)
            new.append((m_new, l_new, alpha * acc + _dot(v_t, p.astype(BF16))))
        return tuple(new)

    init = tuple((jnp.full((1, tq), -jnp.inf, F32), jnp.zeros((1, tq), F32), jnp.zeros((HEAD_DIM, tq), F32))
                 for _ in range(2))
    carry = lax.fori_loop(0, qi, lambda kv, c: block(kv, False, c), init)
    (_, l1, acc1), (_, l2, acc2) = block(qi, True, carry)
    lam = (jnp.exp(jnp.sum(lq1_ref[...] * lk1_ref[...], axis=-1, keepdims=True))
           - jnp.exp(jnp.sum(lq2_ref[...] * lk2_ref[...], axis=-1, keepdims=True)) + lambda_init)
    o_t = acc1 / l1 - lam * (acc2 / l2)
    o_t = o_t * lax.rsqrt(jnp.mean(o_t * o_t, axis=0, keepdims=True) + EPS) * sw_ref[...] * (1.0 - lambda_init)
    o_ref[...] = o_t.T.astype(o_ref.dtype)


def _cast_blocks_per_expert(w, n_steps):
    e, r, _ = w.shape
    per = 1
    while e * per * 2 <= n_steps and r % (per * 2) == 0 and (r // (per * 2)) % (2 * SUBLANES) == 0:
        per *= 2
    return per if e * per <= n_steps and (r // per) % (2 * SUBLANES) == 0 else 0


def _diff_attention(qk, proj, bsz, seq, lam_params, subln_w, lambda_init, cast=()):
    tq = min(512, seq)
    nh = DIFF_HEADS
    nq = seq // tq
    n_steps = bsz * nh * nq
    qk3 = qk.reshape(bsz, seq, 2 * DIFF_WIDTH)
    v_t = proj[:, SEG_DV * SEG:(SEG_DV + 1) * SEG].reshape(bsz, seq, DIFF_WIDTH).transpose(0, 2, 1)
    vec = pl.BlockSpec((1, DIFF_QK_DIM), lambda b, h, i: (0, 0))
    hosted = [w for w in cast if _cast_blocks_per_expert(w, n_steps)]
    cast_specs, cast_blocks = [], []
    for w in hosted:
        per = _cast_blocks_per_expert(w, n_steps)
        n_blk = w.shape[0] * per

        def block_of(b, h, i, per=per, n_blk=n_blk):
            blk = jnp.minimum((b * nh + h) * nq + i, n_blk - 1)
            return (blk // per, blk % per, 0)

        cast_specs.append(pl.BlockSpec((None, w.shape[1] // per, w.shape[2]), block_of))
        cast_blocks.append(n_blk)
    attn_spec = pl.BlockSpec((None, tq, HEAD_DIM), lambda b, h, i: (b, i, h))
    res = pl.pallas_call(
        functools.partial(_attn_kernel, tq, lambda_init, tuple(cast_blocks)),
        grid=(bsz, nh, nq),
        in_specs=[attn_spec,
                  pl.BlockSpec((None, seq, HEAD_DIM), lambda b, h, i: (b, 0, nh + h)),
                  pl.BlockSpec((None, HEAD_DIM, seq), lambda b, h, i: (b, h, 0)),
                  vec, vec, vec, vec,
                  pl.BlockSpec((HEAD_DIM, 1), lambda b, h, i: (0, 0))] + cast_specs,
        out_specs=[attn_spec] + cast_specs,
        out_shape=[jax.ShapeDtypeStruct((bsz, seq, DIFF_WIDTH), BF16)]
                  + [jax.ShapeDtypeStruct(w.shape, BF16) for w in hosted],
        compiler_params=_cparams("arbitrary", "arbitrary", "arbitrary"), name="diff_attn",
    )(qk3, qk3, v_t, *[p.reshape(1, DIFF_QK_DIM).astype(F32) for p in lam_params],
      subln_w.reshape(HEAD_DIM, 1).astype(F32), *hosted)
    converted = iter(res[1:])
    copies = [next(converted) if _cast_blocks_per_expert(w, n_steps) else w.astype(BF16) for w in cast]
    return res[0].reshape(bsz * seq, DIFF_WIDTH), copies


HALO_ROWS = 16


def _conv_kernel(tiles_per_seq, x_ref, prev_ref, w_ref, o_ref):
    i = pl.program_id(0)
    j = pl.program_id(1)
    x = x_ref[...].astype(F32)
    prev = prev_ref[...].astype(F32)[HALO_ROWS - SUBLANES:]
    prev = jnp.where(i % tiles_per_seq == 0, jnp.zeros_like(prev), prev)
    w = w_ref[...]
    head_rows = lax.broadcasted_iota(jnp.int32, (SUBLANES, x.shape[1]), 0)
    y = x * w[CONV_WIDTH - 1:CONV_WIDTH, :]
    y_head = y[:SUBLANES]
    for back in range(1, CONV_WIDTH):
        wk = w[CONV_WIDTH - 1 - back:CONV_WIDTH - back, :]
        y = y + pltpu.roll(x, back, axis=0) * wk
        mixed = jnp.where(head_rows < back, pltpu.roll(prev, back, axis=0), pltpu.roll(x[:SUBLANES], back, axis=0))
        y_head = y_head + mixed * wk
    is_qk = j < 2 * GDN_WIDTH // x.shape[1]
    q_scale = jnp.where(j < GDN_WIDTH // x.shape[1], HEAD_DIM ** -0.5, 1.0)

    def finish(v):
        v = _silu(v)
        outs = []
        for h in range(v.shape[1] // HEAD_DIM):
            vh = v[:, h * HEAD_DIM:(h + 1) * HEAD_DIM]
            nrm = lax.rsqrt(jnp.sum(vh * vh, axis=-1, keepdims=True) + EPS) * q_scale
            outs.append(vh * jnp.where(is_qk, nrm, 1.0))
        return jnp.concatenate(outs, axis=1)

    o_ref[...] = finish(jnp.concatenate([y_head, y[SUBLANES:]], axis=0)).astype(o_ref.dtype)


def _gdn_conv(proj, conv_w, seq):
    t = proj.shape[0]
    tm = min(512, seq)
    cw = GDN_WIDTH
    nj = 3 * GDN_WIDTH // cw
    halo = tm // HALO_ROWS
    return pl.pallas_call(
        functools.partial(_conv_kernel, seq // tm),
        grid=(t // tm, nj),
        in_specs=[pl.BlockSpec((tm, cw), lambda i, j: (i, j)),
                  pl.BlockSpec((HALO_ROWS, cw), lambda i, j: (jnp.maximum(i * halo - 1, 0), j)),
                  pl.BlockSpec((CONV_WIDTH, cw), lambda i, j: (0, j))],
        out_specs=pl.BlockSpec((tm, cw), lambda i, j: (i, j)),
        out_shape=jax.ShapeDtypeStruct((t, 3 * GDN_WIDTH), BF16),
        compiler_params=_cparams("parallel", "parallel"), name="gdn_conv",
    )(proj, proj, conv_w.astype(F32))


def _softplus(x):
    return jnp.maximum(x, 0.0) + jnp.log(1.0 + jnp.exp(-jnp.abs(x)))


def _gdn_kernel(n_chunks, q_ref, k_ref, v_ref, z_ref, ab_ref, abt_ref, alog_ref, dtb_ref, alog_t_ref, dtb_t_ref,
                gw_ref, o_ref, state_ref):
    C = GDN_CHUNK
    nh = GDN_HEADS
    rows = n_chunks * C

    @pl.when(pl.program_id(1) == 0)
    def _():
        state_ref[...] = jnp.zeros_like(state_ref)

    neg_a = -jnp.exp(alog_ref[...])
    g_col = neg_a * _softplus(ab_ref[:, :LANES].astype(F32) + dtb_ref[...])
    g_row = -jnp.exp(alog_t_ref[...]) * _softplus(abt_ref[...] + dtb_t_ref[...])
    r_i = lax.broadcasted_iota(jnp.int32, (rows, rows), 0)
    c_i = lax.broadcasted_iota(jnp.int32, (rows, rows), 1)
    same = (r_i // C) == (c_i // C)
    tri_l = jnp.where(same & (c_i <= r_i), 1.0, 0.0).astype(F32)
    tri_u = jnp.where(same & (r_i <= c_i), 1.0, 0.0).astype(F32)
    gc_all = jnp.dot(tri_l, g_col, precision=HIGHEST, preferred_element_type=F32)
    gr_all = jnp.dot(g_row, tri_u, precision=HIGHEST, preferred_element_type=F32)
    b_all = _sigmoid(ab_ref[:, :LANES].astype(F32))

    incl = same & (r_i >= c_i)
    strict = same & (r_i > c_i)
    eye = jnp.where(r_i == c_i, 1.0, 0.0).astype(F32)
    gw = gw_ref[...]

    heads = range(nh)
    sls = [slice(h * HEAD_DIM, (h + 1) * HEAD_DIM) for h in heads]
    q = [q_ref[:, sl] for sl in sls]
    k = [k_ref[:, sl] for sl in sls]
    kf = [x.astype(F32) for x in k]
    gcol = [gc_all[:, h:h + 1] for h in heads]
    beta = [b_all[:, nh + h:nh + h + 1] for h in heads]
    decay = [jnp.exp(jnp.where(incl, gcol[h] - gr_all[h:h + 1, :], -jnp.inf)) for h in heads]
    pw = [jnp.where(strict, -(beta[h] * _dot_nt(k[h], k[h]) * decay[h]), 0.0) for h in heads]
    t_mat = [eye + p for p in pw]
    pw = [p.astype(BF16) for p in pw]
    for _ in range(int(math.log2(C)) - 1):
        pw = [_dot(p, p).astype(BF16) for p in pw]
        t_mat = [t + _dot(t.astype(BF16), p) for t, p in zip(t_mat, pw)]
    eg = [jnp.exp(g) for g in gcol]
    uw = [_dot(t_mat[h].astype(BF16),
               jnp.concatenate([v_ref[:, sls[h]].astype(F32) * beta[h], kf[h] * (beta[h] * eg[h])],
                               axis=1).astype(BF16)) for h in heads]
    u_all = [x[:, :HEAD_DIM] for x in uw]
    w_all = [x[:, HEAD_DIM:].astype(BF16) for x in uw]
    qk_all = [jnp.where(incl, _dot_nt(q[h], k[h]) * decay[h], 0.0).astype(BF16) for h in heads]
    qe_all = [(q[h].astype(F32) * eg[h]).astype(BF16) for h in heads]
    g_last = [[g[(c + 1) * C - 1:(c + 1) * C, :] for c in range(n_chunks)] for g in gcol]
    k_dec = [(kf[h] * jnp.exp(jnp.concatenate([jnp.broadcast_to(g, (C, 1)) for g in g_last[h]], axis=0)
                              - gcol[h])).astype(BF16) for h in heads]

    wq_all = [jnp.concatenate([x[c * C:(c + 1) * C] for c in range(n_chunks) for x in (w_all[h], qe_all[h])],
                              axis=0) for h in heads]

    state = [state_ref[h] for h in heads]
    for c in range(n_chunks):
        rs = slice(c * C, (c + 1) * C)
        sb = [s.astype(BF16) for s in state]
        ws_qs = [_dot(wq_all[h][2 * c * C:2 * (c + 1) * C], sb[h]) for h in heads]
        v_new = [(u_all[h][rs] - ws_qs[h][:C]).astype(BF16) for h in heads]
        o = [ws_qs[h][C:] + _dot(qk_all[h][rs, rs], v_new[h]) for h in heads]
        state = [state[h] * jnp.exp(g_last[h][c]) + _dot_tn(k_dec[h][rs], v_new[h]) for h in heads]
        for h in heads:
            gated = gw * _silu(z_ref[rs, sls[h]].astype(F32))
            o_h = o[h] * lax.rsqrt(jnp.mean(o[h] * o[h], axis=-1, keepdims=True) + EPS) * gated
            o_ref[rs, sls[h]] = o_h.astype(o_ref.dtype)
    for h in heads:
        state_ref[h] = state[h]


def _gdn(qkv, proj, bsz, seq, a_log, dt_bias, gnorm_w):
    t = bsz * seq
    n_chunks = min(4, seq // GDN_CHUNK)
    rows = n_chunks * GDN_CHUNK
    steps = seq // rows
    ab_off = SEG_U * SEG + AB_OFF
    abt = proj[:, ab_off:ab_off + 2 * SUBLANES].astype(F32).T

    def lane_vec(p):
        return jnp.zeros((1, LANES), F32).at[0, :GDN_HEADS].set(p.astype(F32))

    def sublane_vec(p):
        return jnp.zeros((2 * SUBLANES, 1), F32).at[:GDN_HEADS, 0].set(p.astype(F32))

    def rowblk(seg):
        return pl.BlockSpec((rows, SEG), lambda b, s: (b * steps + s, seg))

    return pl.pallas_call(
        functools.partial(_gdn_kernel, n_chunks),
        grid=(bsz, steps),
        in_specs=[rowblk(0), rowblk(1), rowblk(2),
                  pl.BlockSpec((rows, SEG), lambda b, s: (b * steps + s, SEG_GZ)),
                  pl.BlockSpec((rows, 2 * LANES), lambda b, s: (b * steps + s, ab_off // (2 * LANES))),
                  pl.BlockSpec((2 * SUBLANES, rows), lambda b, s: (0, b * steps + s)),
                  pl.BlockSpec((1, LANES), lambda b, s: (0, 0)),
                  pl.BlockSpec((1, LANES), lambda b, s: (0, 0)),
                  pl.BlockSpec((2 * SUBLANES, 1), lambda b, s: (0, 0)),
                  pl.BlockSpec((2 * SUBLANES, 1), lambda b, s: (0, 0)),
                  pl.BlockSpec((1, HEAD_DIM), lambda b, s: (0, 0))],
        out_specs=pl.BlockSpec((rows, GDN_WIDTH), lambda b, s: (b * steps + s, 0)),
        out_shape=jax.ShapeDtypeStruct((t, GDN_WIDTH), BF16),
        scratch_shapes=[pltpu.VMEM((GDN_HEADS, HEAD_DIM, HEAD_DIM), F32)],
        compiler_params=_cparams("parallel", "arbitrary"), name="gdn",
    )(qkv, qkv, qkv, proj, proj, abt, lane_vec(a_log), lane_vec(dt_bias),
      sublane_vec(a_log), sublane_vec(dt_bias), gnorm_w.reshape(1, HEAD_DIM).astype(F32))


def _outproj_kernel(with_router, a1_ref, a2_ref, a3_ref, w1_ref, w2_ref, w3_ref, h_ref, gate_ref,
                    nw_ref, scale_ref, shift_ref, *rest):
    if with_router:
        wr_hi_ref, wr_lo_ref, hres_ref, hn_ref, route_ref = rest
    else:
        hres_ref, hn_ref = rest
    mix = _dot(a1_ref[...], w1_ref[...]) + _dot(a2_ref[...], w2_ref[...]) + _dot(a3_ref[...], w3_ref[...])
    h = h_ref[...] + gate_ref[...] * mix
    hres_ref[...] = h
    hn = _rms_mod(h, nw_ref[...], scale_ref[...], shift_ref[...])
    hn_ref[...] = hn.astype(hn_ref.dtype)
    if with_router:
        hn_hi = hn.astype(BF16)
        hn_lo = (hn - hn_hi.astype(F32)).astype(BF16)
        logits = (_dot(hn_hi, wr_hi_ref[...]) + _dot(hn_lo, wr_hi_ref[...])) + _dot(hn_hi, wr_lo_ref[...])
        lane = lax.broadcasted_iota(jnp.int32, logits.shape, 1)
        lg = jnp.where(lane < N_EXPERTS, logits, -jnp.inf)
        m1 = jnp.max(lg, axis=-1, keepdims=True)
        i1 = jnp.min(jnp.where(lg == m1, lane, LANES), axis=-1, keepdims=True)
        lg2 = jnp.where(lane == i1, -jnp.inf, lg)
        m2 = jnp.max(lg2, axis=-1, keepdims=True)
        i2 = jnp.min(jnp.where(lg2 == m2, lane, LANES), axis=-1, keepdims=True)
        e = jnp.exp(m2 - m1)
        g1 = 1.0 / (1.0 + e)
        g2 = e / (1.0 + e)
        route_ref[...] = jnp.where(lane == 0, i1.astype(F32),
                                   jnp.where(lane == 1, i2.astype(F32),
                                             jnp.where(lane == 2, g1, jnp.where(lane == 3, g2, 0.0))))


def _outproj(a1, a2, a3, w_out, h_res, seq, gate, norm_w, scale, shift, w_router):
    t, d = h_res.shape
    tm = min(256, t)
    with_router = w_router is not None
    k1, k2 = a1.shape[1], a2.shape[1]
    w1, w2, w3 = w_out[:k1], w_out[k1:k1 + k2], w_out[k1 + k2:]

    def rows(w):
        return pl.BlockSpec((tm, w), lambda i: (i, 0))

    def whole(a):
        return pl.BlockSpec(a.shape, lambda i: (0, 0))

    per_batch = pl.BlockSpec((None, 1, d), lambda i: ((i * tm) // seq, 0, 0))
    nw = norm_w.reshape(1, d).astype(F32)
    args = [a1, a2, a3, w1, w2, w3, h_res, gate, nw, scale, shift]
    specs = [rows(k1), rows(k2), rows(a3.shape[1]), whole(w1), whole(w2), whole(w3), rows(d), per_batch,
             whole(nw), per_batch, per_batch]
    out_shape = [jax.ShapeDtypeStruct((t, d), F32), jax.ShapeDtypeStruct((t, d), BF16)]
    out_specs = [rows(d), rows(d)]
    if with_router:
        wr = jnp.zeros((d, LANES), F32).at[:, :N_EXPERTS].set(w_router.astype(F32))
        wr_hi = wr.astype(BF16)
        wr_lo = (wr - wr_hi.astype(F32)).astype(BF16)
        args += [wr_hi, wr_lo]
        specs += [whole(wr_hi), whole(wr_lo)]
        out_shape.append(jax.ShapeDtypeStruct((t, LANES), F32))
        out_specs.append(rows(LANES))
    return pl.pallas_call(
        functools.partial(_outproj_kernel, with_router),
        grid=(t // tm,), in_specs=specs, out_specs=out_specs, out_shape=out_shape,
        compiler_params=_cparams("parallel"), name="out_proj",
    )(*args)


FFN_SUB = 256


def _ffn_kernel(row_gated, fused_norm, has_mod, emit_res, te_ref, nv_ref, x_ref, w1_ref, w3_ref, w2_ref, *rest):
    rest = list(rest)
    rg_ref = rest.pop(0) if row_gated else None
    if fused_norm:
        h_ref, gate_ref, nw_ref = rest.pop(0), rest.pop(0), rest.pop(0)
        scale_ref, shift_ref = (rest.pop(0), rest.pop(0)) if has_mod else (None, None)
        hres_ref = rest.pop(0) if emit_res else None
    o_ref, acc_ref = rest
    i = pl.program_id(0)
    f = pl.program_id(1)

    @pl.when(f == 0)
    def _():
        acc_ref[...] = jnp.zeros_like(acc_ref)

    @pl.when(i < nv_ref[0])
    def _():
        x = x_ref[...]
        n_sub = w1_ref.shape[1] // FFN_SUB

        def up(s):
            cols = slice(s * FFN_SUB, (s + 1) * FFN_SUB)
            return _dot(x, w1_ref[:, cols]), _dot(x, w3_ref[:, cols])

        def down(s, h):
            return _dot((_silu(h[0]) * h[1]).astype(BF16), w2_ref[s * FFN_SUB:(s + 1) * FFN_SUB, :])

        h = up(0)
        total = acc_ref[...]
        for s in range(1, n_sub + 1):
            h_next = up(s) if s < n_sub else None
            total = total + down(s - 1, h)
            h = h_next
        acc_ref[...] = total

    @pl.when(f == pl.num_programs(1) - 1)
    def _():
        y = acc_ref[...]
        if row_gated:
            y = y * rg_ref[...]
        if fused_norm:
            hh = h_ref[...] + gate_ref[...] * y
            if emit_res:
                hres_ref[...] = hh
            scale = scale_ref[...] if has_mod else None
            shift = shift_ref[...] if has_mod else None
            y = _rms_mod(hh, nw_ref[...], scale, shift)
        o_ref[...] = y.astype(o_ref.dtype)


def _ffn(x, w1, w3, w2, tile_expert, n_valid, row_gate, tm, tf, norm=None):
    r, d = x.shape
    ff = w1.shape[2]
    tm, tf = min(tm, r), min(tf, ff)
    rows = pl.BlockSpec((tm, d), lambda i, f, te, nv: (i, 0))
    in_specs = [rows,
                pl.BlockSpec((None, d, tf), lambda i, f, te, nv: (te[i], 0, f)),
                pl.BlockSpec((None, d, tf), lambda i, f, te, nv: (te[i], 0, f)),
                pl.BlockSpec((None, tf, d), lambda i, f, te, nv: (te[i], f, 0))]
    args = [tile_expert, n_valid, x, w1, w3, w2]
    if row_gate is not None:
        in_specs.append(pl.BlockSpec((tm, 1), lambda i, f, te, nv: (i, 0)))
        args.append(row_gate)
    out_shape, out_specs = [], []
    has_mod = emit_res = False
    out_dtype = BF16
    if norm is not None:
        h_res, seq, gate, norm_w, scale, shift, emit_res, out_dtype = norm
        has_mod = scale is not None
        per_batch = pl.BlockSpec((None, 1, d), lambda i, f, te, nv: ((i * tm) // seq, 0, 0))
        in_specs += [rows, per_batch, pl.BlockSpec((1, d), lambda i, f, te, nv: (0, 0))]
        args += [h_res, gate, norm_w.reshape(1, d)]
        if has_mod:
            in_specs += [per_batch, per_batch]
            args += [scale, shift]
        if emit_res:
            out_shape.append(jax.ShapeDtypeStruct((r, d), F32))
            out_specs.append(rows)
    out_shape.append(jax.ShapeDtypeStruct((r, d), out_dtype))
    out_specs.append(rows)
    grid_spec = pltpu.PrefetchScalarGridSpec(
        num_scalar_prefetch=2, grid=(r // tm, ff // tf), in_specs=in_specs, out_specs=out_specs,
        scratch_shapes=[pltpu.VMEM((tm, d), F32)])
    res = pl.pallas_call(
        functools.partial(_ffn_kernel, row_gate is not None, norm is not None, has_mod, emit_res),
        grid_spec=grid_spec, out_shape=out_shape,
        compiler_params=_cparams("parallel", "arbitrary"), name="ffn",
    )(*args)
    return res if emit_res else res[0]


MOE_TILE = 512


def _moe_plan(route, tm):
    t = route.shape[0]
    e_flat = route[:, :2].astype(jnp.int32).reshape(-1)
    onehot = (e_flat[:, None] == jnp.arange(N_EXPERTS, dtype=jnp.int32)[None, :]).astype(jnp.int32)
    counts = jnp.sum(onehot, axis=0)
    rank = jnp.sum((jnp.cumsum(onehot, axis=0) - onehot) * onehot, axis=1)
    padded = ((counts + tm - 1) // tm) * tm
    ends = jnp.cumsum(padded)
    pos = (ends - padded)[e_flat] + rank
    n_rows = 2 * t + N_EXPERTS * tm
    n_tiles, n_blocks = n_rows // tm, t // tm
    tok_f = (jnp.arange(2 * t, dtype=jnp.int32) // 2).astype(F32)
    rows = jnp.stack([jnp.full((n_rows,), -1.0, F32), jnp.zeros((n_rows,), F32)], axis=1)
    rows = rows.at[pos].set(jnp.stack([tok_f, route[:, 2:4].reshape(-1)], axis=1))
    src_tok = rows[:, 0].astype(jnp.int32)
    row_gate = rows[:, 1]
    n_valid = (ends[-1] // tm).astype(jnp.int32)
    tile_start = jnp.arange(n_tiles, dtype=jnp.int32) * tm
    tile_expert = jnp.sum((tile_start[:, None] >= ends[None, :]).astype(jnp.int32), axis=1)
    last_expert = jnp.sum((tile_start[n_valid - 1] >= ends).astype(jnp.int32))
    tile_expert = jnp.where(tile_start < ends[-1], tile_expert, last_expert).astype(jnp.int32)

    blk = jnp.where(src_tok >= 0, src_tok // tm, -1).reshape(n_tiles, tm, 1)
    in_block = blk == jnp.arange(n_blocks, dtype=jnp.int32)[None, None, :]
    incidence = jnp.any(in_block, axis=1)
    row_sub = (jnp.arange(tm, dtype=jnp.int32) // COMBINE_SUB)[None, :, None]
    sub_lo = jnp.min(jnp.where(in_block, row_sub, tm // COMBINE_SUB), axis=1)
    sub_hi = jnp.max(jnp.where(in_block, row_sub, -1), axis=1)
    unused = (tile_start >= ends[-1])[:, None] & (jnp.arange(n_blocks) == 0)[None, :]
    w_max = n_tiles + N_EXPERTS * n_blocks + N_EXPERTS

    def work_list(m, *per_pair):
        flat = m.reshape(-1)
        n = jnp.sum(flat.astype(jnp.int32))
        idx = jnp.nonzero(flat, size=w_max, fill_value=0)[0].astype(jnp.int32)
        w = jnp.arange(w_max, dtype=jnp.int32)
        valid = w < n
        idx = jnp.where(valid, idx, idx[n - 1])
        major, minor = idx // m.shape[1], idx % m.shape[1]
        first = valid & ((w == 0) | (major != jnp.roll(major, 1)))
        last = valid & ((w == n - 1) | (major != jnp.roll(major, -1)))
        extra = [a.reshape(-1)[idx] for a in per_pair]
        return [a.astype(jnp.int32) for a in (major, minor, first, last, valid, *extra)]

    return (src_tok, row_gate.reshape(n_rows, 1), tile_expert, n_valid.reshape(1),
            work_list(incidence | unused, sub_lo, sub_hi), work_list(incidence.T, sub_lo.T, sub_hi.T))


def _dispatch_kernel(wi_ref, wj_ref, first_ref, last_ref, valid_ref, lo_ref, hi_ref, x_ref, tok_ref, o_ref):
    w = pl.program_id(0)
    tm = x_ref.shape[0]
    n_sub = tm // COMBINE_SUB
    window = 2 * COMBINE_SUB

    @pl.when(valid_ref[w] == 1)
    def _():
        @pl.when(first_ref[w] == 1)
        def _():
            o_ref[...] = jnp.zeros_like(o_ref)

        def picked(tok_col):
            col_tok = lax.broadcasted_iota(jnp.int32, (tok_col.shape[0], tm), 1) + wj_ref[w] * tm
            onehot = jnp.where(tok_col == col_tok, 1.0, 0.0).astype(BF16)
            return _dot(onehot, x_ref[...]).astype(o_ref.dtype)

        first_sub = jnp.minimum(lo_ref[w], n_sub - 2)
        narrow = hi_ref[w] <= first_sub + 1

        @pl.when(narrow)
        def _():
            rows = pl.ds(pl.multiple_of(first_sub * COMBINE_SUB, COMBINE_SUB), window)
            o_ref[rows, :] += picked(tok_ref[rows, :])

        @pl.when(jnp.logical_not(narrow))
        def _():
            o_ref[...] += picked(tok_ref[...])


def _dispatch(x, src_tok, work, tm):
    t, d = x.shape
    n_rows = src_tok.shape[0]
    grid_spec = pltpu.PrefetchScalarGridSpec(
        num_scalar_prefetch=len(work), grid=(work[0].shape[0],),
        in_specs=[pl.BlockSpec((tm, d), lambda w, wi, wj, *_: (wj[w], 0)),
                  pl.BlockSpec((tm, 1), lambda w, wi, wj, *_: (wi[w], 0))],
        out_specs=pl.BlockSpec((tm, d), lambda w, wi, wj, *_: (wi[w], 0)))
    return pl.pallas_call(
        _dispatch_kernel, grid_spec=grid_spec,
        out_shape=jax.ShapeDtypeStruct((n_rows, d), x.dtype),
        compiler_params=_cparams("arbitrary"), name="moe_dispatch",
    )(*work, x, src_tok.reshape(n_rows, 1))


COMBINE_SUB = LANES


def _combine_kernel(has_mod, emit_res, vj_ref, vi_ref, first_ref, last_ref, valid_ref, lo_ref, hi_ref,
                    y_ref, tok_ref, h_ref, gate_ref, nw_ref, *rest):
    rest = list(rest)
    scale_ref = shift_ref = None
    if has_mod:
        scale_ref, shift_ref = rest.pop(0), rest.pop(0)
    hres_ref = rest.pop(0) if emit_res else None
    o_ref, acc_ref = rest
    w = pl.program_id(0)
    tm = y_ref.shape[0]

    n_sub = tm // COMBINE_SUB
    window = 2 * COMBINE_SUB

    @pl.when(valid_ref[w] == 1)
    def _():
        @pl.when(first_ref[w] == 1)
        def _():
            acc_ref[...] = jnp.zeros_like(acc_ref)

        def gathered(tok_row, rows):
            row_tok = lax.broadcasted_iota(jnp.int32, (tm, tok_row.shape[1]), 0) + vj_ref[w] * tm
            return _dot(jnp.where(tok_row == row_tok, 1.0, 0.0).astype(BF16), rows)

        first_sub = jnp.minimum(lo_ref[w], n_sub - 2)
        narrow = hi_ref[w] <= first_sub + 1

        @pl.when(narrow)
        def _():
            toks = tok_ref[pl.ds(first_sub, 2)]
            rows = y_ref[pl.ds(pl.multiple_of(first_sub * COMBINE_SUB, COMBINE_SUB), window), :]
            acc_ref[...] += gathered(jnp.concatenate([toks[0], toks[1]], axis=1), rows)

        @pl.when(jnp.logical_not(narrow))
        def _():
            toks = tok_ref[...]
            acc_ref[...] += gathered(jnp.concatenate([toks[s] for s in range(n_sub)], axis=1), y_ref[...])

        @pl.when(last_ref[w] == 1)
        def _():
            h = h_ref[...] + gate_ref[...] * acc_ref[...]
            if emit_res:
                hres_ref[...] = h
            scale = scale_ref[...] if has_mod else None
            shift = shift_ref[...] if has_mod else None
            o_ref[...] = _rms_mod(h, nw_ref[...], scale, shift).astype(o_ref.dtype)


def _combine_resnorm(ys, src_tok, work, tm, h_res, seq, gate, norm_w, scale, shift, emit_res, out_dtype):
    t, d = h_res.shape
    n_tiles = ys.shape[0] // tm
    has_mod = scale is not None

    def tok_rows(w, vj, vi, *_):
        return (vj[w], 0)

    per_batch = pl.BlockSpec((None, 1, d), lambda w, vj, *_: ((vj[w] * tm) // seq, 0, 0))
    in_specs = [pl.BlockSpec((tm, d), lambda w, vj, vi, *_: (vi[w], 0)),
                pl.BlockSpec((None, tm // COMBINE_SUB, 1, COMBINE_SUB), lambda w, vj, vi, *_: (vi[w], 0, 0, 0)),
                pl.BlockSpec((tm, d), tok_rows), per_batch,
                pl.BlockSpec((1, d), lambda w, *_: (0, 0))]
    args = list(work) + [ys, src_tok.reshape(n_tiles, tm // COMBINE_SUB, 1, COMBINE_SUB), h_res, gate,
                         norm_w.reshape(1, d)]
    if has_mod:
        in_specs += [per_batch, per_batch]
        args += [scale, shift]
    out_shape, out_specs = [], []
    if emit_res:
        out_shape.append(jax.ShapeDtypeStruct((t, d), F32))
        out_specs.append(pl.BlockSpec((tm, d), tok_rows))
    out_shape.append(jax.ShapeDtypeStruct((t, d), out_dtype))
    out_specs.append(pl.BlockSpec((tm, d), tok_rows))
    grid_spec = pltpu.PrefetchScalarGridSpec(
        num_scalar_prefetch=len(work), grid=(work[0].shape[0],), in_specs=in_specs, out_specs=out_specs,
        scratch_shapes=[pltpu.VMEM((tm, d), F32)])
    res = pl.pallas_call(
        functools.partial(_combine_kernel, has_mod, emit_res), grid_spec=grid_spec, out_shape=out_shape,
        compiler_params=_cparams("arbitrary"), name="moe_combine",
    )(*args)
    return res if emit_res else res[0]


def kernel(x, c, positions, w_ada, b_ada, norm_mix, norm_ffn, norm_final, w_in, w_out, ssm_a_re, ssm_a_im, ssm_log_dt, ssm_b_re, ssm_b_im, ssm_c_re, ssm_c_im, ssm_d, ssm_w_glu, diff_lam_q1, diff_lam_k1, diff_lam_q2, diff_lam_k2, diff_subln, gdn_conv, gdn_a_log, gdn_dt_bias, gdn_norm, ffn_w1, ffn_w3, ffn_w2, moe_router, moe_w1, moe_w3, moe_w2):
    bsz, seq, d = x.shape
    t = bsz * seq
    depth = w_in.shape[0]
    h_res = x.astype(F32).reshape(t, d)

    c_pad = jnp.zeros((SUBLANES, d), F32).at[:bsz].set(c.astype(F32))
    mod = _ada(c_pad, w_ada, b_ada)[:, :bsz]
    mods = [[m.reshape(bsz, 1, d) for m in jnp.split(mod[l], 6, axis=-1)] for l in range(depth)]
    rope_tables = _rope_tables(positions)
    scan_steps = max(1, math.ceil(math.log2(seq // SSM_CHUNK)))
    s5_tables = jax.vmap(functools.partial(_s5_tables, n_steps=scan_steps))(
        ssm_a_re, ssm_a_im, ssm_log_dt, ssm_b_re, ssm_b_im, ssm_c_re, ssm_c_im)

    hn = _resnorm(h_res, seq, None, None, norm_mix[0].astype(F32), mods[0][1], mods[0][0], False, BF16)
    out = None
    moe_bf16 = []
    for l in range(depth):
        shift1, scale1, gate1, shift2, scale2, gate2 = mods[l]
        is_moe = l % 2 == 1
        proj = _in_proj(hn, w_in[l])

        y_ssm = _s5_mixer(proj, bsz, seq, [tab[l] for tab in s5_tables], ssm_d[l], ssm_w_glu[l].astype(BF16))

        lambda_init = 0.8 - 0.6 * math.exp(-0.3 * l)
        qk = _rope(proj, rope_tables)
        if is_moe:
            to_cast = (moe_w2[l // 2],)
        elif l + 1 < depth:
            to_cast = (moe_w1[(l + 1) // 2], moe_w3[(l + 1) // 2])
        else:
            to_cast = ()
        y_diff, copies = _diff_attention(qk, proj, bsz, seq,
                                         (diff_lam_q1[l], diff_lam_k1[l], diff_lam_q2[l], diff_lam_k2[l]),
                                         diff_subln[l], lambda_init, to_cast)
        if is_moe:
            moe_bf16 = moe_bf16 + copies
        elif copies:
            moe_bf16 = copies

        qkv = _gdn_conv(proj, gdn_conv[l], seq)
        y_gdn = _gdn(qkv, proj, bsz, seq, gdn_a_log[l], gdn_dt_bias[l], gdn_norm[l])

        res = _outproj(y_ssm, y_diff, y_gdn, w_out[l].astype(BF16), h_res, seq, gate1,
                       norm_ffn[l], scale2, shift2, moe_router[l // 2] if is_moe else None)
        h_res, hn2 = res[0], res[1]

        last = l + 1 == depth
        if last:
            nxt = (norm_final.astype(F32), None, None, False, x.dtype)
        else:
            nxt = (norm_mix[l + 1].astype(F32), mods[l + 1][1], mods[l + 1][0], True, BF16)
        if is_moe:
            src_tok, row_gate, tile_expert, n_valid, work_sorted, work_token = _moe_plan(res[2], MOE_TILE)
            xs = _dispatch(hn2, src_tok, work_sorted, MOE_TILE)
            ys = _ffn(xs, *moe_bf16, tile_expert, n_valid, row_gate, MOE_TILE, 1024)
            res = _combine_resnorm(ys, src_tok, work_token, MOE_TILE, h_res, seq, gate2, *nxt)
        else:
            n_tiles = t // min(512, t)
            res = _ffn(hn2, ffn_w1[l // 2:l // 2 + 1].astype(BF16), ffn_w3[l // 2:l // 2 + 1].astype(BF16),
                       ffn_w2[l // 2:l // 2 + 1].astype(BF16), jnp.zeros((n_tiles,), jnp.int32),
                       jnp.full((1,), n_tiles, jnp.int32), None, 512, 512,
                       norm=(h_res, seq, gate2) + nxt)
        if last:
            out = res
        else:
            h_res, hn = res
    return out.reshape(bsz, seq, d)
```

```python
import functools
import math

import numpy as np
import jax
import jax.numpy as jnp
from jax import lax
from jax.experimental import pallas as pl
from jax.experimental.pallas import tpu as pltpu

F32 = jnp.float32
BF16 = jnp.bfloat16
HIGHEST = lax.Precision.HIGHEST

D_MODEL = 2048
SSM_WIDTH = 512
SSM_CH = 16
SSM_GROUPS = SSM_WIDTH // SSM_CH
SSM_STATE = 64
SSM_CHUNK = 16
DIFF_WIDTH = 768
HEAD_DIM = 128
DIFF_HEADS = DIFF_WIDTH // HEAD_DIM
DIFF_QK_DIM = HEAD_DIM // 2
GDN_WIDTH = 768
GDN_HEADS = GDN_WIDTH // HEAD_DIM
CONV_WIDTH = 4
GDN_CHUNK = 64
ROPE_THETA = 500000.0
ROPE_DIM = DIFF_QK_DIM // 4
ROPE_HALF = ROPE_DIM // 2
N_EXPERTS = 8
EPS = 1e-6
LANES = 128
SUBLANES = 8

SEG = 768
SEG_GQ, SEG_GK, SEG_GV, SEG_GZ, SEG_DQ, SEG_DK, SEG_DV, SEG_U = range(8)
PROJ_WIDTH = 8 * SEG
AB_OFF = SSM_WIDTH

VMEM_LIMIT = 56 * 1024 * 1024


def _cparams(*sem):
    return pltpu.CompilerParams(dimension_semantics=sem, vmem_limit_bytes=VMEM_LIMIT)


def _dot(a, b):
    return jnp.dot(a, b, preferred_element_type=F32)


def _dot_nt(a, b):
    return lax.dot_general(a, b, (((1,), (1,)), ((), ())), preferred_element_type=F32)


def _dot_tn(a, b):
    return lax.dot_general(a, b, (((0,), (0,)), ((), ())), preferred_element_type=F32)


def _sigmoid(x):
    return 1.0 / (1.0 + jnp.exp(-x))


def _silu(x):
    return x * _sigmoid(x)


def _ada_kernel(c_ref, w_ref, b_ref, o_ref):
    cond = _silu(c_ref[...])
    o_ref[...] = _dot(cond.astype(BF16), w_ref[...].astype(BF16)) + b_ref[...]


def _ada(c_pad, w_ada, b_ada):
    depth, d, n = w_ada.shape
    tn = 1024
    return pl.pallas_call(
        _ada_kernel,
        grid=(depth, n // tn),
        in_specs=[pl.BlockSpec((SUBLANES, d), lambda l, j: (0, 0)),
                  pl.BlockSpec((None, d, tn), lambda l, j: (l, 0, j)),
                  pl.BlockSpec((None, 1, tn), lambda l, j: (l, 0, j))],
        out_specs=pl.BlockSpec((None, SUBLANES, tn), lambda l, j: (l, 0, j)),
        out_shape=jax.ShapeDtypeStruct((depth, SUBLANES, n), F32),
        compiler_params=_cparams("arbitrary", "arbitrary"),
        name="ada",
    )(c_pad, w_ada, b_ada.reshape(depth, 1, n))


def _rms_mod(h, w, scale, shift):
    y = h * lax.rsqrt(jnp.mean(h * h, axis=-1, keepdims=True) + EPS) * w
    if scale is not None:
        y = y * (1.0 + scale) + shift
    return y


def _resnorm_kernel(has_delta, has_mod, emit_res, *refs):
    refs = list(refs)
    h = refs.pop(0)[...]
    if has_delta:
        y = refs.pop(0)[...].astype(F32)
        h = h + refs.pop(0)[...] * y
    w = refs.pop(0)[...]
    scale = shift = None
    if has_mod:
        scale = refs.pop(0)[...]
        shift = refs.pop(0)[...]
    if emit_res:
        refs.pop(0)[...] = h
    o_ref = refs.pop(0)
    o_ref[...] = _rms_mod(h, w, scale, shift).astype(o_ref.dtype)


def _resnorm(h_res, seq, delta, gate, norm_w, scale, shift, emit_res, out_dtype):
    t, d = h_res.shape
    tm = min(256, t)
    row = pl.BlockSpec((tm, d), lambda i: (i, 0))
    per_batch = pl.BlockSpec((None, 1, d), lambda i: ((i * tm) // seq, 0, 0))
    args, specs = [h_res], [row]
    if delta is not None:
        args += [delta, gate]
        specs += [row, per_batch]
    args.append(norm_w.reshape(1, d))
    specs.append(pl.BlockSpec((1, d), lambda i: (0, 0)))
    if scale is not None:
        args += [scale, shift]
        specs += [per_batch, per_batch]
    out_shape, out_specs = [], []
    if emit_res:
        out_shape.append(jax.ShapeDtypeStruct((t, d), F32))
        out_specs.append(row)
    out_shape.append(jax.ShapeDtypeStruct((t, d), out_dtype))
    out_specs.append(row)
    res = pl.pallas_call(
        functools.partial(_resnorm_kernel, delta is not None, scale is not None, emit_res),
        grid=(t // tm,), in_specs=specs, out_specs=out_specs, out_shape=out_shape,
        compiler_params=_cparams("parallel"), name="resnorm",
    )(*args)
    return res if emit_res else res[0]


IN_BLOCK = 2 * LANES


def _in_proj_blocks():
    src_segments = [(SEG_U, SSM_WIDTH), (SEG_DQ, DIFF_WIDTH), (SEG_DK, DIFF_WIDTH), (SEG_DV, DIFF_WIDTH),
                    (SEG_GQ, 3 * GDN_WIDTH), (SEG_GZ, GDN_WIDTH)]
    dest = []
    for seg, width in src_segments:
        dest += [seg * SEG // IN_BLOCK + b for b in range(width // IN_BLOCK)]
    dest.append((SEG_U * SEG + AB_OFF) // IN_BLOCK)
    return np.asarray(dest, np.int32)


def _in_proj_kernel(n_cols, dest_ref, x_ref, w_ref, o_ref, wb_ref):
    j = pl.program_id(1)

    @pl.when(pl.program_id(0) == 0)
    def _():
        col = lax.broadcasted_iota(jnp.int32, w_ref.shape, 1) + j * IN_BLOCK
        wb_ref[j] = jnp.where(col < n_cols, w_ref[...], 0.0).astype(BF16)

    o_ref[...] = _dot(x_ref[...], wb_ref[j]).astype(o_ref.dtype)


def _in_proj(x, w_in_l):
    m, k = x.shape
    n_cols = w_in_l.shape[1]
    dest = _in_proj_blocks()
    n_blocks = dest.shape[0]
    tm = min(2048, m)
    grid_spec = pltpu.PrefetchScalarGridSpec(
        num_scalar_prefetch=1, grid=(m // tm, n_blocks),
        in_specs=[pl.BlockSpec((tm, k), lambda i, j, dest: (i, 0)),
                  pl.BlockSpec((k, IN_BLOCK), lambda i, j, dest: (0, jnp.where(i == 0, j, n_blocks - 1)))],
        out_specs=pl.BlockSpec((tm, IN_BLOCK), lambda i, j, dest: (i, dest[j])),
        scratch_shapes=[pltpu.VMEM((n_blocks, k, IN_BLOCK), BF16)])
    return pl.pallas_call(
        functools.partial(_in_proj_kernel, n_cols), grid_spec=grid_spec,
        out_shape=jax.ShapeDtypeStruct((m, PROJ_WIDTH), BF16),
        compiler_params=_cparams("arbitrary", "arbitrary"), name="in_proj",
    )(jnp.asarray(dest), x, w_in_l)


def _s5_tables(a_re, a_im, log_dt, b_re, b_im, c_re, c_im, n_steps):
    L, G, P, H = SSM_CHUNK, SSM_GROUPS, SSM_STATE, SSM_CH
    lam = lax.complex(a_re.astype(F32), a_im.astype(F32))
    log_lam_bar = lam * jnp.exp(log_dt.astype(F32))[:, None]
    lam_bar = jnp.exp(log_lam_bar)
    b_bar = ((lam_bar - 1.0) / lam)[:, :, None] * lax.complex(b_re.astype(F32), b_im.astype(F32))
    c_mat = lax.complex(c_re.astype(F32), c_im.astype(F32))
    steps = jnp.arange(L + 1, dtype=F32)
    pw = jnp.exp(log_lam_bar[:, None, :] * steps[None, :, None])
    kern = jnp.real(jnp.einsum('ghp,gjp,gpi->gjih', c_mat, pw[:, :L], b_bar))
    lag = jnp.arange(L)[None, :] - jnp.arange(L)[:, None]
    tm = jnp.where((lag >= 0)[None, :, :, None, None], kern[:, jnp.clip(lag, 0, L - 1)], 0.0)
    tm = tm.transpose(0, 2, 4, 1, 3).reshape(G, L * H, L * H)
    zc = pw[:, L - 1 - jnp.arange(L)][:, :, :, None] * b_bar[:, None]
    zc = zc.transpose(0, 2, 1, 3).reshape(G, P, L * H)
    zm = jnp.concatenate([jnp.real(zc), jnp.imag(zc)], axis=1)
    cl = (c_mat[:, None] * pw[:, 1:L + 1][:, :, None, :]).reshape(G, L * H, P)
    ym = jnp.concatenate([jnp.real(cl), -jnp.imag(cl)], axis=2)
    a_pows = jnp.exp(log_lam_bar[:, None, :] * (L * 2.0 ** jnp.arange(n_steps, dtype=F32))[None, :, None])
    return (tm.astype(BF16), zm.astype(BF16), ym.astype(BF16),
            jnp.real(a_pows)[..., None], jnp.imag(a_pows)[..., None])


def _s5_core_kernel(n_steps, chunks_per_seq, u_ref, tm_ref, zm_ref, ym_ref, are_ref, aim_ref, o_ref):
    p = SSM_STATE
    n_pos, n_ch, n_lanes = u_ref.shape
    u = u_ref[...].reshape(n_pos * n_ch, n_lanes)
    z = _dot(zm_ref[...], u)
    x_re, x_im = z[:p], z[p:]
    c_idx = lax.broadcasted_iota(jnp.int32, x_re.shape, 1) % chunks_per_seq
    for k in range(n_steps):
        d = 1 << k
        a_re, a_im = are_ref[k], aim_ref[k]
        inside = c_idx >= d
        s_re = jnp.where(inside, pltpu.roll(x_re, d, axis=1), 0.0)
        s_im = jnp.where(inside, pltpu.roll(x_im, d, axis=1), 0.0)
        x_re, x_im = x_re + (a_re * s_re - a_im * s_im), x_im + (a_re * s_im + a_im * s_re)
    inside = c_idx >= 1
    x_prev = jnp.concatenate([jnp.where(inside, pltpu.roll(x_re, 1, axis=1), 0.0),
                              jnp.where(inside, pltpu.roll(x_im, 1, axis=1), 0.0)], axis=0)
    y = _dot(tm_ref[...], u) + _dot(ym_ref[...], x_prev.astype(BF16))
    o_ref[...] = y.astype(o_ref.dtype).reshape(o_ref.shape)


def _s5_core(u_t, tables, chunks_per_seq):
    n_pos, g, n_ch, r = u_t.shape
    tm, zm, ym, are, aim = tables
    p = SSM_STATE
    w = n_pos * n_ch
    n_steps = are.shape[1]

    def grp(*shape):
        return pl.BlockSpec((None,) + shape, lambda i: (i,) + (0,) * len(shape))

    slab = pl.BlockSpec((n_pos, None, n_ch, r), lambda i: (0, i, 0, 0))
    return pl.pallas_call(
        functools.partial(_s5_core_kernel, n_steps, chunks_per_seq),
        grid=(g,),
        in_specs=[slab, grp(w, w), grp(2 * p, w), grp(w, 2 * p), grp(n_steps, p, 1), grp(n_steps, p, 1)],
        out_specs=slab,
        out_shape=jax.ShapeDtypeStruct(u_t.shape, BF16),
        compiler_params=_cparams("parallel"), name="s5_core",
    )(u_t, tm, zm, ym, are, aim)


def _gelu_tanh(x):
    return 0.5 * x * (1.0 + jnp.tanh(math.sqrt(2.0 / math.pi) * (x + 0.044715 * (x * x * x))))


def _s5_post_kernel(y_ref, u_ref, d_ref, w_ref, o_ref):
    u = u_ref[:, :SSM_WIDTH].astype(F32)
    y = _gelu_tanh(y_ref[...].astype(F32) + d_ref[...] * u)
    o_ref[...] = (y * _sigmoid(_dot(y.astype(BF16), w_ref[...]))).astype(o_ref.dtype)


def _s5_post(y_core, proj, d_skip, w_glu):
    t = y_core.shape[0]
    tm = min(512, t)
    return pl.pallas_call(
        _s5_post_kernel,
        grid=(t // tm,),
        in_specs=[pl.BlockSpec((tm, SSM_WIDTH), lambda i: (i, 0)),
                  pl.BlockSpec((tm, SEG), lambda i: (i, SEG_U)),
                  pl.BlockSpec((1, SSM_WIDTH), lambda i: (0, 0)),
                  pl.BlockSpec((SSM_WIDTH, SSM_WIDTH), lambda i: (0, 0))],
        out_specs=pl.BlockSpec((tm, SSM_WIDTH), lambda i: (i, 0)),
        out_shape=jax.ShapeDtypeStruct((t, SSM_WIDTH), BF16),
        compiler_params=_cparams("parallel"), name="s5_post",
    )(y_core, proj, d_skip.reshape(1, SSM_WIDTH).astype(F32), w_glu)


def _s5_mixer(proj, bsz, seq, tables, d_skip, w_glu):
    L, G, H = SSM_CHUNK, SSM_GROUPS, SSM_CH
    t = bsz * seq
    nc = seq // L
    u = proj[:, SEG_U * SEG:SEG_U * SEG + SSM_WIDTH]
    u_t = u.reshape(t // L, L * G * H).T.reshape(L, G, H, t // L)
    y_t = _s5_core(u_t, tables, nc)
    y_core = y_t.reshape(L * G * H, t // L).T.reshape(t, SSM_WIDTH)
    return _s5_post(y_core, proj, d_skip, w_glu)


def _rope_tables(positions):
    inv_freq = ROPE_THETA ** (-jnp.arange(0, ROPE_DIM, 2, dtype=F32) / ROPE_DIM)
    ang = positions.astype(F32).reshape(-1)[:, None] * inv_freq
    cos, sin = jnp.cos(ang), jnp.sin(ang)
    r = np.arange(LANES) % DIFF_QK_DIM
    first = jnp.asarray(r < ROPE_HALF)[None, :]
    second = jnp.asarray((r >= ROPE_HALF) & (r < ROPE_DIM))[None, :]
    idx = jnp.asarray(r % ROPE_HALF)
    cos_l, sin_l = cos[:, idx], sin[:, idx]
    cosf = jnp.where(first | second, cos_l, 1.0)
    sin_a = jnp.where(first, -sin_l, 0.0)
    sin_b = jnp.where(second, sin_l, 0.0)
    return cosf, sin_a, sin_b


def _rope_kernel(x_ref, c_ref, sa_ref, sb_ref, o_ref):
    cosf, sin_a, sin_b = c_ref[...], sa_ref[...], sb_ref[...]
    n_slabs = x_ref.shape[1] // LANES
    for s in range(n_slabs):
        x = x_ref[:, s * LANES:(s + 1) * LANES].astype(F32)
        y = (x * cosf + pltpu.roll(x, LANES - ROPE_HALF, axis=1) * sin_a
             + pltpu.roll(x, ROPE_HALF, axis=1) * sin_b)
        if s < n_slabs // 2:
            y = y * (DIFF_QK_DIM ** -0.5 * math.log2(math.e))
        o_ref[:, s * LANES:(s + 1) * LANES] = y.astype(o_ref.dtype)


def _rope(proj, tables):
    t = proj.shape[0]
    tm = min(512, t)
    w = 2 * DIFF_WIDTH
    tab = pl.BlockSpec((tm, LANES), lambda i: (i, 0))
    return pl.pallas_call(
        _rope_kernel,
        grid=(t // tm,),
        in_specs=[pl.BlockSpec((tm, w), lambda i: (i, SEG_DQ // 2)), tab, tab, tab],
        out_specs=pl.BlockSpec((tm, w), lambda i: (i, 0)),
        out_shape=jax.ShapeDtypeStruct((t, w), BF16),
        compiler_params=_cparams("parallel"), name="rope",
    )(proj, *tables)


def _attn_kernel(tq, lambda_init, cast_blocks, q_ref, k_ref, vt_ref, lq1_ref, lk1_ref, lq2_ref, lk2_ref, sw_ref,
                 *rest):
    n_cast = len(cast_blocks)
    cast_in, o_ref, cast_out = rest[:n_cast], rest[n_cast], rest[n_cast + 1:]
    step = (pl.program_id(0) * pl.num_programs(1) + pl.program_id(1)) * pl.num_programs(2) + pl.program_id(2)
    for n_blk, src, dst in zip(cast_blocks, cast_in, cast_out):
        @pl.when(step < n_blk)
        def _(src=src, dst=dst):
            dst[...] = src[...].astype(dst.dtype)

    qi = pl.program_id(2)
    q = q_ref[...]
    lane = lax.broadcasted_iota(jnp.int32, q.shape, 1)
    zero = jnp.zeros_like(q)
    qm = (jnp.where(lane < DIFF_QK_DIM, q, zero), jnp.where(lane >= DIFF_QK_DIM, q, zero))

    def block(kv, masked, carry):
        start = pl.multiple_of(kv * tq, tq)
        k = k_ref[pl.ds(start, tq), :]
        v_t = vt_ref[:, pl.ds(start, tq)]
        scores = [_dot_nt(k, qm[i]) for i in range(2)]
        new = []
        for i in range(2):
            m_old, l_old, acc = carry[i]
            s = scores[i]
            if masked:
                key = lax.broadcasted_iota(jnp.int32, s.shape, 0)
                qry = lax.broadcasted_iota(jnp.int32, s.shape, 1)
                s = jnp.where(key <= qry, s, -jnp.inf)
            m_new = jnp.maximum(m_old, jnp.max(s, axis=0, keepdims=True))
            alpha = jnp.exp2(m_old - m_new)
            p = jnp.exp2(s - m_new)
            l_new = alpha * l_old + jnp.sum(p, axis=0, keepdims=True)
            new.append((m_new, l_new, alpha * acc + _dot(v_t, p.astype(BF16))))
        return tuple(new)

    init = tuple((jnp.full((1, tq), -jnp.inf, F32), jnp.zeros((1, tq), F32), jnp.zeros((HEAD_DIM, tq), F32))
                 for _ in range(2))
    carry = lax.fori_loop(0, qi, lambda kv, c: block(kv, False, c), init)
    (_, l1, acc1), (_, l2, acc2) = block(qi, True, carry)
    lam = (jnp.exp(jnp.sum(lq1_ref[...] * lk1_ref[...], axis=-1, keepdims=True))
           - jnp.exp(jnp.sum(lq2_ref[...] * lk2_ref[...], axis=-1, keepdims=True)) + lambda_init)
    o_t = acc1 / l1 - lam * (acc2 / l2)
    o_t = o_t * lax.rsqrt(jnp.mean(o_t * o_t, axis=0, keepdims=True) + EPS) * sw_ref[...] * (1.0 - lambda_init)
    o_ref[...] = o_t.T.astype(o_ref.dtype)


def _cast_blocks_per_expert(w, n_steps):
    e, r, _ = w.shape
    per = 1
    while e * per * 2 <= n_steps and r % (per * 2) == 0 and (r // (per * 2)) % (2 * SUBLANES) == 0:
        per *= 2
    return per if e * per <= n_steps and (r // per) % (2 * SUBLANES) == 0 else 0


def _diff_attention(qk, proj, bsz, seq, lam_params, subln_w, lambda_init, cast=()):
    tq = min(512, seq)
    nh = DIFF_HEADS
    nq = seq // tq
    n_steps = bsz * nh * nq
    qk3 = qk.reshape(bsz, seq, 2 * DIFF_WIDTH)
    v_t = proj[:, SEG_DV * SEG:(SEG_DV + 1) * SEG].reshape(bsz, seq, DIFF_WIDTH).transpose(0, 2, 1)
    vec = pl.BlockSpec((1, DIFF_QK_DIM), lambda b, h, i: (0, 0))
    hosted = [w for w in cast if _cast_blocks_per_expert(w, n_steps)]
    cast_specs, cast_blocks = [], []
    for w in hosted:
        per = _cast_blocks_per_expert(w, n_steps)
        n_blk = w.shape[0] * per

        def block_of(b, h, i, per=per, n_blk=n_blk):
            blk = jnp.minimum((b * nh + h) * nq + i, n_blk - 1)
            return (blk // per, blk % per, 0)

        cast_specs.append(pl.BlockSpec((None, w.shape[1] // per, w.shape[2]), block_of))
        cast_blocks.append(n_blk)
    attn_spec = pl.BlockSpec((None, tq, HEAD_DIM), lambda b, h, i: (b, i, h))
    res = pl.pallas_call(
        functools.partial(_attn_kernel, tq, lambda_init, tuple(cast_blocks)),
        grid=(bsz, nh, nq),
        in_specs=[attn_spec,
                  pl.BlockSpec((None, seq, HEAD_DIM), lambda b, h, i: (b, 0, nh + h)),
                  pl.BlockSpec((None, HEAD_DIM, seq), lambda b, h, i: (b, h, 0)),
                  vec, vec, vec, vec,
                  pl.BlockSpec((HEAD_DIM, 1), lambda b, h, i: (0, 0))] + cast_specs,
        out_specs=[attn_spec] + cast_specs,
        out_shape=[jax.ShapeDtypeStruct((bsz, seq, DIFF_WIDTH), BF16)]
                  + [jax.ShapeDtypeStruct(w.shape, BF16) for w in hosted],
        compiler_params=_cparams("arbitrary", "arbitrary", "arbitrary"), name="diff_attn",
    )(qk3, qk3, v_t, *[p.reshape(1, DIFF_QK_DIM).astype(F32) for p in lam_params],
      subln_w.reshape(HEAD_DIM, 1).astype(F32), *hosted)
    converted = iter(res[1:])
    copies = [next(converted) if _cast_blocks_per_expert(w, n_steps) else w.astype(BF16) for w in cast]
    return res[0].reshape(bsz * seq, DIFF_WIDTH), copies


HALO_ROWS = 16


def _conv_kernel(tiles_per_seq, x_ref, prev_ref, w_ref, o_ref):
    i = pl.program_id(0)
    j = pl.program_id(1)
    x = x_ref[...].astype(F32)
    prev = prev_ref[...].astype(F32)[HALO_ROWS - SUBLANES:]
    prev = jnp.where(i % tiles_per_seq == 0, jnp.zeros_like(prev), prev)
    w = w_ref[...]
    head_rows = lax.broadcasted_iota(jnp.int32, (SUBLANES, x.shape[1]), 0)
    y = x * w[CONV_WIDTH - 1:CONV_WIDTH, :]
    y_head = y[:SUBLANES]
    for back in range(1, CONV_WIDTH):
        wk = w[CONV_WIDTH - 1 - back:CONV_WIDTH - back, :]
        y = y + pltpu.roll(x, back, axis=0) * wk
        mixed = jnp.where(head_rows < back, pltpu.roll(prev, back, axis=0), pltpu.roll(x[:SUBLANES], back, axis=0))
        y_head = y_head + mixed * wk
    is_qk = j < 2 * GDN_WIDTH // x.shape[1]
    q_scale = jnp.where(j < GDN_WIDTH // x.shape[1], HEAD_DIM ** -0.5, 1.0)

    def finish(v):
        v = _silu(v)
        outs = []
        for h in range(v.shape[1] // HEAD_DIM):
            vh = v[:, h * HEAD_DIM:(h + 1) * HEAD_DIM]
            nrm = lax.rsqrt(jnp.sum(vh * vh, axis=-1, keepdims=True) + EPS) * q_scale
            outs.append(vh * jnp.where(is_qk, nrm, 1.0))
        return jnp.concatenate(outs, axis=1)

    o_ref[...] = finish(jnp.concatenate([y_head, y[SUBLANES:]], axis=0)).astype(o_ref.dtype)


def _gdn_conv(proj, conv_w, seq):
    t = proj.shape[0]
    tm = min(512, seq)
    cw = GDN_WIDTH
    nj = 3 * GDN_WIDTH // cw
    halo = tm // HALO_ROWS
    return pl.pallas_call(
        functools.partial(_conv_kernel, seq // tm),
        grid=(t // tm, nj),
        in_specs=[pl.BlockSpec((tm, cw), lambda i, j: (i, j)),
                  pl.BlockSpec((HALO_ROWS, cw), lambda i, j: (jnp.maximum(i * halo - 1, 0), j)),
                  pl.BlockSpec((CONV_WIDTH, cw), lambda i, j: (0, j))],
        out_specs=pl.BlockSpec((tm, cw), lambda i, j: (i, j)),
        out_shape=jax.ShapeDtypeStruct((t, 3 * GDN_WIDTH), BF16),
        compiler_params=_cparams("parallel", "parallel"), name="gdn_conv",
    )(proj, proj, conv_w.astype(F32))


def _softplus(x):
    return jnp.maximum(x, 0.0) + jnp.log(1.0 + jnp.exp(-jnp.abs(x)))


def _gdn_kernel(n_chunks, q_ref, k_ref, v_ref, z_ref, ab_ref, abt_ref, alog_ref, dtb_ref, alog_t_ref, dtb_t_ref,
                gw_ref, o_ref, state_ref):
    C = GDN_CHUNK
    nh = GDN_HEADS
    rows = n_chunks * C

    @pl.when(pl.program_id(1) == 0)
    def _():
        state_ref[...] = jnp.zeros_like(state_ref)

    neg_a = -jnp.exp(alog_ref[...])
    g_col = neg_a * _softplus(ab_ref[:, :LANES].astype(F32) + dtb_ref[...])
    g_row = -jnp.exp(alog_t_ref[...]) * _softplus(abt_ref[...] + dtb_t_ref[...])
    r_i = lax.broadcasted_iota(jnp.int32, (rows, rows), 0)
    c_i = lax.broadcasted_iota(jnp.int32, (rows, rows), 1)
    same = (r_i // C) == (c_i // C)
    tri_l = jnp.where(same & (c_i <= r_i), 1.0, 0.0).astype(F32)
    tri_u = jnp.where(same & (r_i <= c_i), 1.0, 0.0).astype(F32)
    gc_all = jnp.dot(tri_l, g_col, precision=HIGHEST, preferred_element_type=F32)
    gr_all = jnp.dot(g_row, tri_u, precision=HIGHEST, preferred_element_type=F32)
    b_all = _sigmoid(ab_ref[:, :LANES].astype(F32))

    incl = same & (r_i >= c_i)
    strict = same & (r_i > c_i)
    eye = jnp.where(r_i == c_i, 1.0, 0.0).astype(F32)
    gw = gw_ref[...]

    heads = range(nh)
    sls = [slice(h * HEAD_DIM, (h + 1) * HEAD_DIM) for h in heads]
    q = [q_ref[:, sl] for sl in sls]
    k = [k_ref[:, sl] for sl in sls]
    kf = [x.astype(F32) for x in k]
    gcol = [gc_all[:, h:h + 1] for h in heads]
    beta = [b_all[:, nh + h:nh + h + 1] for h in heads]
    decay = [jnp.exp(jnp.where(incl, gcol[h] - gr_all[h:h + 1, :], -jnp.inf)) for h in heads]
    pw = [jnp.where(strict, -(beta[h] * _dot_nt(k[h], k[h]) * decay[h]), 0.0) for h in heads]
    t_mat = [eye + p for p in pw]
    pw = [p.astype(BF16) for p in pw]
    for _ in range(int(math.log2(C)) - 1):
        pw = [_dot(p, p).astype(BF16) for p in pw]
        t_mat = [t + _dot(t.astype(BF16), p) for t, p in zip(t_mat, pw)]
    eg = [jnp.exp(g) for g in gcol]
    uw = [_dot(t_mat[h].astype(BF16),
               jnp.concatenate([v_ref[:, sls[h]].astype(F32) * beta[h], kf[h] * (beta[h] * eg[h])],
                               axis=1).astype(BF16)) for h in heads]
    u_all = [x[:, :HEAD_DIM] for x in uw]
    w_all = [x[:, HEAD_DIM:].astype(BF16) for x in uw]
    qk_all = [jnp.where(incl, _dot_nt(q[h], k[h]) * decay[h], 0.0).astype(BF16) for h in heads]
    qe_all = [(q[h].astype(F32) * eg[h]).astype(BF16) for h in heads]
    g_last = [[g[(c + 1) * C - 1:(c + 1) * C, :] for c in range(n_chunks)] for g in gcol]
    k_dec = [(kf[h] * jnp.exp(jnp.concatenate([jnp.broadcast_to(g, (C, 1)) for g in g_last[h]], axis=0)
                              - gcol[h])).astype(BF16) for h in heads]

    wq_all = [jnp.concatenate([x[c * C:(c + 1) * C] for c in range(n_chunks) for x in (w_all[h], qe_all[h])],
                              axis=0) for h in heads]

    state = [state_ref[h] for h in heads]
    for c in range(n_chunks):
        rs = slice(c * C, (c + 1) * C)
        sb = [s.astype(BF16) for s in state]
        ws_qs = [_dot(wq_all[h][2 * c * C:2 * (c + 1) * C], sb[h]) for h in heads]
        v_new = [(u_all[h][rs] - ws_qs[h][:C]).astype(BF16) for h in heads]
        o = [ws_qs[h][C:] + _dot(qk_all[h][rs, rs], v_new[h]) for h in heads]
        state = [state[h] * jnp.exp(g_last[h][c]) + _dot_tn(k_dec[h][rs], v_new[h]) for h in heads]
        for h in heads:
            gated = gw * _silu(z_ref[rs, sls[h]].astype(F32))
            o_h = o[h] * lax.rsqrt(jnp.mean(o[h] * o[h], axis=-1, keepdims=True) + EPS) * gated
            o_ref[rs, sls[h]] = o_h.astype(o_ref.dtype)
    for h in heads:
        state_ref[h] = state[h]


def _gdn(qkv, proj, bsz, seq, a_log, dt_bias, gnorm_w):
    t = bsz * seq
    n_chunks = min(4, seq // GDN_CHUNK)
    rows = n_chunks * GDN_CHUNK
    steps = seq // rows
    ab_off = SEG_U * SEG + AB_OFF
    abt = proj[:, ab_off:ab_off + 2 * SUBLANES].astype(F32).T

    def lane_vec(p):
        return jnp.zeros((1, LANES), F32).at[0, :GDN_HEADS].set(p.astype(F32))

    def sublane_vec(p):
        return jnp.zeros((2 * SUBLANES, 1), F32).at[:GDN_HEADS, 0].set(p.astype(F32))

    def rowblk(seg):
        return pl.BlockSpec((rows, SEG), lambda b, s: (b * steps + s, seg))

    return pl.pallas_call(
        functools.partial(_gdn_kernel, n_chunks),
        grid=(bsz, steps),
        in_specs=[rowblk(0), rowblk(1), rowblk(2),
                  pl.BlockSpec((rows, SEG), lambda b, s: (b * steps + s, SEG_GZ)),
                  pl.BlockSpec((rows, 2 * LANES), lambda b, s: (b * steps + s, ab_off // (2 * LANES))),
                  pl.BlockSpec((2 * SUBLANES, rows), lambda b, s: (0, b * steps + s)),
                  pl.BlockSpec((1, LANES), lambda b, s: (0, 0)),
                  pl.BlockSpec((1, LANES), lambda b, s: (0, 0)),
                  pl.BlockSpec((2 * SUBLANES, 1), lambda b, s: (0, 0)),
                  pl.BlockSpec((2 * SUBLANES, 1), lambda b, s: (0, 0)),
                  pl.BlockSpec((1, HEAD_DIM), lambda b, s: (0, 0))],
        out_specs=pl.BlockSpec((rows, GDN_WIDTH), lambda b, s: (b * steps + s, 0)),
        out_shape=jax.ShapeDtypeStruct((t, GDN_WIDTH), BF16),
        scratch_shapes=[pltpu.VMEM((GDN_HEADS, HEAD_DIM, HEAD_DIM), F32)],
        compiler_params=_cparams("parallel", "arbitrary"), name="gdn",
    )(qkv, qkv, qkv, proj, proj, abt, lane_vec(a_log), lane_vec(dt_bias),
      sublane_vec(a_log), sublane_vec(dt_bias), gnorm_w.reshape(1, HEAD_DIM).astype(F32))


def _outproj_kernel(with_router, a1_ref, a2_ref, a3_ref, w1_ref, w2_ref, w3_ref, h_ref, gate_ref,
                    nw_ref, scale_ref, shift_ref, *rest):
    if with_router:
        wr_hi_ref, wr_lo_ref, hres_ref, hn_ref, route_ref = rest
    else:
        hres_ref, hn_ref = rest
    mix = _dot(a1_ref[...], w1_ref[...]) + _dot(a2_ref[...], w2_ref[...]) + _dot(a3_ref[...], w3_ref[...])
    h = h_ref[...] + gate_ref[...] * mix
    hres_ref[...] = h
    hn = _rms_mod(h, nw_ref[...], scale_ref[...], shift_ref[...])
    hn_ref[...] = hn.astype(hn_ref.dtype)
    if with_router:
        hn_hi = hn.astype(BF16)
        hn_lo = (hn - hn_hi.astype(F32)).astype(BF16)
        logits = (_dot(hn_hi, wr_hi_ref[...]) + _dot(hn_lo, wr_hi_ref[...])) + _dot(hn_hi, wr_lo_ref[...])
        lane = lax.broadcasted_iota(jnp.int32, logits.shape, 1)
        lg = jnp.where(lane < N_EXPERTS, logits, -jnp.inf)
        m1 = jnp.max(lg, axis=-1, keepdims=True)
        i1 = jnp.min(jnp.where(lg == m1, lane, LANES), axis=-1, keepdims=True)
        lg2 = jnp.where(lane == i1, -jnp.inf, lg)
        m2 = jnp.max(lg2, axis=-1, keepdims=True)
        i2 = jnp.min(jnp.where(lg2 == m2, lane, LANES), axis=-1, keepdims=True)
        e = jnp.exp(m2 - m1)
        g1 = 1.0 / (1.0 + e)
        g2 = e / (1.0 + e)
        route_ref[...] = jnp.where(lane == 0, i1.astype(F32),
                                   jnp.where(lane == 1, i2.astype(F32),
                                             jnp.where(lane == 2, g1, jnp.where(lane == 3, g2, 0.0))))


def _outproj(a1, a2, a3, w_out, h_res, seq, gate, norm_w, scale, shift, w_router):
    t, d = h_res.shape
    tm = min(256, t)
    with_router = w_router is not None
    k1, k2 = a1.shape[1], a2.shape[1]
    w1, w2, w3 = w_out[:k1], w_out[k1:k1 + k2], w_out[k1 + k2:]

    def rows(w):
        return pl.BlockSpec((tm, w), lambda i: (i, 0))

    def whole(a):
        return pl.BlockSpec(a.shape, lambda i: (0, 0))

    per_batch = pl.BlockSpec((None, 1, d), lambda i: ((i * tm) // seq, 0, 0))
    nw = norm_w.reshape(1, d).astype(F32)
    args = [a1, a2, a3, w1, w2, w3, h_res, gate, nw, scale, shift]
    specs = [rows(k1), rows(k2), rows(a3.shape[1]), whole(w1), whole(w2), whole(w3), rows(d), per_batch,
             whole(nw), per_batch, per_batch]
    out_shape = [jax.ShapeDtypeStruct((t, d), F32), jax.ShapeDtypeStruct((t, d), BF16)]
    out_specs = [rows(d), rows(d)]
    if with_router:
        wr = jnp.zeros((d, LANES), F32).at[:, :N_EXPERTS].set(w_router.astype(F32))
        wr_hi = wr.astype(BF16)
        wr_lo = (wr - wr_hi.astype(F32)).astype(BF16)
        args += [wr_hi, wr_lo]
        specs += [whole(wr_hi), whole(wr_lo)]
        out_shape.append(jax.ShapeDtypeStruct((t, LANES), F32))
        out_specs.append(rows(LANES))
    return pl.pallas_call(
        functools.partial(_outproj_kernel, with_router),
        grid=(t // tm,), in_specs=specs, out_specs=out_specs, out_shape=out_shape,
        compiler_params=_cparams("parallel"), name="out_proj",
    )(*args)


FFN_SUB = 256


def _ffn_kernel(row_gated, fused_norm, has_mod, emit_res, te_ref, nv_ref, x_ref, w1_ref, w3_ref, w2_ref, *rest):
    rest = list(rest)
    rg_ref = rest.pop(0) if row_gated else None
    if fused_norm:
        h_ref, gate_ref, nw_ref = rest.pop(0), rest.pop(0), rest.pop(0)
        scale_ref, shift_ref = (rest.pop(0), rest.pop(0)) if has_mod else (None, None)
        hres_ref = rest.pop(0) if emit_res else None
    o_ref, acc_ref = rest
    i = pl.program_id(0)
    f = pl.program_id(1)

    @pl.when(f == 0)
    def _():
        acc_ref[...] = jnp.zeros_like(acc_ref)

    @pl.when(i < nv_ref[0])
    def _():
        x = x_ref[...]
        n_sub = w1_ref.shape[1] // FFN_SUB

        def up(s):
            cols = slice(s * FFN_SUB, (s + 1) * FFN_SUB)
            return _dot(x, w1_ref[:, cols]), _dot(x, w3_ref[:, cols])

        def down(s, h):
            return _dot((_silu(h[0]) * h[1]).astype(BF16), w2_ref[s * FFN_SUB:(s + 1) * FFN_SUB, :])

        h = up(0)
        total = acc_ref[...]
        for s in range(1, n_sub + 1):
            h_next = up(s) if s < n_sub else None
            total = total + down(s - 1, h)
            h = h_next
        acc_ref[...] = total

    @pl.when(f == pl.num_programs(1) - 1)
    def _():
        y = acc_ref[...]
        if row_gated:
            y = y * rg_ref[...]
        if fused_norm:
            hh = h_ref[...] + gate_ref[...] * y
            if emit_res:
                hres_ref[...] = hh
            scale = scale_ref[...] if has_mod else None
            shift = shift_ref[...] if has_mod else None
            y = _rms_mod(hh, nw_ref[...], scale, shift)
        o_ref[...] = y.astype(o_ref.dtype)


def _ffn(x, w1, w3, w2, tile_expert, n_valid, row_gate, tm, tf, norm=None):
    r, d = x.shape
    ff = w1.shape[2]
    tm, tf = min(tm, r), min(tf, ff)
    rows = pl.BlockSpec((tm, d), lambda i, f, te, nv: (i, 0))
    in_specs = [rows,
                pl.BlockSpec((None, d, tf), lambda i, f, te, nv: (te[i], 0, f)),
                pl.BlockSpec((None, d, tf), lambda i, f, te, nv: (te[i], 0, f)),
                pl.BlockSpec((None, tf, d), lambda i, f, te, nv: (te[i], f, 0))]
    args = [tile_expert, n_valid, x, w1, w3, w2]
    if row_gate is not None:
        in_specs.append(pl.BlockSpec((tm, 1), lambda i, f, te, nv: (i, 0)))
        args.append(row_gate)
    out_shape, out_specs = [], []
    has_mod = emit_res = False
    out_dtype = BF16
    if norm is not None:
        h_res, seq, gate, norm_w, scale, shift, emit_res, out_dtype = norm
        has_mod = scale is not None
        per_batch = pl.BlockSpec((None, 1, d), lambda i, f, te, nv: ((i * tm) // seq, 0, 0))
        in_specs += [rows, per_batch, pl.BlockSpec((1, d), lambda i, f, te, nv: (0, 0))]
        args += [h_res, gate, norm_w.reshape(1, d)]
        if has_mod:
            in_specs += [per_batch, per_batch]
            args += [scale, shift]
        if emit_res:
            out_shape.append(jax.ShapeDtypeStruct((r, d), F32))
            out_specs.append(rows)
    out_shape.append(jax.ShapeDtypeStruct((r, d), out_dtype))
    out_specs.append(rows)
    grid_spec = pltpu.PrefetchScalarGridSpec(
        num_scalar_prefetch=2, grid=(r // tm, ff // tf), in_specs=in_specs, out_specs=out_specs,
        scratch_shapes=[pltpu.VMEM((tm, d), F32)])
    res = pl.pallas_call(
        functools.partial(_ffn_kernel, row_gate is not None, norm is not None, has_mod, emit_res),
        grid_spec=grid_spec, out_shape=out_shape,
        compiler_params=_cparams("parallel", "arbitrary"), name="ffn",
    )(*args)
    return res if emit_res else res[0]


MOE_TILE = 512


def _moe_plan(route, tm):
    t = route.shape[0]
    e_flat = route[:, :2].astype(jnp.int32).reshape(-1)
    onehot = (e_flat[:, None] == jnp.arange(N_EXPERTS, dtype=jnp.int32)[None, :]).astype(jnp.int32)
    counts = jnp.sum(onehot, axis=0)
    rank = jnp.sum((jnp.cumsum(onehot, axis=0) - onehot) * onehot, axis=1)
    padded = ((counts + tm - 1) // tm) * tm
    ends = jnp.cumsum(padded)
    pos = (ends - padded)[e_flat] + rank
    n_rows = 2 * t + N_EXPERTS * tm
    n_tiles, n_blocks = n_rows // tm, t // tm
    tok_f = (jnp.arange(2 * t, dtype=jnp.int32) // 2).astype(F32)
    rows = jnp.stack([jnp.full((n_rows,), -1.0, F32), jnp.zeros((n_rows,), F32)], axis=1)
    rows = rows.at[pos].set(jnp.stack([tok_f, route[:, 2:4].reshape(-1)], axis=1))
    src_tok = rows[:, 0].astype(jnp.int32)
    row_gate = rows[:, 1]
    n_valid = (ends[-1] // tm).astype(jnp.int32)
    tile_start = jnp.arange(n_tiles, dtype=jnp.int32) * tm
    tile_expert = jnp.sum((tile_start[:, None] >= ends[None, :]).astype(jnp.int32), axis=1)
    last_expert = jnp.sum((tile_start[n_valid - 1] >= ends).astype(jnp.int32))
    tile_expert = jnp.where(tile_start < ends[-1], tile_expert, last_expert).astype(jnp.int32)

    blk = jnp.where(src_tok >= 0, src_tok // tm, -1).reshape(n_tiles, tm, 1)
    in_block = blk == jnp.arange(n_blocks, dtype=jnp.int32)[None, None, :]
    incidence = jnp.any(in_block, axis=1)
    row_sub = (jnp.arange(tm, dtype=jnp.int32) // COMBINE_SUB)[None, :, None]
    sub_lo = jnp.min(jnp.where(in_block, row_sub, tm // COMBINE_SUB), axis=1)
    sub_hi = jnp.max(jnp.where(in_block, row_sub, -1), axis=1)
    unused = (tile_start >= ends[-1])[:, None] & (jnp.arange(n_blocks) == 0)[None, :]
    w_max = n_tiles + N_EXPERTS * n_blocks + N_EXPERTS

    def work_list(m, *per_pair):
        flat = m.reshape(-1)
        n = jnp.sum(flat.astype(jnp.int32))
        idx = jnp.nonzero(flat, size=w_max, fill_value=0)[0].astype(jnp.int32)
        w = jnp.arange(w_max, dtype=jnp.int32)
        valid = w < n
        idx = jnp.where(valid, idx, idx[n - 1])
        major, minor = idx // m.shape[1], idx % m.shape[1]
        first = valid & ((w == 0) | (major != jnp.roll(major, 1)))
        last = valid & ((w == n - 1) | (major != jnp.roll(major, -1)))
        extra = [a.reshape(-1)[idx] for a in per_pair]
        return [a.astype(jnp.int32) for a in (major, minor, first, last, valid, *extra)]

    return (src_tok, row_gate.reshape(n_rows, 1), tile_expert, n_valid.reshape(1),
            work_list(incidence | unused, sub_lo, sub_hi), work_list(incidence.T, sub_lo.T, sub_hi.T))


def _dispatch_kernel(wi_ref, wj_ref, first_ref, last_ref, valid_ref, lo_ref, hi_ref, x_ref, tok_ref, o_ref):
    w = pl.program_id(0)
    tm = x_ref.shape[0]
    n_sub = tm // COMBINE_SUB
    window = 2 * COMBINE_SUB

    @pl.when(valid_ref[w] == 1)
    def _():
        @pl.when(first_ref[w] == 1)
        def _():
            o_ref[...] = jnp.zeros_like(o_ref)

        def picked(tok_col):
            col_tok = lax.broadcasted_iota(jnp.int32, (tok_col.shape[0], tm), 1) + wj_ref[w] * tm
            onehot = jnp.where(tok_col == col_tok, 1.0, 0.0).astype(BF16)
            return _dot(onehot, x_ref[...]).astype(o_ref.dtype)

        first_sub = jnp.minimum(lo_ref[w], n_sub - 2)
        narrow = hi_ref[w] <= first_sub + 1

        @pl.when(narrow)
        def _():
            rows = pl.ds(pl.multiple_of(first_sub * COMBINE_SUB, COMBINE_SUB), window)
            o_ref[rows, :] += picked(tok_ref[rows, :])

        @pl.when(jnp.logical_not(narrow))
        def _():
            o_ref[...] += picked(tok_ref[...])


def _dispatch(x, src_tok, work, tm):
    t, d = x.shape
    n_rows = src_tok.shape[0]
    grid_spec = pltpu.PrefetchScalarGridSpec(
        num_scalar_prefetch=len(work), grid=(work[0].shape[0],),
        in_specs=[pl.BlockSpec((tm, d), lambda w, wi, wj, *_: (wj[w], 0)),
                  pl.BlockSpec((tm, 1), lambda w, wi, wj, *_: (wi[w], 0))],
        out_specs=pl.BlockSpec((tm, d), lambda w, wi, wj, *_: (wi[w], 0)))
    return pl.pallas_call(
        _dispatch_kernel, grid_spec=grid_spec,
        out_shape=jax.ShapeDtypeStruct((n_rows, d), x.dtype),
        compiler_params=_cparams("arbitrary"), name="moe_dispatch",
    )(*work, x, src_tok.reshape(n_rows, 1))


COMBINE_SUB = LANES


def _combine_kernel(has_mod, emit_res, vj_ref, vi_ref, first_ref, last_ref, valid_ref, lo_ref, hi_ref,
                    y_ref, tok_ref, h_ref, gate_ref, nw_ref, *rest):
    rest = list(rest)
    scale_ref = shift_ref = None
    if has_mod:
        scale_ref, shift_ref = rest.pop(0), rest.pop(0)
    hres_ref = rest.pop(0) if emit_res else None
    o_ref, acc_ref = rest
    w = pl.program_id(0)
    tm = y_ref.shape[0]

    n_sub = tm // COMBINE_SUB
    window = 2 * COMBINE_SUB

    @pl.when(valid_ref[w] == 1)
    def _():
        @pl.when(first_ref[w] == 1)
        def _():
            acc_ref[...] = jnp.zeros_like(acc_ref)

        def gathered(tok_row, rows):
            row_tok = lax.broadcasted_iota(jnp.int32, (tm, tok_row.shape[1]), 0) + vj_ref[w] * tm
            return _dot(jnp.where(tok_row == row_tok, 1.0, 0.0).astype(BF16), rows)

        first_sub = jnp.minimum(lo_ref[w], n_sub - 2)
        narrow = hi_ref[w] <= first_sub + 1

        @pl.when(narrow)
        def _():
            toks = tok_ref[pl.ds(first_sub, 2)]
            rows = y_ref[pl.ds(pl.multiple_of(first_sub * COMBINE_SUB, COMBINE_SUB), window), :]
            acc_ref[...] += gathered(jnp.concatenate([toks[0], toks[1]], axis=1), rows)

        @pl.when(jnp.logical_not(narrow))
        def _():
            toks = tok_ref[...]
            acc_ref[...] += gathered(jnp.concatenate([toks[s] for s in range(n_sub)], axis=1), y_ref[...])

        @pl.when(last_ref[w] == 1)
        def _():
            h = h_ref[...] + gate_ref[...] * acc_ref[...]
            if emit_res:
                hres_ref[...] = h
            scale = scale_ref[...] if has_mod else None
            shift = shift_ref[...] if has_mod else None
            o_ref[...] = _rms_mod(h, nw_ref[...], scale, shift).astype(o_ref.dtype)


def _combine_resnorm(ys, src_tok, work, tm, h_res, seq, gate, norm_w, scale, shift, emit_res, out_dtype):
    t, d = h_res.shape
    n_tiles = ys.shape[0] // tm
    has_mod = scale is not None

    def tok_rows(w, vj, vi, *_):
        return (vj[w], 0)

    per_batch = pl.BlockSpec((None, 1, d), lambda w, vj, *_: ((vj[w] * tm) // seq, 0, 0))
    in_specs = [pl.BlockSpec((tm, d), lambda w, vj, vi, *_: (vi[w], 0)),
                pl.BlockSpec((None, tm // COMBINE_SUB, 1, COMBINE_SUB), lambda w, vj, vi, *_: (vi[w], 0, 0, 0)),
                pl.BlockSpec((tm, d), tok_rows), per_batch,
                pl.BlockSpec((1, d), lambda w, *_: (0, 0))]
    args = list(work) + [ys, src_tok.reshape(n_tiles, tm // COMBINE_SUB, 1, COMBINE_SUB), h_res, gate,
                         norm_w.reshape(1, d)]
    if has_mod:
        in_specs += [per_batch, per_batch]
        args += [scale, shift]
    out_shape, out_specs = [], []
    if emit_res:
        out_shape.append(jax.ShapeDtypeStruct((t, d), F32))
        out_specs.append(pl.BlockSpec((tm, d), tok_rows))
    out_shape.append(jax.ShapeDtypeStruct((t, d), out_dtype))
    out_specs.append(pl.BlockSpec((tm, d), tok_rows))
    grid_spec = pltpu.PrefetchScalarGridSpec(
        num_scalar_prefetch=len(work), grid=(work[0].shape[0],), in_specs=in_specs, out_specs=out_specs,
        scratch_shapes=[pltpu.VMEM((tm, d), F32)])
    res = pl.pallas_call(
        functools.partial(_combine_kernel, has_mod, emit_res), grid_spec=grid_spec, out_shape=out_shape,
        compiler_params=_cparams("arbitrary"), name="moe_combine",
    )(*args)
    return res if emit_res else res[0]


def kernel(x, c, positions, w_ada, b_ada, norm_mix, norm_ffn, norm_final, w_in, w_out, ssm_a_re, ssm_a_im, ssm_log_dt, ssm_b_re, ssm_b_im, ssm_c_re, ssm_c_im, ssm_d, ssm_w_glu, diff_lam_q1, diff_lam_k1, diff_lam_q2, diff_lam_k2, diff_subln, gdn_conv, gdn_a_log, gdn_dt_bias, gdn_norm, ffn_w1, ffn_w3, ffn_w2, moe_router, moe_w1, moe_w3, moe_w2):
    bsz, seq, d = x.shape
    t = bsz * seq
    depth = w_in.shape[0]
    h_res = x.astype(F32).reshape(t, d)

    c_pad = jnp.zeros((SUBLANES, d), F32).at[:bsz].set(c.astype(F32))
    mod = _ada(c_pad, w_ada, b_ada)[:, :bsz]
    mods = [[m.reshape(bsz, 1, d) for m in jnp.split(mod[l], 6, axis=-1)] for l in range(depth)]
    rope_tables = _rope_tables(positions)
    scan_steps = max(1, math.ceil(math.log2(seq // SSM_CHUNK)))
    s5_tables = jax.vmap(functools.partial(_s5_tables, n_steps=scan_steps))(
        ssm_a_re, ssm_a_im, ssm_log_dt, ssm_b_re, ssm_b_im, ssm_c_re, ssm_c_im)

    hn = _resnorm(h_res, seq, None, None, norm_mix[0].astype(F32), mods[0][1], mods[0][0], False, BF16)
    out = None
    moe_bf16 = []
    for l in range(depth):
        shift1, scale1, gate1, shift2, scale2, gate2 = mods[l]
        is_moe = l % 2 == 1
        proj = _in_proj(hn, w_in[l])

        y_ssm = _s5_mixer(proj, bsz, seq, [tab[l] for tab in s5_tables], ssm_d[l], ssm_w_glu[l].astype(BF16))

        lambda_init = 0.8 - 0.6 * math.exp(-0.3 * l)
        qk = _rope(proj, rope_tables)
        if is_moe:
            to_cast = (w_out[l:l + 1], moe_w2[l // 2])
        else:
            to_cast = (w_out[l:l + 1], ffn_w1[l // 2:l // 2 + 1], ffn_w3[l // 2:l // 2 + 1], ffn_w2[l // 2:l // 2 + 1])
            if l + 1 < depth:
                to_cast += (moe_w1[(l + 1) // 2], moe_w3[(l + 1) // 2])
        y_diff, copies = _diff_attention(qk, proj, bsz, seq,
                                         (diff_lam_q1[l], diff_lam_k1[l], diff_lam_q2[l], diff_lam_k2[l]),
                                         diff_subln[l], lambda_init, to_cast)
        w_out_bf16 = copies[0][0]
        if is_moe:
            moe_bf16 = moe_bf16 + copies[1:]
        else:
            ffn_bf16, moe_bf16 = copies[1:4], copies[4:]

        qkv = _gdn_conv(proj, gdn_conv[l], seq)
        y_gdn = _gdn(qkv, proj, bsz, seq, gdn_a_log[l], gdn_dt_bias[l], gdn_norm[l])

        res = _outproj(y_ssm, y_diff, y_gdn, w_out_bf16, h_res, seq, gate1,
                       norm_ffn[l], scale2, shift2, moe_router[l // 2] if is_moe else None)
        h_res, hn2 = res[0], res[1]

        last = l + 1 == depth
        if last:
            nxt = (norm_final.astype(F32), None, None, False, x.dtype)
        else:
            nxt = (norm_mix[l + 1].astype(F32), mods[l + 1][1], mods[l + 1][0], True, BF16)
        if is_moe:
            src_tok, row_gate, tile_expert, n_valid, work_sorted, work_token = _moe_plan(res[2], MOE_TILE)
            xs = _dispatch(hn2, src_tok, work_sorted, MOE_TILE)
            ys = _ffn(xs, *moe_bf16, tile_expert, n_valid, row_gate, MOE_TILE, 1024)
            res = _combine_resnorm(ys, src_tok, work_token, MOE_TILE, h_res, seq, gate2, *nxt)
        else:
            n_tiles = t // min(512, t)
            res = _ffn(hn2, *ffn_bf16, jnp.zeros((n_tiles,), jnp.int32),
                       jnp.full((1,), n_tiles, jnp.int32), None, 512, 512,
                       norm=(h_res, seq, gate2) + nxt)
        if last:
            out = res
        else:
            h_res, hn = res
    return out.reshape(bsz, seq, d)
```
